```python
import math
import jax, jax.numpy as jnp
from jax import lax
import numpy as np

D_MODEL = 1024
BATCH = 16
SEQ = 2048
DEPTH = 1

HEAD_DIM = 64
D_MIX = D_MODEL
N_HEADS_A = 8
N_HEADS_B = 8
WIDTH_A = N_HEADS_A * HEAD_DIM
WIDTH_B = N_HEADS_B * HEAD_DIM
ROT_DIM = HEAD_DIM // 4
ROPE_THETA = 500000.0
DILATED_PATTERNS = ((128, 1), (512, 4), (2048, 16))
WIN_BLOCK = 128
KV_LATENT = 128
IDX_HEADS = 4
IDX_DIM = 64
TOPK_MAX = 256
Q_BLOCK = 128
EPS = 1e-6
IN_SPLITS = (WIDTH_A, WIDTH_A, WIDTH_A, WIDTH_A,
             WIDTH_B, KV_LATENT, WIDTH_B,
             IDX_HEADS * IDX_DIM, IDX_DIM, IDX_HEADS)
D_IN = sum(IN_SPLITS)

kernel_name = "hybrid_dilated_window_dsa_parallel_heads"


def rmsnorm(x, g):
    xf = x.astype(jnp.float32)
    y = xf * lax.rsqrt(jnp.mean(xf * xf, axis=-1, keepdims=True) + EPS)
    return (y * g.astype(jnp.float32)).astype(x.dtype)


def rope_tables(positions):
    inv_freq = ROPE_THETA ** (-jnp.arange(0, ROT_DIM, 2, dtype=jnp.float32) / ROT_DIM)
    ang = positions.astype(jnp.float32)[..., None] * inv_freq
    return jnp.cos(ang), jnp.sin(ang)


def apply_rope(x, cos, sin):
    half = ROT_DIM // 2
    xr = x[..., :ROT_DIM].astype(jnp.float32)
    x1, x2 = xr[..., :half], xr[..., half:]
    rot = jnp.concatenate([x1 * cos - x2 * sin, x2 * cos + x1 * sin], axis=-1)
    return jnp.concatenate([rot.astype(x.dtype), x[..., ROT_DIM:]], axis=-1)


def split_columns(proj):
    offs = [int(v) for v in np.cumsum(IN_SPLITS)[:-1]]
    return jnp.split(proj, offs, axis=-1)


def dilated_window_pattern(q, k, v, dilation, steps):
    B, S, H, D = q.shape
    n = S // dilation
    nb = -(-n // WIN_BLOCK)
    npad = nb * WIN_BLOCK

    def to_sub(t):
        t = t.reshape(B, n, dilation, H, D)
        return jnp.pad(t, ((0, 0), (0, npad - n), (0, 0), (0, 0), (0, 0)))

    def band(t):
        tp = jnp.pad(t, ((0, 0), (WIN_BLOCK, 0), (0, 0), (0, 0), (0, 0)))
        tp = tp.reshape(B, nb + 1, WIN_BLOCK, dilation, H, D)
        return jnp.concatenate([tp[:, :-1], tp[:, 1:]], axis=2)

    qb = to_sub(q).reshape(B, nb, WIN_BLOCK, dilation, H, D)
    kb = band(to_sub(k))
    vb = band(to_sub(v))
    s = jnp.einsum('bnqrhd,bnkrhd->bnrhqk', qb, kb).astype(jnp.float32) * (D ** -0.5)
    qi = jnp.arange(WIN_BLOCK)[:, None]
    kj = jnp.arange(2 * WIN_BLOCK)[None, :]
    diff = qi + WIN_BLOCK - kj
    key_step = jnp.arange(nb)[:, None, None] * WIN_BLOCK - WIN_BLOCK + kj[None]
    mask = (diff >= 0) & (diff <= steps) & (key_step >= 0)
    s = jnp.where(mask[None, :, None, None], s, -jnp.inf)
    m = jnp.max(s, axis=-1, keepdims=True)
    p = jnp.exp(s - m)
    l = jnp.sum(p, axis=-1, keepdims=True)
    o = jnp.einsum('bnrhqk,bnkrhd->bnqrhd', (p / l).astype(v.dtype), vb)
    lse = (m + jnp.log(l))[..., 0]
    o = o.reshape(B, npad, dilation, H, D)[:, :n].reshape(B, S, H, D)
    lse = lse.transpose(0, 1, 4, 2, 3).reshape(B, npad, dilation, H)[:, :n].reshape(B, S, H)
    return o, lse


def dilated_mixture(q, k, v):
    outs, lses = [], []
    for window, dilation in DILATED_PATTERNS:
        o, lse = dilated_window_pattern(q, k, v, dilation, window // dilation)
        outs.append(o)
        lses.append(lse)
    alpha = jax.nn.softmax(jnp.stack(lses, axis=-1), axis=-1)
    o = jnp.stack(outs, axis=-1).astype(jnp.float32)
    return jnp.einsum('bshdp,bshp->bshd', o, alpha).astype(q.dtype)


def indexed_sparse_attention(q, k, latent, iq, ik, iw):
    B, S, HB, D = q.shape
    n_blocks = S // Q_BLOCK
    topk = min(TOPK_MAX, S // 4)
    b_idx = jnp.arange(B)[:, None, None]
    key_pos = jnp.arange(S)

    def to_blocks(t):
        return jnp.swapaxes(t.reshape((B, n_blocks, Q_BLOCK) + t.shape[2:]), 0, 1)

    def block_fn(args):
        qb, iqb, iwb, start = args
        t = start + jnp.arange(Q_BLOCK)
        sc_idx = jnp.einsum('bqhd,bsd->bqhs', iqb, ik).astype(jnp.float32)
        index_score = jnp.einsum('bqhs,bqh->bqs', jax.nn.relu(sc_idx), iwb.astype(jnp.float32))
        admissible = key_pos[None, None, :] <= t[None, :, None]
        index_score = jnp.where(admissible, index_score, -jnp.inf)
        _, sel = lax.top_k(index_score, topk)
        valid = sel <= t[None, :, None]
        k_sel = k[b_idx, sel]
        v_sel = latent[b_idx, sel]
        s = jnp.einsum('bqhd,bqkd->bqhk', qb, k_sel).astype(jnp.float32) * (D ** -0.5)
        s = jnp.where(valid[:, :, None, :], s, -jnp.inf)
        p = jax.nn.softmax(s, axis=-1).astype(v_sel.dtype)
        return jnp.einsum('bqhk,bqkc->bqhc', p, v_sel)

    starts = jnp.arange(n_blocks, dtype=jnp.int32) * Q_BLOCK
    out = lax.map(block_fn, (to_blocks(q), to_blocks(iq), to_blocks(iw), starts))
    return jnp.swapaxes(out, 0, 1).reshape(B, S, HB, latent.shape[-1])


def setup_inputs(seed: int = 0) -> dict:
    key = jax.random.key(seed)
    ks = jax.random.split(key, 12)
    x = jax.random.normal(ks[0], (BATCH, SEQ, D_MODEL), jnp.float32)
    start = jax.random.randint(ks[1], (BATCH, 1), 0, 4096, dtype=jnp.int32)
    positions = (start + jnp.arange(SEQ, dtype=jnp.int32)[None, :]).astype(jnp.int32)

    def gain(k, n):
        return 1.0 + 0.01 * jax.random.normal(k, (DEPTH, n), jnp.float32)

    return {
        "x": x,
        "positions": positions,
        "norm_in_gain": gain(ks[2], D_MODEL),
        "w_in": jax.random.normal(ks[3], (DEPTH, D_MODEL, D_IN), jnp.float32) * D_MODEL ** -0.5,
        "kv_norm_gain": gain(ks[4], KV_LATENT),
        "w_uk": jax.random.normal(ks[5], (DEPTH, KV_LATENT, HEAD_DIM), jnp.float32) * KV_LATENT ** -0.5,
        "w_uv": jax.random.normal(ks[6], (DEPTH, N_HEADS_B, KV_LATENT, HEAD_DIM), jnp.float32) * KV_LATENT ** -0.5,
        "idx_k_norm_gain": gain(ks[7], IDX_DIM),
        "branch_norm_gain_a": gain(ks[8], WIDTH_A),
        "branch_norm_gain_b": gain(ks[9], WIDTH_B),
        "w_out": jax.random.normal(ks[10], (DEPTH, D_MIX, D_MODEL), jnp.float32) * D_MIX ** -0.5,
        "final_norm_gain": 1.0 + 0.01 * jax.random.normal(ks[11], (D_MODEL,), jnp.float32),
    }


def reference(x, positions, norm_in_gain, w_in, kv_norm_gain, w_uk, w_uv, idx_k_norm_gain,
              branch_norm_gain_a, branch_norm_gain_b, w_out, final_norm_gain):
    B, S, _ = x.shape
    cos, sin = rope_tables(positions)
    cos_h, sin_h = cos[:, :, None, :], sin[:, :, None, :]
    h = x
    for layer in range(DEPTH):
        u = rmsnorm(h, norm_in_gain[layer])
        proj = u @ w_in[layer]
        qa, ka, va, za, qb, ckv, zb, iq, ik, iw = split_columns(proj)

        qa = apply_rope(qa.reshape(B, S, N_HEADS_A, HEAD_DIM), cos_h, sin_h)
        ka = apply_rope(ka.reshape(B, S, N_HEADS_A, HEAD_DIM), cos_h, sin_h)
        va = va.reshape(B, S, N_HEADS_A, HEAD_DIM)
        o_a = dilated_mixture(qa, ka, va).reshape(B, S, WIDTH_A)
        y_a = rmsnorm(o_a * jax.nn.silu(za), branch_norm_gain_a[layer])

        qb = apply_rope(qb.reshape(B, S, N_HEADS_B, HEAD_DIM), cos_h, sin_h)
        ckv = rmsnorm(ckv, kv_norm_gain[layer])
        kb = apply_rope(ckv @ w_uk[layer], cos, sin)
        iq = apply_rope(iq.reshape(B, S, IDX_HEADS, IDX_DIM), cos_h, sin_h)
        ik = apply_rope(rmsnorm(ik, idx_k_norm_gain[layer]), cos, sin)
        iw = iw * (IDX_HEADS ** -0.5 * IDX_DIM ** -0.5)
        o_lat = indexed_sparse_attention(qb, kb, ckv, iq, ik, iw)
        o_b = jnp.einsum('bshc,hcd->bshd', o_lat, w_uv[layer]).reshape(B, S, WIDTH_B)
        y_b = rmsnorm(o_b * jax.nn.silu(zb), branch_norm_gain_b[layer])

        h = h + jnp.concatenate([y_a, y_b], axis=-1) @ w_out[layer]
    return rmsnorm(h, final_norm_gain)
```

```python
import functools

import numpy as np
import jax
import jax.numpy as jnp
from jax import lax
from jax.experimental import pallas as pl
from jax.experimental.pallas import tpu as pltpu

F32 = jnp.float32
BF16 = jnp.bfloat16
I32 = jnp.int32

HEAD_DIM = 64
N_HEADS = 8
WIDTH = N_HEADS * HEAD_DIM
N_PAIRS = N_HEADS // 2
ROT_DIM = HEAD_DIM // 4
ROPE_THETA = 500000.0
KV_LATENT = 128
IDX_HEADS = 4
IDX_DIM = 64
TOPK_MAX = 256
EPS = 1e-6
PATTERNS = ((128, 1), (512, 4), (2048, 16))

LANES = 128
KBLK = 128
NEG_BIG = -1e30
VMEM_LIMIT = 56 * 1024 * 1024

D_IN_PACKED = 28 * LANES


def _dot(a, b):
    return jnp.dot(a, b, preferred_element_type=F32)


def _dot_nt(a, b):
    return lax.dot_general(a, b, (((1,), (1,)), ((), ())), preferred_element_type=F32)


def _proj_kernel(x_ref, gin_ref, w_ref, cos_ref, sina_ref, sinb_ref, gkv_ref, wuk_ref, gik_ref,
                 qa_ref, ka_ref, vat_ref, zat_ref, qb_ref, zbt_ref, iq_ref, kbd_ref, ckvt_ref,
                 ikd_ref, iwt_ref):
    tm = x_ref.shape[1]
    nblk = tm // KBLK
    x = x_ref[0]
    ms = jnp.mean(x * x, axis=-1, keepdims=True)
    u = (x * lax.rsqrt(ms + EPS) * gin_ref[...]).astype(BF16)
    cosf, sina, sinb = cos_ref[0], sina_ref[0], sinb_ref[0]

    def rope(t):
        return t * cosf + pltpu.roll(t, LANES - ROT_DIM // 2, 1) * sina + pltpu.roll(t, ROT_DIM // 2, 1) * sinb

    def proj(unit):
        return _dot(u, w_ref[:, unit * LANES:(unit + 2) * LANES])

    def halves(acc):
        return acc[:, :LANES], acc[:, LANES:]

    def store_t(ref, unit, t):
        tt = t.T.astype(BF16)
        for i in range(nblk):
            ref[0, i, unit * LANES:(unit + 1) * LANES, :] = tt[:, i * KBLK:(i + 1) * KBLK]

    for pair in range(2):
        for half, t in enumerate(halves(proj(2 * pair))):
            c = (2 * pair + half) * LANES
            qa_ref[0, :, c:c + LANES] = (rope(t) * (HEAD_DIM ** -0.5)).astype(BF16)
        for half, t in enumerate(halves(proj(4 + 2 * pair))):
            c = (2 * pair + half) * LANES
            ka_ref[0, :, c:c + LANES] = rope(t).astype(BF16)
        for half, t in enumerate(halves(proj(8 + 2 * pair))):
            store_t(vat_ref, 2 * pair + half, t)
        for half, t in enumerate(halves(proj(12 + 2 * pair))):
            c = (2 * pair + half) * LANES
            zat_ref[0, c:c + LANES, :] = t.T.astype(BF16)
        for half, t in enumerate(halves(proj(16 + 2 * pair))):
            c = (2 * pair + half) * LANES
            qb_ref[0, :, c:c + LANES] = (rope(t) * (HEAD_DIM ** -0.5)).astype(BF16)
        for half, t in enumerate(halves(proj(20 + 2 * pair))):
            c = (2 * pair + half) * LANES
            zbt_ref[0, c:c + LANES, :] = t.T.astype(BF16)
    for half, t in enumerate(halves(proj(24))):
        iq_ref[0, :, half * LANES:(half + 1) * LANES] = rope(t).astype(BF16)

    ckv, last = halves(proj(26))
    ckv = ckv * lax.rsqrt(jnp.mean(ckv * ckv, axis=-1, keepdims=True) + EPS) * gkv_ref[...]
    store_t(ckvt_ref, 0, ckv)
    kbd_ref[0] = rope(_dot(ckv.astype(BF16), wuk_ref[...])).astype(BF16)
    lane = lax.broadcasted_iota(I32, last.shape, 1)
    ik_ms = jnp.sum(jnp.where(lane < IDX_DIM, last * last, 0.0), axis=-1, keepdims=True) / IDX_DIM
    ik = rope(last * lax.rsqrt(ik_ms + EPS) * gik_ref[...])
    ikd_ref[0] = (ik + pltpu.roll(ik, IDX_DIM, 1)).astype(BF16)
    iwt_ref[0] = last.T[IDX_DIM:IDX_DIM + 8, :] * (IDX_HEADS ** -0.5 * IDX_DIM ** -0.5)


def _mix_a_bias_tables(tq, tk):
    o_min = -(tq - KBLK)
    o_last = PATTERNS[1][0] + tk - KBLK
    offsets = list(range(o_min, o_last + 1, KBLK)) + [o_last + KBLK]
    qi = np.arange(tq)[None, :]
    kj = np.arange(tk)[:, None]
    tables = []
    for o in offsets:
        d = o + qi - kj
        mult = np.zeros(d.shape, np.int64)
        for window, dil in PATTERNS:
            mult += ((d >= 0) & (d % dil == 0) & (d <= window)).astype(np.int64)
        tables.append(np.where(mult > 0, np.log(np.maximum(mult, 1)), NEG_BIG))
    return np.stack(tables).astype(np.float32), o_min


def _mix_a_kernel(o_min, n_tables, qa_ref, ka_ref, vat_ref, zat_ref, gain_ref, bias_ref,
                  out_ref, g_ref):
    tq = qa_ref.shape[1]
    j = pl.program_id(1)
    n_kb = (j * tq + tq - 1) // KBLK + 1
    lane = lax.broadcasted_iota(I32, (tq, LANES), 1)

    for p in range(N_PAIRS):
        qp = qa_ref[0, :, p * LANES:(p + 1) * LANES]
        zero = jnp.zeros_like(qp)
        qm = jnp.concatenate([jnp.where(lane < HEAD_DIM, qp, zero),
                              jnp.where(lane >= HEAD_DIM, qp, zero)], axis=0)

        def body(kb, carry, p=p, qm=qm):
            m, l, acc0, acc1 = carry
            k0 = pl.multiple_of(kb * KBLK, KBLK)
            kblk = ka_ref[0, pl.ds(k0, KBLK), p * LANES:(p + 1) * LANES]
            s = _dot_nt(kblk, qm)
            bi = jnp.minimum((j * tq - kb * KBLK - o_min) // KBLK, n_tables - 1)
            bias = bias_ref[bi]
            s = s + jnp.concatenate([bias, bias], axis=1)
            m_new = jnp.maximum(m, jnp.max(s, axis=0, keepdims=True))
            alpha = jnp.exp(m - m_new)
            pt = jnp.exp(s - m_new)
            l = alpha * l + jnp.sum(pt, axis=0, keepdims=True)
            pt = pt.astype(BF16)
            vt = vat_ref[0, kb, p * LANES:(p + 1) * LANES, :]
            acc0 = acc0 * alpha[:, :tq] + _dot(vt[:HEAD_DIM], pt[:, :tq])
            acc1 = acc1 * alpha[:, tq:] + _dot(vt[HEAD_DIM:], pt[:, tq:])
            return m_new, l, acc0, acc1

        init = (jnp.full((1, 2 * tq), NEG_BIG, F32), jnp.zeros((1, 2 * tq), F32),
                jnp.zeros((HEAD_DIM, tq), F32), jnp.zeros((HEAD_DIM, tq), F32))
        _, l, acc0, acc1 = lax.fori_loop(0, n_kb, body, init)
        g_ref[p * LANES:p * LANES + HEAD_DIM, :] = acc0 / l[:, :tq]
        g_ref[p * LANES + HEAD_DIM:(p + 1) * LANES, :] = acc1 / l[:, tq:]

    z = zat_ref[0].astype(F32)
    g = g_ref[...] * (z / (1.0 + jnp.exp(-z)))
    ms = jnp.mean(g * g, axis=0, keepdims=True)
    out_ref[0] = (g * lax.rsqrt(ms + EPS) * gain_ref[...]).astype(BF16)


def _mix_b_kernel(topk, qb_ref, iq_ref, iwt_ref, zbt_ref, ikd_ref, kbd_ref, ckvt_ref, wuvt_ref,
                  gain_ref, out_ref, sc_ref, acc_ref, g_ref):
    tq = qb_ref.shape[1]
    seq = ikd_ref.shape[1]
    j = pl.program_id(1)
    n_kb = (j * tq + tq - 1) // KBLK + 1
    lane = lax.broadcasted_iota(I32, (tq, LANES), 1)
    row = lax.broadcasted_iota(I32, (KBLK, tq), 0)
    t_pos = j * tq + lax.broadcasted_iota(I32, (KBLK, tq), 1)

    def masked_heads(ref, n_pairs):
        parts = []
        for p in range(n_pairs):
            xp = ref[0, :, p * LANES:(p + 1) * LANES]
            zero = jnp.zeros_like(xp)
            parts += [jnp.where(lane < HEAD_DIM, xp, zero), jnp.where(lane >= HEAD_DIM, xp, zero)]
        return jnp.concatenate(parts, axis=0)

    iqm = masked_heads(iq_ref, IDX_HEADS // 2)

    def score_body(kb, carry):
        k0 = pl.multiple_of(kb * KBLK, KBLK)
        s = jnp.maximum(_dot_nt(ikd_ref[0, pl.ds(k0, KBLK), :], iqm), 0.0)
        sc = s[:, :tq] * iwt_ref[0, 0:1, :]
        for h in range(1, IDX_HEADS):
            sc = sc + s[:, h * tq:(h + 1) * tq] * iwt_ref[0, h:h + 1, :]
        sc_ref[kb] = jnp.where(kb * KBLK + row <= t_pos, sc, -jnp.inf)
        return carry

    lax.fori_loop(0, n_kb, score_body, 0)

    def count(pred):
        def body(kb, cnt):
            return cnt + jnp.sum(pred(sc_ref[kb], kb).astype(I32), axis=0, keepdims=True)
        return lax.fori_loop(0, n_kb, body, jnp.zeros((1, tq), I32))

    def key_to_float(key):
        bits = key ^ ((key >> 31) & jnp.int32(0x7FFFFFFF))
        return lax.bitcast_convert_type(bits, F32)

    def value_step(i, prefix):
        cand = prefix + (jnp.int32(1) << (31 - i))
        thr = key_to_float(cand)
        cnt = count(lambda s, kb: s >= thr)
        return jnp.where(cnt >= topk, cand, prefix)

    prefix = lax.fori_loop(0, 32, value_step, jnp.full((1, tq), -2 ** 31, I32))
    thr = key_to_float(prefix)
    need = topk - count(lambda s, kb: s > thr)

    n_idx_bits = max(1, (seq - 1).bit_length())

    def tie_step(i, cut):
        cand = cut + (jnp.int32(1) << (n_idx_bits - 1 - i))
        cnt = count(lambda s, kb: (s == thr) & (kb * KBLK + row < cand))
        return jnp.where(cnt < need, cand, cut)

    cut = lax.fori_loop(0, n_idx_bits, tie_step, jnp.zeros((1, tq), I32))

    qm = masked_heads(qb_ref, N_PAIRS)
    acc_ref[...] = jnp.zeros_like(acc_ref)

    def attn_body(kb, carry):
        m, l = carry
        k0 = pl.multiple_of(kb * KBLK, KBLK)
        sc = sc_ref[kb]
        kidx = kb * KBLK + row
        sel = (sc > thr) | ((sc == thr) & (kidx <= cut)) | (t_pos < topk)
        bias = jnp.where(sel & (kidx <= t_pos), 0.0, NEG_BIG)
        s = _dot_nt(kbd_ref[0, pl.ds(k0, KBLK), :], qm)
        s = s + jnp.concatenate([bias] * N_HEADS, axis=1)
        m_new = jnp.maximum(m, jnp.max(s, axis=0, keepdims=True))
        alpha = jnp.exp(m - m_new)
        pt = jnp.exp(s - m_new)
        l = alpha * l + jnp.sum(pt, axis=0, keepdims=True)
        acc_ref[...] = acc_ref[...] * alpha + _dot(ckvt_ref[0, kb], pt.astype(BF16))
        return m_new, l

    init = (jnp.full((1, N_HEADS * tq), NEG_BIG, F32), jnp.zeros((1, N_HEADS * tq), F32))
    _, l = lax.fori_loop(0, n_kb, attn_body, init)
    o_lat = (acc_ref[...] / l).astype(BF16)
    for h in range(N_HEADS):
        g_ref[h * HEAD_DIM:(h + 1) * HEAD_DIM, :] = _dot(wuvt_ref[h], o_lat[:, h * tq:(h + 1) * tq])

    z = zbt_ref[0].astype(F32)
    g = g_ref[...] * (z / (1.0 + jnp.exp(-z)))
    ms = jnp.mean(g * g, axis=0, keepdims=True)
    out_ref[0] = (g * lax.rsqrt(ms + EPS) * gain_ref[...]).astype(BF16)


def _out_kernel(yat_ref, ybt_ref, x_ref, w_ref, gain_ref, out_ref):
    y = jnp.concatenate([yat_ref[0].astype(F32).T, ybt_ref[0].astype(F32).T], axis=1).astype(BF16)
    h = x_ref[0] + _dot(y, w_ref[...])
    ms = jnp.mean(h * h, axis=-1, keepdims=True)
    out_ref[0] = h * lax.rsqrt(ms + EPS) * gain_ref[...]


def _rope_tables(positions):
    half = ROT_DIM // 2
    inv_freq = ROPE_THETA ** (-jnp.arange(0, ROT_DIM, 2, dtype=F32) / ROT_DIM)
    ang = positions.astype(F32)[..., None] * inv_freq
    cos, sin = jnp.cos(ang), jnp.sin(ang)
    c = np.arange(LANES) % HEAD_DIM
    sel = c % half
    cosf = jnp.where(c < ROT_DIM, cos[..., sel], 1.0)
    sina = jnp.where(c < half, -sin[..., sel], 0.0)
    sinb = jnp.where((c >= half) & (c < ROT_DIM), sin[..., sel], 0.0)
    return cosf, sina, sinb


def _pack_w_in(w):
    offs = np.cumsum([0, WIDTH, WIDTH, WIDTH, WIDTH, WIDTH, KV_LATENT, WIDTH,
                      IDX_HEADS * IDX_DIM, IDX_DIM, IDX_HEADS])
    qa, ka, va, za, qb, ckv, zb, iq, ik, iw = [w[:, offs[i]:offs[i + 1]] for i in range(10)]
    pad = jnp.zeros((w.shape[0], LANES - IDX_DIM - IDX_HEADS), w.dtype)
    return jnp.concatenate([qa, ka, va, za, qb, zb, iq, ckv, ik, iw, pad], axis=1)


def _full(shape):
    return pl.BlockSpec(shape, lambda *_: (0,) * len(shape))


@jax.jit
def kernel(x, positions, norm_in_gain, w_in, kv_norm_gain, w_uk, w_uv, idx_k_norm_gain,
           branch_norm_gain_a, branch_norm_gain_b, w_out, final_norm_gain):
    bsz, seq, d_model = x.shape
    assert seq % KBLK == 0 and seq <= PATTERNS[-1][0]
    n_blk = seq // KBLK
    topk = min(TOPK_MAX, seq // 4)
    tm = min(512, seq)
    tq = 128
    params = pltpu.CompilerParams(dimension_semantics=("arbitrary", "arbitrary"),
                                  vmem_limit_bytes=VMEM_LIMIT)

    assert w_in.shape[0] == 1
    h = x
    for layer in range(1):
        cosf, sina, sinb = _rope_tables(positions)
        w_packed = _pack_w_in(w_in[layer]).astype(BF16)
        wuk_dup = jnp.concatenate([w_uk[layer], w_uk[layer]], axis=1).astype(BF16)
        gik = jnp.concatenate([idx_k_norm_gain[layer], jnp.zeros((LANES - IDX_DIM,), F32)])[None]
        wuv_t = jnp.swapaxes(w_uv[layer], 1, 2).astype(BF16)
        gain_a = jnp.broadcast_to(branch_norm_gain_a[layer][:, None], (WIDTH, tq))
        gain_b = jnp.broadcast_to(branch_norm_gain_b[layer][:, None], (WIDTH, tq))

        row_blk = lambda c: pl.BlockSpec((1, tm, c), lambda b, i: (b, i, 0))
        col_blk = lambda c: pl.BlockSpec((1, c, tm), lambda b, i: (b, 0, i))
        key_blk = lambda c: pl.BlockSpec((1, tm // KBLK, c, KBLK), lambda b, i: (b, i, 0, 0))
        qa, ka, vat, zat, qb, zbt, iq, kbd, ckvt, ikd, iwt = pl.pallas_call(
            _proj_kernel,
            grid=(bsz, seq // tm),
            in_specs=[row_blk(d_model), _full((1, d_model)), _full((d_model, D_IN_PACKED)),
                      row_blk(LANES), row_blk(LANES), row_blk(LANES),
                      _full((1, KV_LATENT)), _full((KV_LATENT, LANES)), _full((1, LANES))],
            out_specs=[row_blk(WIDTH), row_blk(WIDTH), key_blk(WIDTH), col_blk(WIDTH),
                       row_blk(WIDTH), col_blk(WIDTH), row_blk(2 * LANES), row_blk(LANES),
                       key_blk(KV_LATENT), row_blk(LANES), col_blk(8)],
            out_shape=[jax.ShapeDtypeStruct((bsz, seq, WIDTH), BF16),
                       jax.ShapeDtypeStruct((bsz, seq, WIDTH), BF16),
                       jax.ShapeDtypeStruct((bsz, n_blk, WIDTH, KBLK), BF16),
                       jax.ShapeDtypeStruct((bsz, WIDTH, seq), BF16),
                       jax.ShapeDtypeStruct((bsz, seq, WIDTH), BF16),
                       jax.ShapeDtypeStruct((bsz, WIDTH, seq), BF16),
                       jax.ShapeDtypeStruct((bsz, seq, 2 * LANES), BF16),
                       jax.ShapeDtypeStruct((bsz, seq, LANES), BF16),
                       jax.ShapeDtypeStruct((bsz, n_blk, KV_LATENT, KBLK), BF16),
                       jax.ShapeDtypeStruct((bsz, seq, LANES), BF16),
                       jax.ShapeDtypeStruct((bsz, 8, seq), F32)],
            compiler_params=params, name="proj",
        )(h, norm_in_gain[layer][None], w_packed, cosf, sina, sinb,
          kv_norm_gain[layer][None], wuk_dup, gik)

        q_row = lambda c: pl.BlockSpec((1, tq, c), lambda b, i: (b, i, 0))
        q_col = lambda c: pl.BlockSpec((1, c, tq), lambda b, i: (b, 0, i))
        per_b3 = lambda s1, s2: pl.BlockSpec((1, s1, s2), lambda b, i: (b, 0, 0))
        per_b4 = lambda s1, s2, s3: pl.BlockSpec((1, s1, s2, s3), lambda b, i: (b, 0, 0, 0))

        bias_np, o_min = _mix_a_bias_tables(tq, KBLK)
        yat = pl.pallas_call(
            functools.partial(_mix_a_kernel, o_min, bias_np.shape[0]),
            grid=(bsz, seq // tq),
            in_specs=[q_row(WIDTH), per_b3(seq, WIDTH), per_b4(n_blk, WIDTH, KBLK), q_col(WIDTH),
                      _full((WIDTH, tq)), _full(bias_np.shape)],
            out_specs=q_col(WIDTH),
            out_shape=jax.ShapeDtypeStruct((bsz, WIDTH, seq), BF16),
            scratch_shapes=[pltpu.VMEM((WIDTH, tq), F32)],
            compiler_params=params, name="mix_a",
        )(qa, ka, vat, zat, gain_a, jnp.asarray(bias_np))

        ybt = pl.pallas_call(
            functools.partial(_mix_b_kernel, topk),
            grid=(bsz, seq // tq),
            in_specs=[q_row(WIDTH), q_row(2 * LANES), q_col(8), q_col(WIDTH),
                      per_b3(seq, LANES), per_b3(seq, LANES), per_b4(n_blk, KV_LATENT, KBLK),
                      _full((N_HEADS, HEAD_DIM, KV_LATENT)), _full((WIDTH, tq))],
            out_specs=q_col(WIDTH),
            out_shape=jax.ShapeDtypeStruct((bsz, WIDTH, seq), BF16),
            scratch_shapes=[pltpu.VMEM((n_blk, KBLK, tq), F32),
                            pltpu.VMEM((KV_LATENT, N_HEADS * tq), F32),
                            pltpu.VMEM((WIDTH, tq), F32)],
            compiler_params=params, name="mix_b",
        )(qb, iq, iwt, zbt, ikd, kbd, ckvt, wuv_t, gain_b)

        h = pl.pallas_call(
            _out_kernel,
            grid=(bsz, seq // tm),
            in_specs=[col_blk(WIDTH), col_blk(WIDTH), row_blk(d_model),
                      _full((2 * WIDTH, d_model)), _full((1, d_model))],
            out_specs=row_blk(d_model),
            out_shape=jax.ShapeDtypeStruct((bsz, seq, d_model), F32),
            compiler_params=params, name="out_proj",
        )(yat, ybt, h, w_out[layer].astype(BF16), final_norm_gain[None])
    return h
```

```python
import functools
import math

import numpy as np
import jax
import jax.numpy as jnp
from jax import lax
from jax.experimental import pallas as pl
from jax.experimental.pallas import tpu as pltpu

F32 = jnp.float32
BF16 = jnp.bfloat16
I32 = jnp.int32

HEAD_DIM = 64
N_HEADS = 8
WIDTH = N_HEADS * HEAD_DIM
N_PAIRS = N_HEADS // 2
ROT_DIM = HEAD_DIM // 4
ROPE_THETA = 500000.0
KV_LATENT = 128
IDX_HEADS = 4
IDX_DIM = 64
TOPK_MAX = 256
EPS = 1e-6
PATTERNS = ((128, 1), (512, 4), (2048, 16))

LANES = 128
SUBLANES = 8
MXU_DEPTH = 256
KBLK = MXU_DEPTH
TQ = LANES
TM = 512
NEG_BIG = -1e30
VMEM_LIMIT = 56 * 1024 * 1024
Q_SCALE = HEAD_DIM ** -0.5 * math.log2(math.e)

D_IN_PACKED = 28 * LANES


def _dot(a, b):
    return jnp.dot(a, b, preferred_element_type=F32)


def _dot_nt(a, b):
    return lax.dot_general(a, b, (((1,), (1,)), ((), ())), preferred_element_type=F32)


def _store_head_masked(dst_ref, row0, src_ref, n_pairs, tq):
    lane = lax.broadcasted_iota(I32, (tq, LANES), 1)
    for p in range(n_pairs):
        xp = src_ref[0, :, p * LANES:(p + 1) * LANES]
        zero = jnp.zeros_like(xp)
        dst_ref[row0 + 2 * p * tq:row0 + (2 * p + 1) * tq, :LANES] = jnp.where(lane < HEAD_DIM, xp, zero)
        dst_ref[row0 + (2 * p + 1) * tq:row0 + (2 * p + 2) * tq, :LANES] = jnp.where(lane >= HEAD_DIM, xp, zero)


def _proj_kernel(x_ref, gin_ref, w_ref, cos_ref, sina_ref, sinb_ref, gkv_ref, wuk_ref, gik_ref,
                 qa_ref, ka_ref, vat_ref, zat_ref, qb_ref, zbt_ref, iq_ref, kbd_ref, ckvt_ref,
                 ikd_ref, iwt_ref):
    tm = x_ref.shape[1]
    nblk = tm // KBLK
    x = x_ref[0]
    ms = jnp.mean(x * x, axis=-1, keepdims=True)
    u = (x * lax.rsqrt(ms + EPS) * gin_ref[...]).astype(BF16)
    cosf, sina, sinb = cos_ref[0], sina_ref[0], sinb_ref[0]

    def rope(t):
        return t * cosf + pltpu.roll(t, LANES - ROT_DIM // 2, 1) * sina + pltpu.roll(t, ROT_DIM // 2, 1) * sinb

    def proj(unit):
        return _dot(u, w_ref[:, unit * LANES:(unit + 2) * LANES])

    def halves(acc):
        return acc[:, :LANES], acc[:, LANES:]

    def store_t(ref, unit, t):
        tt = t.T.astype(BF16)
        for i in range(nblk):
            ref[0, i, unit * LANES:(unit + 1) * LANES, :] = tt[:, i * KBLK:(i + 1) * KBLK]

    for pair in range(2):
        for half, t in enumerate(halves(proj(2 * pair))):
            c = (2 * pair + half) * LANES
            qa_ref[0, :, c:c + LANES] = (rope(t) * Q_SCALE).astype(BF16)
        for half, t in enumerate(halves(proj(4 + 2 * pair))):
            c = (2 * pair + half) * LANES
            ka_ref[0, :, c:c + LANES] = rope(t).astype(BF16)
        for half, t in enumerate(halves(proj(8 + 2 * pair))):
            store_t(vat_ref, 2 * pair + half, t)
        for half, t in enumerate(halves(proj(12 + 2 * pair))):
            c = (2 * pair + half) * LANES
            zat_ref[0, c:c + LANES, :] = t.T.astype(BF16)
        for half, t in enumerate(halves(proj(16 + 2 * pair))):
            c = (2 * pair + half) * LANES
            qb_ref[0, :, c:c + LANES] = (rope(t) * Q_SCALE).astype(BF16)
        for half, t in enumerate(halves(proj(20 + 2 * pair))):
            c = (2 * pair + half) * LANES
            zbt_ref[0, c:c + LANES, :] = t.T.astype(BF16)
    for half, t in enumerate(halves(proj(24))):
        iq_ref[0, :, half * LANES:(half + 1) * LANES] = rope(t).astype(BF16)

    ckv, last = halves(proj(26))
    ckv = ckv * lax.rsqrt(jnp.mean(ckv * ckv, axis=-1, keepdims=True) + EPS) * gkv_ref[...]
    store_t(ckvt_ref, 0, ckv)
    kbd_ref[0] = rope(_dot(ckv.astype(BF16), wuk_ref[...])).astype(BF16)
    lane = lax.broadcasted_iota(I32, last.shape, 1)
    ik_ms = jnp.sum(jnp.where(lane < IDX_DIM, last * last, 0.0), axis=-1, keepdims=True) / IDX_DIM
    ik = rope(last * lax.rsqrt(ik_ms + EPS) * gik_ref[...])
    ikd_ref[0] = (ik + pltpu.roll(ik, IDX_DIM, 1)).astype(BF16)
    iwt_ref[0] = last.T[IDX_DIM:IDX_DIM + SUBLANES, :] * (IDX_HEADS ** -0.5 * IDX_DIM ** -0.5)


def _mix_a_bias_tables(tq, tk):
    o_min = -(tq - LANES)
    o_last = PATTERNS[1][0] + tk - LANES
    offsets = list(range(o_min, o_last + 1, LANES)) + [o_last + LANES]
    qi = np.arange(tq)[None, :]
    kj = np.arange(tk)[:, None]
    tables = []
    for o in offsets:
        d = o + qi - kj
        mult = np.zeros(d.shape, np.int64)
        for window, dil in PATTERNS:
            mult += ((d >= 0) & (d % dil == 0) & (d <= window)).astype(np.int64)
        tables.append(np.where(mult > 0, np.log2(np.maximum(mult, 1)), NEG_BIG))
    return np.stack(tables).astype(np.float32), o_min


def _mix_a_kernel(o_min, n_tables, qa_ref, ka_ref, vat_ref, zat_ref, gain_ref, bias_ref,
                  out_ref, qm_ref, acc_ref):
    tq = qa_ref.shape[1]
    j = pl.program_id(1)
    n_kb = (j * tq + tq - 1) // KBLK + 1
    _store_head_masked(qm_ref, 0, qa_ref, N_PAIRS, tq)
    acc_ref[...] = jnp.zeros_like(acc_ref)

    def body(kb, carry):
        ms, ls = carry
        k0 = pl.multiple_of(kb * KBLK, KBLK)
        bias = bias_ref[jnp.minimum((j * tq - kb * KBLK - o_min) // LANES, n_tables - 1)]
        bias2 = jnp.concatenate([bias, bias], axis=1)
        new_ms, new_ls = [], []
        for p in range(N_PAIRS):
            kblk = ka_ref[0, pl.ds(k0, KBLK), p * LANES:(p + 1) * LANES]
            s = _dot_nt(kblk, qm_ref[2 * p * tq:(2 * p + 2) * tq, :]) + bias2
            m_new = jnp.maximum(ms[p], jnp.max(s, axis=0, keepdims=True))
            alpha = jnp.exp2(ms[p] - m_new)
            pt = jnp.exp2(s - m_new)
            new_ms.append(m_new)
            new_ls.append(alpha * ls[p] + jnp.sum(pt, axis=0, keepdims=True))
            pt = pt.astype(BF16)
            vt = vat_ref[0, kb, p * LANES:(p + 1) * LANES, :]
            r0 = p * LANES
            acc_ref[r0:r0 + HEAD_DIM, :] = (acc_ref[r0:r0 + HEAD_DIM, :] * alpha[:, :tq]
                                            + _dot(vt[:HEAD_DIM], pt[:, :tq]))
            acc_ref[r0 + HEAD_DIM:r0 + LANES, :] = (acc_ref[r0 + HEAD_DIM:r0 + LANES, :] * alpha[:, tq:]
                                                    + _dot(vt[HEAD_DIM:], pt[:, tq:]))
        return tuple(new_ms), tuple(new_ls)

    init = (tuple(jnp.full((1, 2 * tq), NEG_BIG, F32) for _ in range(N_PAIRS)),
            tuple(jnp.zeros((1, 2 * tq), F32) for _ in range(N_PAIRS)))
    _, ls = lax.fori_loop(0, n_kb, body, init)
    for p in range(N_PAIRS):
        r0 = p * LANES
        acc_ref[r0:r0 + HEAD_DIM, :] = acc_ref[r0:r0 + HEAD_DIM, :] / ls[p][:, :tq]
        acc_ref[r0 + HEAD_DIM:r0 + LANES, :] = acc_ref[r0 + HEAD_DIM:r0 + LANES, :] / ls[p][:, tq:]

    z = zat_ref[0].astype(F32)
    g = acc_ref[...] * (z / (1.0 + jnp.exp(-z)))
    ms = jnp.mean(g * g, axis=0, keepdims=True)
    out_ref[0] = (g * lax.rsqrt(ms + EPS) * gain_ref[...]).astype(BF16)


def _mix_b_kernel(topk, qb_ref, iq_ref, iwt_ref, zbt_ref, ikd_ref, kbd_ref, ckvt_ref, wuvt_ref,
                  gain_ref, out_ref, sc_ref, iqm_ref, rhs_ref, acc_ref, g_ref):
    tq = qb_ref.shape[1]
    seq = ikd_ref.shape[1]
    j = pl.program_id(1)
    n_kb = (j * tq + tq - 1) // KBLK + 1
    row = lax.broadcasted_iota(I32, (KBLK, tq), 0)
    t_pos = j * tq + lax.broadcasted_iota(I32, (KBLK, tq), 1)

    _store_head_masked(iqm_ref, 0, iq_ref, IDX_HEADS // 2, tq)

    def score_body(kb, carry):
        k0 = pl.multiple_of(kb * KBLK, KBLK)
        s = jnp.maximum(_dot_nt(ikd_ref[0, pl.ds(k0, KBLK), :], iqm_ref[...]), 0.0)
        sc = s[:, :tq] * iwt_ref[0, 0:1, :]
        for h in range(1, IDX_HEADS):
            sc = sc + s[:, h * tq:(h + 1) * tq] * iwt_ref[0, h:h + 1, :]
        sc_ref[kb] = jnp.where(kb * KBLK + row <= t_pos, sc, -jnp.inf)
        return carry

    lax.fori_loop(0, n_kb, score_body, 0)

    def count(*preds):
        def body(kb, accs):
            s = sc_ref[kb]
            return tuple(acc + jnp.sum(pred(s, kb).astype(I32).reshape(KBLK // SUBLANES, SUBLANES, tq), axis=0)
                         for acc, pred in zip(accs, preds))
        accs = lax.fori_loop(0, n_kb, body, tuple(jnp.zeros((SUBLANES, tq), I32) for _ in preds))
        return tuple(jnp.sum(acc, axis=0, keepdims=True) for acc in accs)

    def key_to_float(key):
        bits = key ^ ((key >> 31) & jnp.int32(0x7FFFFFFF))
        return lax.bitcast_convert_type(bits, F32)

    def value_step(i, prefix):
        cand = prefix + (jnp.int32(1) << (31 - i))
        thr = key_to_float(cand)
        cnt, = count(lambda s, kb: s >= thr)
        return jnp.where(cnt >= topk, cand, prefix)

    prefix = lax.fori_loop(0, 32, value_step, jnp.full((1, tq), -2 ** 31, I32))
    thr = key_to_float(prefix)
    n_gt, n_eq = count(lambda s, kb: s > thr, lambda s, kb: s == thr)
    need = topk - n_gt

    n_idx_bits = max(1, (seq - 1).bit_length())

    def tie_walk():
        def tie_step(i, cut):
            cand = cut + (jnp.int32(1) << (n_idx_bits - 1 - i))
            cnt, = count(lambda s, kb: (s == thr) & (kb * KBLK + row < cand))
            return jnp.where(cnt < need, cand, cut)
        return lax.fori_loop(0, n_idx_bits, tie_step, jnp.zeros((1, tq), I32))

    cut = lax.cond(jnp.max(n_eq - need) > 0, tie_walk, lambda: jnp.full((1, tq), seq, I32))

    _store_head_masked(rhs_ref, 0, qb_ref, N_PAIRS, tq)
    eye = (lax.broadcasted_iota(I32, (tq, tq), 0) == lax.broadcasted_iota(I32, (tq, tq), 1)).astype(BF16)
    for h in range(N_HEADS):
        rhs_ref[h * tq:(h + 1) * tq, LANES:] = eye
    acc_ref[...] = jnp.zeros_like(acc_ref)

    def attn_body(kb, carry):
        m, l = carry
        k0 = pl.multiple_of(kb * KBLK, KBLK)
        sc = sc_ref[kb]
        kidx = kb * KBLK + row
        sel = (sc > thr) | ((sc == thr) & (kidx <= cut)) | (t_pos < topk)
        bias = jnp.where(sel & (kidx <= t_pos), 0.0, NEG_BIG).astype(BF16)
        lhs = jnp.concatenate([kbd_ref[0, pl.ds(k0, KBLK), :], bias], axis=1)
        s = _dot_nt(lhs, rhs_ref[...])
        m_new = jnp.maximum(m, jnp.max(s, axis=0, keepdims=True))
        alpha = jnp.exp2(m - m_new)
        pt = jnp.exp2(s - m_new)
        l = alpha * l + jnp.sum(pt, axis=0, keepdims=True)
        acc_ref[...] = acc_ref[...] * alpha + _dot(ckvt_ref[0, kb], pt.astype(BF16))
        return m_new, l

    init = (jnp.full((1, N_HEADS * tq), NEG_BIG, F32), jnp.zeros((1, N_HEADS * tq), F32))
    _, l = lax.fori_loop(0, n_kb, attn_body, init)
    o_lat = (acc_ref[...] / l).astype(BF16)
    for h in range(N_HEADS):
        g_ref[h * HEAD_DIM:(h + 1) * HEAD_DIM, :] = _dot(wuvt_ref[h], o_lat[:, h * tq:(h + 1) * tq])

    z = zbt_ref[0].astype(F32)
    g = g_ref[...] * (z / (1.0 + jnp.exp(-z)))
    ms = jnp.mean(g * g, axis=0, keepdims=True)
    out_ref[0] = (g * lax.rsqrt(ms + EPS) * gain_ref[...]).astype(BF16)


def _out_kernel(yat_ref, ybt_ref, x_ref, w_ref, gain_ref, out_ref):
    y = jnp.concatenate([yat_ref[0].astype(F32).T, ybt_ref[0].astype(F32).T], axis=1).astype(BF16)
    h = x_ref[0] + _dot(y, w_ref[...])
    ms = jnp.mean(h * h, axis=-1, keepdims=True)
    out_ref[0] = h * lax.rsqrt(ms + EPS) * gain_ref[...]


def _rope_tables(positions):
    half = ROT_DIM // 2
    inv_freq = ROPE_THETA ** (-jnp.arange(0, ROT_DIM, 2, dtype=F32) / ROT_DIM)
    ang = positions.astype(F32)[..., None] * inv_freq
    cos, sin = jnp.cos(ang), jnp.sin(ang)
    c = np.arange(LANES) % HEAD_DIM
    sel = c % half
    cosf = jnp.where(c < ROT_DIM, cos[..., sel], 1.0)
    sina = jnp.where(c < half, -sin[..., sel], 0.0)
    sinb = jnp.where((c >= half) & (c < ROT_DIM), sin[..., sel], 0.0)
    return cosf, sina, sinb


def _pack_w_in(w):
    offs = np.cumsum([0, WIDTH, WIDTH, WIDTH, WIDTH, WIDTH, KV_LATENT, WIDTH,
                      IDX_HEADS * IDX_DIM, IDX_DIM, IDX_HEADS])
    qa, ka, va, za, qb, ckv, zb, iq, ik, iw = [w[:, offs[i]:offs[i + 1]] for i in range(10)]
    pad = jnp.zeros((w.shape[0], LANES - IDX_DIM - IDX_HEADS), w.dtype)
    return jnp.concatenate([qa, ka, va, za, qb, zb, iq, ckv, ik, iw, pad], axis=1)


def _full(shape):
    return pl.BlockSpec(shape, lambda *_: (0,) * len(shape))


@jax.jit
def kernel(x, positions, norm_in_gain, w_in, kv_norm_gain, w_uk, w_uv, idx_k_norm_gain,
           branch_norm_gain_a, branch_norm_gain_b, w_out, final_norm_gain):
    bsz, seq, d_model = x.shape
    assert w_in.shape[0] == 1
    assert seq % KBLK == 0 and seq <= PATTERNS[-1][0]
    n_blk = seq // KBLK
    topk = min(TOPK_MAX, seq // 4)
    tm = min(TM, seq)
    tq = TQ
    params = pltpu.CompilerParams(dimension_semantics=("arbitrary", "arbitrary"),
                                  vmem_limit_bytes=VMEM_LIMIT)

    cosf, sina, sinb = _rope_tables(positions)
    w_packed = _pack_w_in(w_in[0]).astype(BF16)
    wuk_dup = jnp.concatenate([w_uk[0], w_uk[0]], axis=1).astype(BF16)
    gik = jnp.concatenate([idx_k_norm_gain[0], jnp.zeros((LANES - IDX_DIM,), F32)])[None]
    wuv_t = jnp.swapaxes(w_uv[0], 1, 2).astype(BF16)
    gain_a = jnp.broadcast_to(branch_norm_gain_a[0][:, None], (WIDTH, tq))
    gain_b = jnp.broadcast_to(branch_norm_gain_b[0][:, None], (WIDTH, tq))

    row_blk = lambda c: pl.BlockSpec((1, tm, c), lambda b, i: (b, i, 0))
    col_blk = lambda c: pl.BlockSpec((1, c, tm), lambda b, i: (b, 0, i))
    key_blk = lambda c: pl.BlockSpec((1, tm // KBLK, c, KBLK), lambda b, i: (b, i, 0, 0))
    qa, ka, vat, zat, qb, zbt, iq, kbd, ckvt, ikd, iwt = pl.pallas_call(
        _proj_kernel,
        grid=(bsz, seq // tm),
        in_specs=[row_blk(d_model), _full((1, d_model)), _full((d_model, D_IN_PACKED)),
                  row_blk(LANES), row_blk(LANES), row_blk(LANES),
                  _full((1, KV_LATENT)), _full((KV_LATENT, LANES)), _full((1, LANES))],
        out_specs=[row_blk(WIDTH), row_blk(WIDTH), key_blk(WIDTH), col_blk(WIDTH),
                   row_blk(WIDTH), col_blk(WIDTH), row_blk(2 * LANES), row_blk(LANES),
                   key_blk(KV_LATENT), row_blk(LANES), col_blk(SUBLANES)],
        out_shape=[jax.ShapeDtypeStruct((bsz, seq, WIDTH), BF16),
                   jax.ShapeDtypeStruct((bsz, seq, WIDTH), BF16),
                   jax.ShapeDtypeStruct((bsz, n_blk, WIDTH, KBLK), BF16),
                   jax.ShapeDtypeStruct((bsz, WIDTH, seq), BF16),
                   jax.ShapeDtypeStruct((bsz, seq, WIDTH), BF16),
                   jax.ShapeDtypeStruct((bsz, WIDTH, seq), BF16),
                   jax.ShapeDtypeStruct((bsz, seq, 2 * LANES), BF16),
                   jax.ShapeDtypeStruct((bsz, seq, LANES), BF16),
                   jax.ShapeDtypeStruct((bsz, n_blk, KV_LATENT, KBLK), BF16),
                   jax.ShapeDtypeStruct((bsz, seq, LANES), BF16),
                   jax.ShapeDtypeStruct((bsz, SUBLANES, seq), F32)],
        compiler_params=params, name="proj",
    )(x, norm_in_gain[0][None], w_packed, cosf, sina, sinb, kv_norm_gain[0][None], wuk_dup, gik)

    q_row = lambda c: pl.BlockSpec((1, tq, c), lambda b, i: (b, i, 0))
    q_col = lambda c: pl.BlockSpec((1, c, tq), lambda b, i: (b, 0, i))
    per_b3 = lambda s1, s2: pl.BlockSpec((1, s1, s2), lambda b, i: (b, 0, 0))
    per_b4 = lambda s1, s2, s3: pl.BlockSpec((1, s1, s2, s3), lambda b, i: (b, 0, 0, 0))

    bias_np, o_min = _mix_a_bias_tables(tq, KBLK)
    yat = pl.pallas_call(
        functools.partial(_mix_a_kernel, o_min, bias_np.shape[0]),
        grid=(bsz, seq // tq),
        in_specs=[q_row(WIDTH), per_b3(seq, WIDTH), per_b4(n_blk, WIDTH, KBLK), q_col(WIDTH),
                  _full((WIDTH, tq)), _full(bias_np.shape)],
        out_specs=q_col(WIDTH),
        out_shape=jax.ShapeDtypeStruct((bsz, WIDTH, seq), BF16),
        scratch_shapes=[pltpu.VMEM((N_HEADS * tq, LANES), BF16),
                        pltpu.VMEM((WIDTH, tq), F32)],
        compiler_params=params, name="mix_a",
    )(qa, ka, vat, zat, gain_a, jnp.asarray(bias_np))

    ybt = pl.pallas_call(
        functools.partial(_mix_b_kernel, topk),
        grid=(bsz, seq // tq),
        in_specs=[q_row(WIDTH), q_row(2 * LANES), q_col(SUBLANES), q_col(WIDTH),
                  per_b3(seq, LANES), per_b3(seq, LANES), per_b4(n_blk, KV_LATENT, KBLK),
                  _full((N_HEADS, HEAD_DIM, KV_LATENT)), _full((WIDTH, tq))],
        out_specs=q_col(WIDTH),
        out_shape=jax.ShapeDtypeStruct((bsz, WIDTH, seq), BF16),
        scratch_shapes=[pltpu.VMEM((n_blk, KBLK, tq), F32),
                        pltpu.VMEM((IDX_HEADS * tq, LANES), BF16),
                        pltpu.VMEM((N_HEADS * tq, LANES + tq), BF16),
                        pltpu.VMEM((KV_LATENT, N_HEADS * tq), F32),
                        pltpu.VMEM((WIDTH, tq), F32)],
        compiler_params=params, name="mix_b",
    )(qb, iq, iwt, zbt, ikd, kbd, ckvt, wuv_t, gain_b)

    return pl.pallas_call(
        _out_kernel,
        grid=(bsz, seq // tm),
        in_specs=[col_blk(WIDTH), col_blk(WIDTH), row_blk(d_model),
                  _full((2 * WIDTH, d_model)), _full((1, d_model))],
        out_specs=row_blk(d_model),
        out_shape=jax.ShapeDtypeStruct((bsz, seq, d_model), F32),
        compiler_params=params, name="out_proj",
    )(yat, ybt, x, w_out[0].astype(BF16), final_norm_gain[None])
```

```python
import functools
import math

import numpy as np
import jax
import jax.numpy as jnp
from jax import lax
from jax.experimental import pallas as pl
from jax.experimental.pallas import tpu as pltpu

F32 = jnp.float32
BF16 = jnp.bfloat16
I32 = jnp.int32

HEAD_DIM = 64
N_HEADS = 8
WIDTH = N_HEADS * HEAD_DIM
N_PAIRS = N_HEADS // 2
ROT_DIM = HEAD_DIM // 4
ROPE_THETA = 500000.0
KV_LATENT = 128
IDX_HEADS = 4
IDX_DIM = 64
TOPK_MAX = 256
EPS = 1e-6
PATTERNS = ((128, 1), (512, 4), (2048, 16))

LANES = 128
SUBLANES = 8
MXU_DEPTH = 256
KBLK = MXU_DEPTH
TQ = LANES
TM = 512
NEG_BIG = -1e30
VMEM_LIMIT = 56 * 1024 * 1024
Q_SCALE = HEAD_DIM ** -0.5 * math.log2(math.e)

D_IN_PACKED = 28 * LANES


def _dot(a, b):
    return jnp.dot(a, b, preferred_element_type=F32)


def _dot_nt(a, b):
    return lax.dot_general(a, b, (((1,), (1,)), ((), ())), preferred_element_type=F32)


def _store_head_masked(dst_ref, row0, src_ref, n_pairs, tq):
    lane = lax.broadcasted_iota(I32, (tq, LANES), 1)
    for p in range(n_pairs):
        xp = src_ref[0, :, p * LANES:(p + 1) * LANES]
        zero = jnp.zeros_like(xp)
        dst_ref[row0 + 2 * p * tq:row0 + (2 * p + 1) * tq, :LANES] = jnp.where(lane < HEAD_DIM, xp, zero)
        dst_ref[row0 + (2 * p + 1) * tq:row0 + (2 * p + 2) * tq, :LANES] = jnp.where(lane >= HEAD_DIM, xp, zero)


def _proj_kernel(x_ref, gin_ref, w_ref, cos_ref, sina_ref, sinb_ref, gkv_ref, wuk_ref, gik_ref,
                 qa_ref, ka_ref, vat_ref, zat_ref, qb_ref, zbt_ref, iq_ref, kbd_ref, ckvt_ref,
                 ikd_ref, iwt_ref):
    tm = x_ref.shape[1]
    nblk = tm // KBLK
    x = x_ref[0]
    ms = jnp.mean(x * x, axis=-1, keepdims=True)
    u = (x * lax.rsqrt(ms + EPS) * gin_ref[...]).astype(BF16)
    cosf, sina, sinb = cos_ref[0], sina_ref[0], sinb_ref[0]

    def rope(t):
        return t * cosf + pltpu.roll(t, LANES - ROT_DIM // 2, 1) * sina + pltpu.roll(t, ROT_DIM // 2, 1) * sinb

    def proj(unit):
        return _dot(u, w_ref[:, unit * LANES:(unit + 2) * LANES])

    def halves(acc):
        return acc[:, :LANES], acc[:, LANES:]

    def store_t(ref, unit, t):
        tt = t.T.astype(BF16)
        for i in range(nblk):
            ref[0, i, unit * LANES:(unit + 1) * LANES, :] = tt[:, i * KBLK:(i + 1) * KBLK]

    for pair in range(2):
        for half, t in enumerate(halves(proj(2 * pair))):
            c = (2 * pair + half) * LANES
            qa_ref[0, :, c:c + LANES] = (rope(t) * Q_SCALE).astype(BF16)
        for half, t in enumerate(halves(proj(4 + 2 * pair))):
            c = (2 * pair + half) * LANES
            ka_ref[0, :, c:c + LANES] = rope(t).astype(BF16)
        for half, t in enumerate(halves(proj(8 + 2 * pair))):
            store_t(vat_ref, 2 * pair + half, t)
        for half, t in enumerate(halves(proj(12 + 2 * pair))):
            c = (2 * pair + half) * LANES
            zat_ref[0, c:c + LANES, :] = t.T.astype(BF16)
        for half, t in enumerate(halves(proj(16 + 2 * pair))):
            c = (2 * pair + half) * LANES
            qb_ref[0, :, c:c + LANES] = (rope(t) * Q_SCALE).astype(BF16)
        for half, t in enumerate(halves(proj(20 + 2 * pair))):
            c = (2 * pair + half) * LANES
            zbt_ref[0, c:c + LANES, :] = t.T.astype(BF16)
    for half, t in enumerate(halves(proj(24))):
        iq_ref[0, :, half * LANES:(half + 1) * LANES] = rope(t).astype(BF16)

    ckv, last = halves(proj(26))
    ckv = ckv * lax.rsqrt(jnp.mean(ckv * ckv, axis=-1, keepdims=True) + EPS) * gkv_ref[...]
    store_t(ckvt_ref, 0, ckv)
    kbd_ref[0] = rope(_dot(ckv.astype(BF16), wuk_ref[...])).astype(BF16)
    lane = lax.broadcasted_iota(I32, last.shape, 1)
    ik_ms = jnp.sum(jnp.where(lane < IDX_DIM, last * last, 0.0), axis=-1, keepdims=True) / IDX_DIM
    ik = rope(last * lax.rsqrt(ik_ms + EPS) * gik_ref[...])
    ikd_ref[0] = (ik + pltpu.roll(ik, IDX_DIM, 1)).astype(BF16)
    iwt_ref[0] = last.T[IDX_DIM:IDX_DIM + SUBLANES, :] * (IDX_HEADS ** -0.5 * IDX_DIM ** -0.5)


def _mix_a_bias_tables(tq, tk):
    o_min = -(tq - LANES)
    o_last = PATTERNS[1][0] + tk - LANES
    offsets = list(range(o_min, o_last + 1, LANES)) + [o_last + LANES]
    qi = np.arange(tq)[None, :]
    kj = np.arange(tk)[:, None]
    tables = []
    for o in offsets:
        d = o + qi - kj
        mult = np.zeros(d.shape, np.int64)
        for window, dil in PATTERNS:
            mult += ((d >= 0) & (d % dil == 0) & (d <= window)).astype(np.int64)
        tables.append(np.where(mult > 0, np.log2(np.maximum(mult, 1)), NEG_BIG))
    return np.stack(tables).astype(np.float32), o_min


def _mix_a_kernel(o_min, n_tables, qa_ref, ka_ref, vat_ref, zat_ref, gain_ref, bias_ref,
                  out_ref, qm_ref, s_ref, pt_ref, acc_ref):
    tq = qa_ref.shape[1]
    j = pl.program_id(1)
    n_kb = (j * tq + tq - 1) // KBLK + 1
    _store_head_masked(qm_ref, 0, qa_ref, N_PAIRS, tq)
    acc_ref[...] = jnp.zeros_like(acc_ref)
    pt_ref[...] = jnp.zeros_like(pt_ref)

    def scores(kb, p):
        k0 = pl.multiple_of(kb * KBLK, KBLK)
        return _dot_nt(ka_ref[0, pl.ds(k0, KBLK), p * LANES:(p + 1) * LANES],
                       qm_ref[2 * p * tq:(2 * p + 2) * tq, :])

    def accumulate(kb, p, alpha):
        vt = vat_ref[0, kb, p * LANES:(p + 1) * LANES, :]
        r0 = p * LANES
        acc_ref[r0:r0 + HEAD_DIM, :] = (acc_ref[r0:r0 + HEAD_DIM, :] * alpha[:, :tq]
                                        + _dot(vt[:HEAD_DIM], pt_ref[p, :, :tq]))
        acc_ref[r0 + HEAD_DIM:r0 + LANES, :] = (acc_ref[r0 + HEAD_DIM:r0 + LANES, :] * alpha[:, tq:]
                                                + _dot(vt[HEAD_DIM:], pt_ref[p, :, tq:]))

    for p in range(N_PAIRS):
        s_ref[p] = scores(0, p)

    def body(kb, carry):
        ms, ls, alphas = carry
        prev = jnp.maximum(kb - 1, 0)
        nxt = jnp.minimum(kb + 1, n_kb - 1)
        bias = bias_ref[jnp.minimum((j * tq - kb * KBLK - o_min) // LANES, n_tables - 1)]
        bias2 = jnp.concatenate([bias, bias], axis=1)
        new_ms, new_ls, new_alphas = [], [], []
        for p in range(N_PAIRS):
            accumulate(prev, p, alphas[p])
            s = s_ref[p] + bias2
            m_new = jnp.maximum(ms[p], jnp.max(s, axis=0, keepdims=True))
            alpha = jnp.exp2(ms[p] - m_new)
            pt = jnp.exp2(s - m_new)
            new_ms.append(m_new)
            new_ls.append(alpha * ls[p] + jnp.sum(pt, axis=0, keepdims=True))
            new_alphas.append(alpha)
            pt_ref[p] = pt.astype(BF16)
            s_ref[p] = scores(nxt, p)
        return tuple(new_ms), tuple(new_ls), tuple(new_alphas)

    init = (tuple(jnp.full((1, 2 * tq), NEG_BIG, F32) for _ in range(N_PAIRS)),
            tuple(jnp.zeros((1, 2 * tq), F32) for _ in range(N_PAIRS)),
            tuple(jnp.ones((1, 2 * tq), F32) for _ in range(N_PAIRS)))
    _, ls, alphas = lax.fori_loop(0, n_kb, body, init)
    for p in range(N_PAIRS):
        accumulate(n_kb - 1, p, alphas[p])
        r0 = p * LANES
        acc_ref[r0:r0 + HEAD_DIM, :] = acc_ref[r0:r0 + HEAD_DIM, :] / ls[p][:, :tq]
        acc_ref[r0 + HEAD_DIM:r0 + LANES, :] = acc_ref[r0 + HEAD_DIM:r0 + LANES, :] / ls[p][:, tq:]

    z = zat_ref[0].astype(F32)
    g = acc_ref[...] * (z / (1.0 + jnp.exp(-z)))
    ms = jnp.mean(g * g, axis=0, keepdims=True)
    out_ref[0] = (g * lax.rsqrt(ms + EPS) * gain_ref[...]).astype(BF16)


def _mix_b_kernel(topk, qb_ref, iq_ref, iwt_ref, zbt_ref, ikd_ref, kbd_ref, ckvt_ref, wuvt_ref,
                  gain_ref, tri_ref, out_ref, sc_ref, bias_ref, iqm_ref, rhs_ref, raw_ref, rank_ref,
                  s_ref, pt_ref, acc_ref, g_ref):
    tq = qb_ref.shape[1]
    j = pl.program_id(1)
    n_kb = (j * tq + tq - 1) // KBLK + 1
    row = lax.broadcasted_iota(I32, (KBLK, tq), 0)
    t_pos = j * tq + lax.broadcasted_iota(I32, (KBLK, tq), 1)

    def key_rows(ref, kb):
        return ref[0, pl.ds(pl.multiple_of(kb * KBLK, KBLK), KBLK), :]

    def next_block(kb):
        return jnp.minimum(kb + 1, n_kb - 1)


    _store_head_masked(iqm_ref, 0, iq_ref, IDX_HEADS // 2, tq)

    def raw_scores(kb):
        return _dot_nt(key_rows(ikd_ref, kb), iqm_ref[...])

    raw_ref[...] = raw_scores(0)

    def score_body(kb, carry):
        sc = jnp.maximum(raw_ref[:, :tq], 0.0) * iwt_ref[0, 0:1, :]
        for h in range(1, IDX_HEADS):
            sc = sc + jnp.maximum(raw_ref[:, h * tq:(h + 1) * tq], 0.0) * iwt_ref[0, h:h + 1, :]
        sc_ref[kb] = jnp.where(kb * KBLK + row <= t_pos, sc, -jnp.inf)
        raw_ref[...] = raw_scores(next_block(kb))
        return carry

    lax.fori_loop(0, n_kb, score_body, 0)

    def count(pred):
        def body(kb, acc):
            hit = pred(sc_ref[kb]).astype(I32)
            return acc + jnp.sum(hit.reshape(KBLK // SUBLANES, SUBLANES, tq), axis=0)
        acc = lax.fori_loop(0, n_kb, body, jnp.zeros((SUBLANES, tq), I32))
        return jnp.sum(acc, axis=0, keepdims=True)

    def key_to_float(key):
        bits = key ^ ((key >> 31) & jnp.int32(0x7FFFFFFF))
        return lax.bitcast_convert_type(bits, F32)

    def value_step(i, prefix):
        cand = prefix + (jnp.int32(1) << (31 - i))
        thr = key_to_float(cand)
        return jnp.where(count(lambda s: s >= thr) >= topk, cand, prefix)

    prefix = lax.fori_loop(0, 32, value_step, jnp.full((1, tq), -2 ** 31, I32))
    thr = key_to_float(prefix)
    need = (topk - count(lambda s: s > thr)).astype(F32)

    def tie_ranks(kb):
        tie = jnp.where(sc_ref[kb] == thr, 1.0, 0.0).astype(BF16)
        return _dot(tri_ref[...], tie)

    rank_ref[...] = tie_ranks(0)

    def bias_body(kb, ties_before):
        sc = sc_ref[kb]
        rank = rank_ref[...]
        take = (sc == thr) & (rank + ties_before <= need)
        sel = (sc > thr) | take | (t_pos < topk)
        bias_ref[kb] = jnp.where(sel & (kb * KBLK + row <= t_pos), 0.0, NEG_BIG).astype(BF16)
        rank_ref[...] = tie_ranks(next_block(kb))
        return ties_before + rank[KBLK - 1:KBLK, :]

    lax.fori_loop(0, n_kb, bias_body, jnp.zeros((1, tq), F32))

    _store_head_masked(rhs_ref, 0, qb_ref, N_PAIRS, tq)
    eye = (lax.broadcasted_iota(I32, (tq, tq), 0) == lax.broadcasted_iota(I32, (tq, tq), 1)).astype(BF16)
    for h in range(N_HEADS):
        rhs_ref[h * tq:(h + 1) * tq, LANES:] = eye
    acc_ref[...] = jnp.zeros_like(acc_ref)
    pt_ref[...] = jnp.zeros_like(pt_ref)

    def masked_scores(kb):
        lhs = jnp.concatenate([key_rows(kbd_ref, kb), bias_ref[kb]], axis=1)
        return _dot_nt(lhs, rhs_ref[...])

    def accumulate(kb, alpha):
        acc_ref[...] = acc_ref[...] * alpha + _dot(ckvt_ref[0, kb], pt_ref[...])

    s_ref[...] = masked_scores(0)

    def attn_body(kb, carry):
        m, l, alpha = carry
        accumulate(jnp.maximum(kb - 1, 0), alpha)
        s = s_ref[...]
        m_new = jnp.maximum(m, jnp.max(s, axis=0, keepdims=True))
        alpha = jnp.exp2(m - m_new)
        pt = jnp.exp2(s - m_new)
        l = alpha * l + jnp.sum(pt, axis=0, keepdims=True)
        pt_ref[...] = pt.astype(BF16)
        s_ref[...] = masked_scores(next_block(kb))
        return m_new, l, alpha

    init = (jnp.full((1, N_HEADS * tq), NEG_BIG, F32), jnp.zeros((1, N_HEADS * tq), F32),
            jnp.ones((1, N_HEADS * tq), F32))
    _, l, alpha = lax.fori_loop(0, n_kb, attn_body, init)
    accumulate(n_kb - 1, alpha)
    o_lat = (acc_ref[...] / l).astype(BF16)
    for h in range(N_HEADS):
        g_ref[h * HEAD_DIM:(h + 1) * HEAD_DIM, :] = _dot(wuvt_ref[h], o_lat[:, h * tq:(h + 1) * tq])

    z = zbt_ref[0].astype(F32)
    g = g_ref[...] * (z / (1.0 + jnp.exp(-z)))
    ms = jnp.mean(g * g, axis=0, keepdims=True)
    out_ref[0] = (g * lax.rsqrt(ms + EPS) * gain_ref[...]).astype(BF16)


def _out_kernel(yat_ref, ybt_ref, x_ref, w_ref, gain_ref, out_ref):
    y = jnp.concatenate([yat_ref[0].astype(F32).T, ybt_ref[0].astype(F32).T], axis=1).astype(BF16)
    h = x_ref[0] + _dot(y, w_ref[...])
    ms = jnp.mean(h * h, axis=-1, keepdims=True)
    out_ref[0] = h * lax.rsqrt(ms + EPS) * gain_ref[...]


def _rope_tables(positions):
    half = ROT_DIM // 2
    inv_freq = ROPE_THETA ** (-jnp.arange(0, ROT_DIM, 2, dtype=F32) / ROT_DIM)
    ang = positions.astype(F32)[..., None] * inv_freq
    cos, sin = jnp.cos(ang), jnp.sin(ang)
    c = np.arange(LANES) % HEAD_DIM
    sel = c % half
    cosf = jnp.where(c < ROT_DIM, cos[..., sel], 1.0)
    sina = jnp.where(c < half, -sin[..., sel], 0.0)
    sinb = jnp.where((c >= half) & (c < ROT_DIM), sin[..., sel], 0.0)
    return cosf, sina, sinb


def _pack_w_in(w):
    offs = np.cumsum([0, WIDTH, WIDTH, WIDTH, WIDTH, WIDTH, KV_LATENT, WIDTH,
                      IDX_HEADS * IDX_DIM, IDX_DIM, IDX_HEADS])
    qa, ka, va, za, qb, ckv, zb, iq, ik, iw = [w[:, offs[i]:offs[i + 1]] for i in range(10)]
    pad = jnp.zeros((w.shape[0], LANES - IDX_DIM - IDX_HEADS), w.dtype)
    return jnp.concatenate([qa, ka, va, za, qb, zb, iq, ckv, ik, iw, pad], axis=1)


def _full(shape):
    return pl.BlockSpec(shape, lambda *_: (0,) * len(shape))


@jax.jit
def kernel(x, positions, norm_in_gain, w_in, kv_norm_gain, w_uk, w_uv, idx_k_norm_gain,
           branch_norm_gain_a, branch_norm_gain_b, w_out, final_norm_gain):
    bsz, seq, d_model = x.shape
    assert w_in.shape[0] == 1
    assert seq % KBLK == 0 and seq <= PATTERNS[-1][0]
    n_blk = seq // KBLK
    topk = min(TOPK_MAX, seq // 4)
    tm = min(TM, seq)
    tq = TQ
    params = pltpu.CompilerParams(dimension_semantics=("arbitrary", "arbitrary"),
                                  vmem_limit_bytes=VMEM_LIMIT)

    cosf, sina, sinb = _rope_tables(positions)
    w_packed = _pack_w_in(w_in[0]).astype(BF16)
    wuk_dup = jnp.concatenate([w_uk[0], w_uk[0]], axis=1).astype(BF16)
    gik = jnp.concatenate([idx_k_norm_gain[0], jnp.zeros((LANES - IDX_DIM,), F32)])[None]
    wuv_t = jnp.swapaxes(w_uv[0], 1, 2).astype(BF16)
    gain_a = jnp.broadcast_to(branch_norm_gain_a[0][:, None], (WIDTH, tq))
    gain_b = jnp.broadcast_to(branch_norm_gain_b[0][:, None], (WIDTH, tq))

    row_blk = lambda c: pl.BlockSpec((1, tm, c), lambda b, i: (b, i, 0))
    col_blk = lambda c: pl.BlockSpec((1, c, tm), lambda b, i: (b, 0, i))
    key_blk = lambda c: pl.BlockSpec((1, tm // KBLK, c, KBLK), lambda b, i: (b, i, 0, 0))
    qa, ka, vat, zat, qb, zbt, iq, kbd, ckvt, ikd, iwt = pl.pallas_call(
        _proj_kernel,
        grid=(bsz, seq // tm),
        in_specs=[row_blk(d_model), _full((1, d_model)), _full((d_model, D_IN_PACKED)),
                  row_blk(LANES), row_blk(LANES), row_blk(LANES),
                  _full((1, KV_LATENT)), _full((KV_LATENT, LANES)), _full((1, LANES))],
        out_specs=[row_blk(WIDTH), row_blk(WIDTH), key_blk(WIDTH), col_blk(WIDTH),
                   row_blk(WIDTH), col_blk(WIDTH), row_blk(2 * LANES), row_blk(LANES),
                   key_blk(KV_LATENT), row_blk(LANES), col_blk(SUBLANES)],
        out_shape=[jax.ShapeDtypeStruct((bsz, seq, WIDTH), BF16),
                   jax.ShapeDtypeStruct((bsz, seq, WIDTH), BF16),
                   jax.ShapeDtypeStruct((bsz, n_blk, WIDTH, KBLK), BF16),
                   jax.ShapeDtypeStruct((bsz, WIDTH, seq), BF16),
                   jax.ShapeDtypeStruct((bsz, seq, WIDTH), BF16),
                   jax.ShapeDtypeStruct((bsz, WIDTH, seq), BF16),
                   jax.ShapeDtypeStruct((bsz, seq, 2 * LANES), BF16),
                   jax.ShapeDtypeStruct((bsz, seq, LANES), BF16),
                   jax.ShapeDtypeStruct((bsz, n_blk, KV_LATENT, KBLK), BF16),
                   jax.ShapeDtypeStruct((bsz, seq, LANES), BF16),
                   jax.ShapeDtypeStruct((bsz, SUBLANES, seq), F32)],
        compiler_params=params, name="proj",
    )(x, norm_in_gain[0][None], w_packed, cosf, sina, sinb, kv_norm_gain[0][None], wuk_dup, gik)

    q_row = lambda c: pl.BlockSpec((1, tq, c), lambda b, i: (b, i, 0))
    q_col = lambda c: pl.BlockSpec((1, c, tq), lambda b, i: (b, 0, i))
    per_b3 = lambda s1, s2: pl.BlockSpec((1, s1, s2), lambda b, i: (b, 0, 0))
    per_b4 = lambda s1, s2, s3: pl.BlockSpec((1, s1, s2, s3), lambda b, i: (b, 0, 0, 0))

    bias_np, o_min = _mix_a_bias_tables(tq, KBLK)
    yat = pl.pallas_call(
        functools.partial(_mix_a_kernel, o_min, bias_np.shape[0]),
        grid=(bsz, seq // tq),
        in_specs=[q_row(WIDTH), per_b3(seq, WIDTH), per_b4(n_blk, WIDTH, KBLK), q_col(WIDTH),
                  _full((WIDTH, tq)), _full(bias_np.shape)],
        out_specs=q_col(WIDTH),
        out_shape=jax.ShapeDtypeStruct((bsz, WIDTH, seq), BF16),
        scratch_shapes=[pltpu.VMEM((N_HEADS * tq, LANES), BF16),
                        pltpu.VMEM((N_PAIRS, KBLK, 2 * tq), F32),
                        pltpu.VMEM((N_PAIRS, KBLK, 2 * tq), BF16),
                        pltpu.VMEM((WIDTH, tq), F32)],
        compiler_params=params, name="mix_a",
    )(qa, ka, vat, zat, gain_a, jnp.asarray(bias_np))

    ybt = pl.pallas_call(
        functools.partial(_mix_b_kernel, topk),
        grid=(bsz, seq // tq),
        in_specs=[q_row(WIDTH), q_row(2 * LANES), q_col(SUBLANES), q_col(WIDTH),
                  per_b3(seq, LANES), per_b3(seq, LANES), per_b4(n_blk, KV_LATENT, KBLK),
                  _full((N_HEADS, HEAD_DIM, KV_LATENT)), _full((WIDTH, tq)), _full((KBLK, KBLK))],
        out_specs=q_col(WIDTH),
        out_shape=jax.ShapeDtypeStruct((bsz, WIDTH, seq), BF16),
        scratch_shapes=[pltpu.VMEM((n_blk, KBLK, tq), F32),
                        pltpu.VMEM((n_blk, KBLK, tq), BF16),
                        pltpu.VMEM((IDX_HEADS * tq, LANES), BF16),
                        pltpu.VMEM((N_HEADS * tq, LANES + tq), BF16),
                        pltpu.VMEM((KBLK, IDX_HEADS * tq), F32),
                        pltpu.VMEM((KBLK, tq), F32),
                        pltpu.VMEM((KBLK, N_HEADS * tq), F32),
                        pltpu.VMEM((KBLK, N_HEADS * tq), BF16),
                        pltpu.VMEM((KV_LATENT, N_HEADS * tq), F32),
                        pltpu.VMEM((WIDTH, tq), F32)],
        compiler_params=params, name="mix_b",
    )(qb, iq, iwt, zbt, ikd, kbd, ckvt, wuv_t, gain_b,
      jnp.asarray(np.tril(np.ones((KBLK, KBLK), np.float32)), BF16))

    return pl.pallas_call(
        _out_kernel,
        grid=(bsz, seq // tm),
        in_specs=[col_blk(WIDTH), col_blk(WIDTH), row_blk(d_model),
                  _full((2 * WIDTH, d_model)), _full((1, d_model))],
        out_specs=row_blk(d_model),
        out_shape=jax.ShapeDtypeStruct((bsz, seq, d_model), F32),
        compiler_params=params, name="out_proj",
    )(yat, ybt, x, w_out[0].astype(BF16), final_norm_gain[None])
```

```python
import functools
import math

import numpy as np
import jax
import jax.numpy as jnp
from jax import lax
from jax.experimental import pallas as pl
from jax.experimental.pallas import tpu as pltpu

F32 = jnp.float32
BF16 = jnp.bfloat16
I32 = jnp.int32

HEAD_DIM = 64
N_HEADS = 8
WIDTH = N_HEADS * HEAD_DIM
N_PAIRS = N_HEADS // 2
ROT_DIM = HEAD_DIM // 4
ROPE_THETA = 500000.0
KV_LATENT = 128
IDX_HEADS = 4
IDX_DIM = 64
TOPK_MAX = 256
EPS = 1e-6
PATTERNS = ((128, 1), (512, 4), (2048, 16))

LANES = 128
SUBLANES = 8
MXU_DEPTH = 256
KBLK = MXU_DEPTH
TQ_A = LANES
TQ_B = 2 * LANES
TM = 512
NEG_BIG = -1e30
VMEM_LIMIT = 56 * 1024 * 1024
Q_SCALE = HEAD_DIM ** -0.5 * math.log2(math.e)

D_IN_PACKED = 28 * LANES


def _dot(a, b):
    return jnp.dot(a, b, preferred_element_type=F32)


def _dot_nt(a, b):
    return lax.dot_general(a, b, (((1,), (1,)), ((), ())), preferred_element_type=F32)


def _split_head_pair(xp):
    lane = lax.broadcasted_iota(I32, xp.shape, 1)
    zero = jnp.zeros_like(xp)
    return jnp.where(lane < HEAD_DIM, xp, zero), jnp.where(lane >= HEAD_DIM, xp, zero)


def _proj_kernel(x_ref, gin_ref, w_ref, cos_ref, sina_ref, sinb_ref, gkv_ref, wuk_ref, gik_ref,
                 qa_ref, ka_ref, vat_ref, zat_ref, qb_ref, zbt_ref, iq_ref, kbd_ref, ckvt_ref,
                 ikd_ref, iwt_ref):
    tm = x_ref.shape[1]
    nblk = tm // KBLK
    x = x_ref[0]
    ms = jnp.mean(x * x, axis=-1, keepdims=True)
    u = (x * lax.rsqrt(ms + EPS) * gin_ref[...]).astype(BF16)
    cosf, sina, sinb = cos_ref[0], sina_ref[0], sinb_ref[0]

    def rope(t):
        return t * cosf + pltpu.roll(t, LANES - ROT_DIM // 2, 1) * sina + pltpu.roll(t, ROT_DIM // 2, 1) * sinb

    def proj(unit):
        return _dot(u, w_ref[:, unit * LANES:(unit + 2) * LANES])

    def halves(acc):
        return acc[:, :LANES], acc[:, LANES:]

    def store_t(ref, unit, t):
        tt = t.T.astype(BF16)
        for i in range(nblk):
            ref[0, i, unit * LANES:(unit + 1) * LANES, :] = tt[:, i * KBLK:(i + 1) * KBLK]

    for pair in range(2):
        for half, t in enumerate(halves(proj(2 * pair))):
            c = (2 * pair + half) * LANES
            qa_ref[0, :, c:c + LANES] = (rope(t) * Q_SCALE).astype(BF16)
        for half, t in enumerate(halves(proj(4 + 2 * pair))):
            c = (2 * pair + half) * LANES
            ka_ref[0, :, c:c + LANES] = rope(t).astype(BF16)
        for half, t in enumerate(halves(proj(8 + 2 * pair))):
            store_t(vat_ref, 2 * pair + half, t)
        for half, t in enumerate(halves(proj(12 + 2 * pair))):
            c = (2 * pair + half) * LANES
            zat_ref[0, c:c + LANES, :] = t.T.astype(BF16)
        for half, t in enumerate(halves(proj(16 + 2 * pair))):
            c = (2 * pair + half) * LANES
            qb_ref[0, :, c:c + LANES] = (rope(t) * Q_SCALE).astype(BF16)
        for half, t in enumerate(halves(proj(20 + 2 * pair))):
            c = (2 * pair + half) * LANES
            zbt_ref[0, c:c + LANES, :] = t.T.astype(BF16)
    for half, t in enumerate(halves(proj(24))):
        iq_ref[0, :, half * LANES:(half + 1) * LANES] = rope(t).astype(BF16)

    ckv, last = halves(proj(26))
    ckv = ckv * lax.rsqrt(jnp.mean(ckv * ckv, axis=-1, keepdims=True) + EPS) * gkv_ref[...]
    store_t(ckvt_ref, 0, ckv)
    kbd_ref[0] = rope(_dot(ckv.astype(BF16), wuk_ref[...])).astype(BF16)
    lane = lax.broadcasted_iota(I32, last.shape, 1)
    ik_ms = jnp.sum(jnp.where(lane < IDX_DIM, last * last, 0.0), axis=-1, keepdims=True) / IDX_DIM
    ik = rope(last * lax.rsqrt(ik_ms + EPS) * gik_ref[...])
    ikd_ref[0] = (ik + pltpu.roll(ik, IDX_DIM, 1)).astype(BF16)
    iwt_ref[0] = last.T[IDX_DIM:IDX_DIM + SUBLANES, :] * (IDX_HEADS ** -0.5 * IDX_DIM ** -0.5)


def _mix_a_bias_tables(tq, tk):
    o_min = -(tq - LANES)
    o_last = PATTERNS[1][0] + tk - LANES
    offsets = list(range(o_min, o_last + 1, LANES)) + [o_last + LANES]
    qi = np.arange(tq)[None, :]
    kj = np.arange(tk)[:, None]
    tables = []
    for o in offsets:
        d = o + qi - kj
        mult = np.zeros(d.shape, np.int64)
        for window, dil in PATTERNS:
            mult += ((d >= 0) & (d % dil == 0) & (d <= window)).astype(np.int64)
        tables.append(np.where(mult > 0, np.log2(np.maximum(mult, 1)), NEG_BIG))
    return np.stack(tables).astype(np.float32), o_min


def _mix_a_kernel(o_min, n_tables, qa_ref, ka_ref, vat_ref, zat_ref, gain_ref, bias_ref,
                  out_ref, qm_ref, s_ref, pt_ref, acc_ref):
    tq = qa_ref.shape[1]
    j = pl.program_id(1)
    n_kb = (j * tq + tq - 1) // KBLK + 1
    for p in range(N_PAIRS):
        qm_ref[2 * p * tq:(2 * p + 1) * tq, :], qm_ref[(2 * p + 1) * tq:(2 * p + 2) * tq, :] = (
            _split_head_pair(qa_ref[0, :, p * LANES:(p + 1) * LANES]))
    acc_ref[...] = jnp.zeros_like(acc_ref)
    pt_ref[...] = jnp.zeros_like(pt_ref)

    def scores(kb, p):
        k0 = pl.multiple_of(kb * KBLK, KBLK)
        return _dot_nt(ka_ref[0, pl.ds(k0, KBLK), p * LANES:(p + 1) * LANES],
                       qm_ref[2 * p * tq:(2 * p + 2) * tq, :])

    def accumulate(kb, p, alpha):
        vt = vat_ref[0, kb, p * LANES:(p + 1) * LANES, :]
        r0 = p * LANES
        acc_ref[r0:r0 + HEAD_DIM, :] = (acc_ref[r0:r0 + HEAD_DIM, :] * alpha[:, :tq]
                                        + _dot(vt[:HEAD_DIM], pt_ref[p, :, :tq]))
        acc_ref[r0 + HEAD_DIM:r0 + LANES, :] = (acc_ref[r0 + HEAD_DIM:r0 + LANES, :] * alpha[:, tq:]
                                                + _dot(vt[HEAD_DIM:], pt_ref[p, :, tq:]))

    for p in range(N_PAIRS):
        s_ref[p] = scores(0, p)

    def body(kb, carry):
        ms, ls, alphas = carry
        prev = jnp.maximum(kb - 1, 0)
        nxt = jnp.minimum(kb + 1, n_kb - 1)
        bias = bias_ref[jnp.minimum((j * tq - kb * KBLK - o_min) // LANES, n_tables - 1)]
        bias2 = jnp.concatenate([bias, bias], axis=1)
        new_ms, new_ls, new_alphas = [], [], []
        for p in range(N_PAIRS):
            accumulate(prev, p, alphas[p])
            s = s_ref[p] + bias2
            m_new = jnp.maximum(ms[p], jnp.max(s, axis=0, keepdims=True))
            alpha = jnp.exp2(ms[p] - m_new)
            pt = jnp.exp2(s - m_new)
            new_ms.append(m_new)
            new_ls.append(alpha * ls[p] + jnp.sum(pt, axis=0, keepdims=True))
            new_alphas.append(alpha)
            pt_ref[p] = pt.astype(BF16)
            s_ref[p] = scores(nxt, p)
        return tuple(new_ms), tuple(new_ls), tuple(new_alphas)

    init = (tuple(jnp.full((1, 2 * tq), NEG_BIG, F32) for _ in range(N_PAIRS)),
            tuple(jnp.zeros((1, 2 * tq), F32) for _ in range(N_PAIRS)),
            tuple(jnp.ones((1, 2 * tq), F32) for _ in range(N_PAIRS)))
    _, ls, alphas = lax.fori_loop(0, n_kb, body, init)
    for p in range(N_PAIRS):
        accumulate(n_kb - 1, p, alphas[p])
        r0 = p * LANES
        acc_ref[r0:r0 + HEAD_DIM, :] = acc_ref[r0:r0 + HEAD_DIM, :] / ls[p][:, :tq]
        acc_ref[r0 + HEAD_DIM:r0 + LANES, :] = acc_ref[r0 + HEAD_DIM:r0 + LANES, :] / ls[p][:, tq:]

    z = zat_ref[0].astype(F32)
    g = acc_ref[...] * (z / (1.0 + jnp.exp(-z)))
    ms = jnp.mean(g * g, axis=0, keepdims=True)
    out_ref[0] = (g * lax.rsqrt(ms + EPS) * gain_ref[...]).astype(BF16)


def _mix_b_kernel(topk, qb_ref, iq_ref, iwt_ref, zbt_ref, ikd_ref, kbd_ref, ckvt_ref, wuvt_ref,
                  gain_ref, tri_ref, out_ref, sc_ref, bias_ref, iqm_ref, rhs_ref, raw_ref, rank_ref,
                  s_ref, pt_ref, acc_ref, g_ref):
    tq = qb_ref.shape[1]
    n_grp = tq // LANES
    iw_cols = IDX_HEADS * LANES
    hd_cols = N_HEADS * LANES
    j = pl.program_id(1)
    n_kb = (j * tq + tq - 1) // KBLK + 1
    row = lax.broadcasted_iota(I32, (KBLK, tq), 0)
    t_pos = j * tq + lax.broadcasted_iota(I32, (KBLK, tq), 1)

    def key_rows(ref, kb):
        return ref[0, pl.ds(pl.multiple_of(kb * KBLK, KBLK), KBLK), :]

    def next_block(kb):
        return jnp.minimum(kb + 1, n_kb - 1)

    def group(g):
        return slice(g * LANES, (g + 1) * LANES)


    for g in range(n_grp):
        for p in range(IDX_HEADS // 2):
            iqm_ref[g, 2 * p * LANES:(2 * p + 1) * LANES, :], iqm_ref[g, (2 * p + 1) * LANES:(2 * p + 2) * LANES, :] = (
                _split_head_pair(iq_ref[0, group(g), p * LANES:(p + 1) * LANES]))

    def store_raw_scores(kb):
        keys = key_rows(ikd_ref, kb)
        for g in range(n_grp):
            raw_ref[:, g * iw_cols:(g + 1) * iw_cols] = _dot_nt(keys, iqm_ref[g])

    store_raw_scores(0)

    def score_body(kb, carry):
        for g in range(n_grp):
            c0 = g * iw_cols
            sc = jnp.maximum(raw_ref[:, c0:c0 + LANES], 0.0) * iwt_ref[0, 0:1, group(g)]
            for h in range(1, IDX_HEADS):
                sc = sc + (jnp.maximum(raw_ref[:, c0 + h * LANES:c0 + (h + 1) * LANES], 0.0)
                           * iwt_ref[0, h:h + 1, group(g)])
            causal = (kb * KBLK + lax.broadcasted_iota(I32, (KBLK, LANES), 0)
                      <= j * tq + g * LANES + lax.broadcasted_iota(I32, (KBLK, LANES), 1))
            sc_ref[kb, :, group(g)] = jnp.where(causal, sc, -jnp.inf)
        store_raw_scores(next_block(kb))
        return carry

    lax.fori_loop(0, n_kb, score_body, 0)

    def count(pred):
        def body(kb, acc):
            hit = pred(sc_ref[kb]).astype(I32)
            return acc + jnp.sum(hit.reshape(KBLK // SUBLANES, SUBLANES, tq), axis=0)
        acc = lax.fori_loop(0, n_kb, body, jnp.zeros((SUBLANES, tq), I32))
        return jnp.sum(acc, axis=0, keepdims=True)

    def key_to_float(key):
        bits = key ^ ((key >> 31) & jnp.int32(0x7FFFFFFF))
        return lax.bitcast_convert_type(bits, F32)

    def value_step(i, prefix):
        cand = prefix + (jnp.int32(1) << (31 - i))
        thr = key_to_float(cand)
        return jnp.where(count(lambda s: s >= thr) >= topk, cand, prefix)

    prefix = lax.fori_loop(0, 32, value_step, jnp.full((1, tq), -2 ** 31, I32))
    thr = key_to_float(prefix)
    need = (topk - count(lambda s: s > thr)).astype(F32)

    def tie_ranks(kb):
        tie = jnp.where(sc_ref[kb] == thr, 1.0, 0.0).astype(BF16)
        return _dot(tri_ref[...], tie)

    rank_ref[...] = tie_ranks(0)

    def bias_body(kb, ties_before):
        sc = sc_ref[kb]
        rank = rank_ref[...]
        take = (sc == thr) & (rank + ties_before <= need)
        sel = (sc > thr) | take | (t_pos < topk)
        bias_ref[kb] = jnp.where(sel & (kb * KBLK + row <= t_pos), 0.0, NEG_BIG).astype(BF16)
        rank_ref[...] = tie_ranks(next_block(kb))
        return ties_before + rank[KBLK - 1:KBLK, :]

    lax.fori_loop(0, n_kb, bias_body, jnp.zeros((1, tq), F32))

    eye = (lax.broadcasted_iota(I32, (LANES, LANES), 0)
           == lax.broadcasted_iota(I32, (LANES, LANES), 1)).astype(BF16)
    for g in range(n_grp):
        for p in range(N_PAIRS):
            rhs_ref[g, 2 * p * LANES:(2 * p + 1) * LANES, :LANES], rhs_ref[g, (2 * p + 1) * LANES:(2 * p + 2) * LANES, :LANES] = (
                _split_head_pair(qb_ref[0, group(g), p * LANES:(p + 1) * LANES]))
        for h in range(N_HEADS):
            rhs_ref[g, h * LANES:(h + 1) * LANES, LANES:] = eye
    acc_ref[...] = jnp.zeros_like(acc_ref)
    pt_ref[...] = jnp.zeros_like(pt_ref)

    def store_masked_scores(kb):
        keys = key_rows(kbd_ref, kb)
        for g in range(n_grp):
            lhs = jnp.concatenate([keys, bias_ref[kb, :, group(g)]], axis=1)
            s_ref[:, g * hd_cols:(g + 1) * hd_cols] = _dot_nt(lhs, rhs_ref[g])

    def accumulate(kb, alpha):
        acc_ref[...] = acc_ref[...] * alpha + _dot(ckvt_ref[0, kb], pt_ref[...])

    store_masked_scores(0)

    def attn_body(kb, carry):
        m, l, alpha = carry
        accumulate(jnp.maximum(kb - 1, 0), alpha)
        s = s_ref[...]
        m_new = jnp.maximum(m, jnp.max(s, axis=0, keepdims=True))
        alpha = jnp.exp2(m - m_new)
        pt = jnp.exp2(s - m_new)
        l = alpha * l + jnp.sum(pt, axis=0, keepdims=True)
        pt_ref[...] = pt.astype(BF16)
        store_masked_scores(next_block(kb))
        return m_new, l, alpha

    n_col = n_grp * hd_cols
    init = (jnp.full((1, n_col), NEG_BIG, F32), jnp.zeros((1, n_col), F32), jnp.ones((1, n_col), F32))
    _, l, alpha = lax.fori_loop(0, n_kb, attn_body, init)
    accumulate(n_kb - 1, alpha)
    o_lat = (acc_ref[...] / l).astype(BF16)
    for g in range(n_grp):
        for h in range(N_HEADS):
            c0 = g * hd_cols + h * LANES
            g_ref[h * HEAD_DIM:(h + 1) * HEAD_DIM, group(g)] = _dot(wuvt_ref[h], o_lat[:, c0:c0 + LANES])

    z = zbt_ref[0].astype(F32)
    gated = g_ref[...] * (z / (1.0 + jnp.exp(-z)))
    ms = jnp.mean(gated * gated, axis=0, keepdims=True)
    out_ref[0] = (gated * lax.rsqrt(ms + EPS) * gain_ref[...]).astype(BF16)


def _out_kernel(yat_ref, ybt_ref, x_ref, w_ref, gain_ref, out_ref):
    y = jnp.concatenate([yat_ref[0].astype(F32).T, ybt_ref[0].astype(F32).T], axis=1).astype(BF16)
    h = x_ref[0] + _dot(y, w_ref[...])
    ms = jnp.mean(h * h, axis=-1, keepdims=True)
    out_ref[0] = h * lax.rsqrt(ms + EPS) * gain_ref[...]


def _rope_tables(positions):
    half = ROT_DIM // 2
    inv_freq = ROPE_THETA ** (-jnp.arange(0, ROT_DIM, 2, dtype=F32) / ROT_DIM)
    ang = positions.astype(F32)[..., None] * inv_freq
    cos, sin = jnp.cos(ang), jnp.sin(ang)
    c = np.arange(LANES) % HEAD_DIM
    sel = c % half
    cosf = jnp.where(c < ROT_DIM, cos[..., sel], 1.0)
    sina = jnp.where(c < half, -sin[..., sel], 0.0)
    sinb = jnp.where((c >= half) & (c < ROT_DIM), sin[..., sel], 0.0)
    return cosf, sina, sinb


def _pack_w_in(w):
    offs = np.cumsum([0, WIDTH, WIDTH, WIDTH, WIDTH, WIDTH, KV_LATENT, WIDTH,
                      IDX_HEADS * IDX_DIM, IDX_DIM, IDX_HEADS])
    qa, ka, va, za, qb, ckv, zb, iq, ik, iw = [w[:, offs[i]:offs[i + 1]] for i in range(10)]
    pad = jnp.zeros((w.shape[0], LANES - IDX_DIM - IDX_HEADS), w.dtype)
    return jnp.concatenate([qa, ka, va, za, qb, zb, iq, ckv, ik, iw, pad], axis=1)


def _full(shape):
    return pl.BlockSpec(shape, lambda *_: (0,) * len(shape))


@jax.jit
def kernel(x, positions, norm_in_gain, w_in, kv_norm_gain, w_uk, w_uv, idx_k_norm_gain,
           branch_norm_gain_a, branch_norm_gain_b, w_out, final_norm_gain):
    bsz, seq, d_model = x.shape
    assert w_in.shape[0] == 1
    assert seq % KBLK == 0 and seq <= PATTERNS[-1][0]
    n_blk = seq // KBLK
    topk = min(TOPK_MAX, seq // 4)
    tm = min(TM, seq)
    tqa, tqb = TQ_A, TQ_B
    params = pltpu.CompilerParams(dimension_semantics=("arbitrary", "arbitrary"),
                                  vmem_limit_bytes=VMEM_LIMIT)

    cosf, sina, sinb = _rope_tables(positions)
    w_packed = _pack_w_in(w_in[0]).astype(BF16)
    wuk_dup = jnp.concatenate([w_uk[0], w_uk[0]], axis=1).astype(BF16)
    gik = jnp.concatenate([idx_k_norm_gain[0], jnp.zeros((LANES - IDX_DIM,), F32)])[None]
    wuv_t = jnp.swapaxes(w_uv[0], 1, 2).astype(BF16)
    gain_a = jnp.broadcast_to(branch_norm_gain_a[0][:, None], (WIDTH, tqa))
    gain_b = jnp.broadcast_to(branch_norm_gain_b[0][:, None], (WIDTH, tqb))

    row_blk = lambda c: pl.BlockSpec((1, tm, c), lambda b, i: (b, i, 0))
    col_blk = lambda c: pl.BlockSpec((1, c, tm), lambda b, i: (b, 0, i))
    key_blk = lambda c: pl.BlockSpec((1, tm // KBLK, c, KBLK), lambda b, i: (b, i, 0, 0))
    qa, ka, vat, zat, qb, zbt, iq, kbd, ckvt, ikd, iwt = pl.pallas_call(
        _proj_kernel,
        grid=(bsz, seq // tm),
        in_specs=[row_blk(d_model), _full((1, d_model)), _full((d_model, D_IN_PACKED)),
                  row_blk(LANES), row_blk(LANES), row_blk(LANES),
                  _full((1, KV_LATENT)), _full((KV_LATENT, LANES)), _full((1, LANES))],
        out_specs=[row_blk(WIDTH), row_blk(WIDTH), key_blk(WIDTH), col_blk(WIDTH),
                   row_blk(WIDTH), col_blk(WIDTH), row_blk(2 * LANES), row_blk(LANES),
                   key_blk(KV_LATENT), row_blk(LANES), col_blk(SUBLANES)],
        out_shape=[jax.ShapeDtypeStruct((bsz, seq, WIDTH), BF16),
                   jax.ShapeDtypeStruct((bsz, seq, WIDTH), BF16),
                   jax.ShapeDtypeStruct((bsz, n_blk, WIDTH, KBLK), BF16),
                   jax.ShapeDtypeStruct((bsz, WIDTH, seq), BF16),
                   jax.ShapeDtypeStruct((bsz, seq, WIDTH), BF16),
                   jax.ShapeDtypeStruct((bsz, WIDTH, seq), BF16),
                   jax.ShapeDtypeStruct((bsz, seq, 2 * LANES), BF16),
                   jax.ShapeDtypeStruct((bsz, seq, LANES), BF16),
                   jax.ShapeDtypeStruct((bsz, n_blk, KV_LATENT, KBLK), BF16),
                   jax.ShapeDtypeStruct((bsz, seq, LANES), BF16),
                   jax.ShapeDtypeStruct((bsz, SUBLANES, seq), F32)],
        compiler_params=params, name="proj",
    )(x, norm_in_gain[0][None], w_packed, cosf, sina, sinb, kv_norm_gain[0][None], wuk_dup, gik)

    q_row = lambda tq, c: pl.BlockSpec((1, tq, c), lambda b, i: (b, i, 0))
    q_col = lambda tq, c: pl.BlockSpec((1, c, tq), lambda b, i: (b, 0, i))
    per_b3 = lambda s1, s2: pl.BlockSpec((1, s1, s2), lambda b, i: (b, 0, 0))
    per_b4 = lambda s1, s2, s3: pl.BlockSpec((1, s1, s2, s3), lambda b, i: (b, 0, 0, 0))

    bias_np, o_min = _mix_a_bias_tables(tqa, KBLK)
    yat = pl.pallas_call(
        functools.partial(_mix_a_kernel, o_min, bias_np.shape[0]),
        grid=(bsz, seq // tqa),
        in_specs=[q_row(tqa, WIDTH), per_b3(seq, WIDTH), per_b4(n_blk, WIDTH, KBLK), q_col(tqa, WIDTH),
                  _full((WIDTH, tqa)), _full(bias_np.shape)],
        out_specs=q_col(tqa, WIDTH),
        out_shape=jax.ShapeDtypeStruct((bsz, WIDTH, seq), BF16),
        scratch_shapes=[pltpu.VMEM((N_HEADS * tqa, LANES), BF16),
                        pltpu.VMEM((N_PAIRS, KBLK, 2 * tqa), F32),
                        pltpu.VMEM((N_PAIRS, KBLK, 2 * tqa), BF16),
                        pltpu.VMEM((WIDTH, tqa), F32)],
        compiler_params=params, name="mix_a",
    )(qa, ka, vat, zat, gain_a, jnp.asarray(bias_np))

    n_grp = tqb // LANES
    ybt = pl.pallas_call(
        functools.partial(_mix_b_kernel, topk),
        grid=(bsz, seq // tqb),
        in_specs=[q_row(tqb, WIDTH), q_row(tqb, 2 * LANES), q_col(tqb, SUBLANES), q_col(tqb, WIDTH),
                  per_b3(seq, LANES), per_b3(seq, LANES), per_b4(n_blk, KV_LATENT, KBLK),
                  _full((N_HEADS, HEAD_DIM, KV_LATENT)), _full((WIDTH, tqb)), _full((KBLK, KBLK))],
        out_specs=q_col(tqb, WIDTH),
        out_shape=jax.ShapeDtypeStruct((bsz, WIDTH, seq), BF16),
        scratch_shapes=[pltpu.VMEM((n_blk, KBLK, tqb), F32),
                        pltpu.VMEM((n_blk, KBLK, tqb), BF16),
                        pltpu.VMEM((n_grp, IDX_HEADS * LANES, LANES), BF16),
                        pltpu.VMEM((n_grp, N_HEADS * LANES, 2 * LANES), BF16),
                        pltpu.VMEM((KBLK, n_grp * IDX_HEADS * LANES), F32),
                        pltpu.VMEM((KBLK, tqb), F32),
                        pltpu.VMEM((KBLK, n_grp * N_HEADS * LANES), F32),
                        pltpu.VMEM((KBLK, n_grp * N_HEADS * LANES), BF16),
                        pltpu.VMEM((KV_LATENT, n_grp * N_HEADS * LANES), F32),
                        pltpu.VMEM((WIDTH, tqb), F32)],
        compiler_params=params, name="mix_b",
    )(qb, iq, iwt, zbt, ikd, kbd, ckvt, wuv_t, gain_b,
      jnp.asarray(np.tril(np.ones((KBLK, KBLK), np.float32)), BF16))

    return pl.pallas_call(
        _out_kernel,
        grid=(bsz, seq // tm),
        in_specs=[col_blk(WIDTH), col_blk(WIDTH), row_blk(d_model),
                  _full((2 * WIDTH, d_model)), _full((1, d_model))],
        out_specs=row_blk(d_model),
        out_shape=jax.ShapeDtypeStruct((bsz, seq, d_model), F32),
        compiler_params=params, name="out_proj",
    )(yat, ybt, x, w_out[0].astype(BF16), final_norm_gain[None])
```

```python
import functools
import math

import numpy as np
import jax
import jax.numpy as jnp
from jax import lax
from jax.experimental import pallas as pl
from jax.experimental.pallas import tpu as pltpu

F32 = jnp.float32
BF16 = jnp.bfloat16
I32 = jnp.int32

HEAD_DIM = 64
N_HEADS = 8
WIDTH = N_HEADS * HEAD_DIM
N_PAIRS = N_HEADS // 2
ROT_DIM = HEAD_DIM // 4
ROPE_THETA = 500000.0
KV_LATENT = 128
IDX_HEADS = 4
IDX_DIM = 64
TOPK_MAX = 256
EPS = 1e-6
PATTERNS = ((128, 1), (512, 4), (2048, 16))

LANES = 128
SUBLANES = 8
MXU_DEPTH = 256
KBLK = MXU_DEPTH
TQ_A = 2 * LANES
TQ_B = 2 * LANES
TM = 512
NEG_BIG = -1e30
VMEM_LIMIT = 56 * 1024 * 1024
Q_SCALE = HEAD_DIM ** -0.5 * math.log2(math.e)

U_QA, U_KA, U_VA, U_ZA, U_QB, U_CKV, U_ZB, U_IQ, U_LAST, N_UNITS = 0, 4, 8, 12, 16, 20, 21, 25, 27, 28
D_IN_PADDED = N_UNITS * LANES


def _dot(a, b):
    return jnp.dot(a, b, preferred_element_type=F32)


def _dot_nt(a, b):
    return lax.dot_general(a, b, (((1,), (1,)), ((), ())), preferred_element_type=F32)


def _split_head_pair(xp):
    lane = lax.broadcasted_iota(I32, xp.shape, 1)
    zero = jnp.zeros_like(xp)
    return jnp.where(lane < HEAD_DIM, xp, zero), jnp.where(lane >= HEAD_DIM, xp, zero)


def _proj_kernel(x_ref, gin_ref, w_ref, rope_ref, gkv_ref, wuk_ref, gik_ref,
                 qa_ref, ka_ref, vat_ref, zat_ref, qb_ref, zbt_ref, iq_ref, kbd_ref, ckvt_ref,
                 ikd_ref, iwt_ref):
    tm = x_ref.shape[1]
    nblk = tm // KBLK
    x = x_ref[0]
    ms = jnp.mean(x * x, axis=-1, keepdims=True)
    u = (x * lax.rsqrt(ms + EPS) * gin_ref[...]).astype(BF16)
    cosf, sina, sinb = (rope_ref[0, :, i * LANES:(i + 1) * LANES] for i in range(3))

    def rope(t):
        return t * cosf + pltpu.roll(t, LANES - ROT_DIM // 2, 1) * sina + pltpu.roll(t, ROT_DIM // 2, 1) * sinb

    def proj(unit):
        return _dot(u, w_ref[:, unit * LANES:(unit + 2) * LANES])

    def store_t(ref, unit, t):
        tt = t.T.astype(BF16)
        for i in range(nblk):
            ref[0, i, unit * LANES:(unit + 1) * LANES, :] = tt[:, i * KBLK:(i + 1) * KBLK]

    def lanes(unit):
        return slice(unit * LANES, (unit + 1) * LANES)

    def finish(unit, t):
        if unit < U_KA:
            qa_ref[0, :, lanes(unit - U_QA)] = (rope(t) * Q_SCALE).astype(BF16)
        elif unit < U_VA:
            ka_ref[0, :, lanes(unit - U_KA)] = rope(t).astype(BF16)
        elif unit < U_ZA:
            store_t(vat_ref, unit - U_VA, t)
        elif unit < U_QB:
            zat_ref[0, lanes(unit - U_ZA), :] = t.T.astype(BF16)
        elif unit < U_CKV:
            qb_ref[0, :, lanes(unit - U_QB)] = (rope(t) * Q_SCALE).astype(BF16)
        elif unit == U_CKV:
            ckv = t * lax.rsqrt(jnp.mean(t * t, axis=-1, keepdims=True) + EPS) * gkv_ref[...]
            store_t(ckvt_ref, 0, ckv)
            kbd_ref[0] = rope(_dot(ckv.astype(BF16), wuk_ref[...])).astype(BF16)
        elif unit < U_IQ:
            zbt_ref[0, lanes(unit - U_ZB), :] = t.T.astype(BF16)
        elif unit < U_LAST:
            iq_ref[0, :, lanes(unit - U_IQ)] = rope(t).astype(BF16)
        else:
            lane = lax.broadcasted_iota(I32, t.shape, 1)
            ik_ms = jnp.sum(jnp.where(lane < IDX_DIM, t * t, 0.0), axis=-1, keepdims=True) / IDX_DIM
            ik = rope(t * lax.rsqrt(ik_ms + EPS) * gik_ref[...])
            ikd_ref[0] = (ik + pltpu.roll(ik, IDX_DIM, 1)).astype(BF16)
            iwt_ref[0] = t.T[IDX_DIM:IDX_DIM + SUBLANES, :] * (IDX_HEADS ** -0.5 * IDX_DIM ** -0.5)

    for unit in range(0, N_UNITS, 2):
        acc = proj(unit)
        finish(unit, acc[:, :LANES])
        finish(unit + 1, acc[:, LANES:])


def _mix_a_bias_tables(tq, tk):
    assert tq == LANES
    o_last = PATTERNS[1][0] + tk - LANES
    offsets = list(range(0, o_last + 1, LANES)) + [o_last + LANES]
    qi = np.arange(tq)[None, :]
    kj = np.arange(tk)[:, None]
    tables = []
    for o in offsets:
        d = o + qi - kj
        mult = np.zeros(d.shape, np.int64)
        for window, dil in PATTERNS:
            mult += ((d >= 0) & (d % dil == 0) & (d <= window)).astype(np.int64)
        tables.append(np.where(mult > 0, np.log2(np.maximum(mult, 1)), NEG_BIG))
    return np.stack(tables).astype(np.float32)


def _mix_a_kernel(n_tables, qa_ref, ka_ref, vat_ref, zat_ref, gain_ref, bias_ref,
                  out_ref, qm_ref, s_ref, pt_ref, acc_ref):
    tq = qa_ref.shape[1]
    n_grp = tq // LANES
    j = pl.program_id(1)
    n_kb = (j * tq + tq - 1) // KBLK + 1
    chains = [(g, p) for g in range(n_grp) for p in range(N_PAIRS)]

    def group(g):
        return slice(g * LANES, (g + 1) * LANES)

    for g, p in chains:
        qm_ref[g, 2 * p * LANES:(2 * p + 1) * LANES, :], qm_ref[g, (2 * p + 1) * LANES:(2 * p + 2) * LANES, :] = (
            _split_head_pair(qa_ref[0, group(g), p * LANES:(p + 1) * LANES]))
    acc_ref[...] = jnp.zeros_like(acc_ref)
    pt_ref[...] = jnp.zeros_like(pt_ref)

    def scores(kb, g, p):
        k0 = pl.multiple_of(kb * KBLK, KBLK)
        return _dot_nt(ka_ref[0, pl.ds(k0, KBLK), p * LANES:(p + 1) * LANES],
                       qm_ref[g, 2 * p * LANES:(2 * p + 2) * LANES, :])

    def accumulate(kb, g, p, alpha):
        vt = vat_ref[0, kb, p * LANES:(p + 1) * LANES, :]
        r0 = p * LANES
        acc_ref[r0:r0 + HEAD_DIM, group(g)] = (acc_ref[r0:r0 + HEAD_DIM, group(g)] * alpha[:, :LANES]
                                               + _dot(vt[:HEAD_DIM], pt_ref[g, p, :, :LANES]))
        acc_ref[r0 + HEAD_DIM:r0 + LANES, group(g)] = (acc_ref[r0 + HEAD_DIM:r0 + LANES, group(g)] * alpha[:, LANES:]
                                                       + _dot(vt[HEAD_DIM:], pt_ref[g, p, :, LANES:]))

    for g, p in chains:
        s_ref[g, p] = scores(0, g, p)

    def body(kb, carry):
        ms, ls, alphas = carry
        prev = jnp.maximum(kb - 1, 0)
        nxt = jnp.minimum(kb + 1, n_kb - 1)
        new_ms, new_ls, new_alphas = [], [], []
        for c, (g, p) in enumerate(chains):
            bias = bias_ref[jnp.minimum((j * tq + g * LANES - kb * KBLK) // LANES, n_tables - 1)]
            accumulate(prev, g, p, alphas[c])
            s = s_ref[g, p] + jnp.concatenate([bias, bias], axis=1)
            m_new = jnp.maximum(ms[c], jnp.max(s, axis=0, keepdims=True))
            alpha = jnp.exp2(ms[c] - m_new)
            pt = jnp.exp2(s - m_new)
            new_ms.append(m_new)
            new_ls.append(alpha * ls[c] + jnp.sum(pt, axis=0, keepdims=True))
            new_alphas.append(alpha)
            pt_ref[g, p] = pt.astype(BF16)
            s_ref[g, p] = scores(nxt, g, p)
        return tuple(new_ms), tuple(new_ls), tuple(new_alphas)

    init = (tuple(jnp.full((1, 2 * LANES), NEG_BIG, F32) for _ in chains),
            tuple(jnp.zeros((1, 2 * LANES), F32) for _ in chains),
            tuple(jnp.ones((1, 2 * LANES), F32) for _ in chains))
    _, ls, alphas = lax.fori_loop(0, n_kb, body, init)
    for c, (g, p) in enumerate(chains):
        accumulate(n_kb - 1, g, p, alphas[c])
        r0 = p * LANES
        acc_ref[r0:r0 + HEAD_DIM, group(g)] = acc_ref[r0:r0 + HEAD_DIM, group(g)] / ls[c][:, :LANES]
        acc_ref[r0 + HEAD_DIM:r0 + LANES, group(g)] = acc_ref[r0 + HEAD_DIM:r0 + LANES, group(g)] / ls[c][:, LANES:]

    z = zat_ref[0].astype(F32)
    g = acc_ref[...] * (z / (1.0 + jnp.exp(-z)))
    ms = jnp.mean(g * g, axis=0, keepdims=True)
    out_ref[0] = (g * lax.rsqrt(ms + EPS) * gain_ref[...]).astype(BF16)


def _mix_b_kernel(topk, qb_ref, iq_ref, iwt_ref, zbt_ref, ikd_ref, kbd_ref, ckvt_ref, wuvt_ref,
                  gain_ref, tri_ref, out_ref, sc_ref, bias_ref, iqm_ref, rhs_ref, raw_ref, rank_ref,
                  s_ref, pt_ref, acc_ref, g_ref):
    tq = qb_ref.shape[1]
    n_grp = tq // LANES
    iw_cols = IDX_HEADS * LANES
    hd_cols = N_HEADS * LANES
    j = pl.program_id(1)
    n_kb = (j * tq + tq - 1) // KBLK + 1
    row = lax.broadcasted_iota(I32, (KBLK, tq), 0)
    t_pos = j * tq + lax.broadcasted_iota(I32, (KBLK, tq), 1)

    def key_rows(ref, kb):
        return ref[0, pl.ds(pl.multiple_of(kb * KBLK, KBLK), KBLK), :]

    def next_block(kb):
        return jnp.minimum(kb + 1, n_kb - 1)

    def group(g):
        return slice(g * LANES, (g + 1) * LANES)


    for g in range(n_grp):
        for p in range(IDX_HEADS // 2):
            iqm_ref[g, 2 * p * LANES:(2 * p + 1) * LANES, :], iqm_ref[g, (2 * p + 1) * LANES:(2 * p + 2) * LANES, :] = (
                _split_head_pair(iq_ref[0, group(g), p * LANES:(p + 1) * LANES]))

    def store_raw_scores(kb):
        keys = key_rows(ikd_ref, kb)
        for g in range(n_grp):
            raw_ref[:, g * iw_cols:(g + 1) * iw_cols] = _dot_nt(keys, iqm_ref[g])

    store_raw_scores(0)

    def score_body(kb, carry):
        for g in range(n_grp):
            c0 = g * iw_cols
            sc = jnp.maximum(raw_ref[:, c0:c0 + LANES], 0.0) * iwt_ref[0, 0:1, group(g)]
            for h in range(1, IDX_HEADS):
                sc = sc + (jnp.maximum(raw_ref[:, c0 + h * LANES:c0 + (h + 1) * LANES], 0.0)
                           * iwt_ref[0, h:h + 1, group(g)])
            causal = (kb * KBLK + lax.broadcasted_iota(I32, (KBLK, LANES), 0)
                      <= j * tq + g * LANES + lax.broadcasted_iota(I32, (KBLK, LANES), 1))
            sc_ref[kb, :, group(g)] = jnp.where(causal, sc, -jnp.inf)
        store_raw_scores(next_block(kb))
        return carry

    lax.fori_loop(0, n_kb, score_body, 0)

    def count(pred):
        def body(kb, acc):
            hit = pred(sc_ref[kb]).astype(I32)
            return acc + jnp.sum(hit.reshape(KBLK // SUBLANES, SUBLANES, tq), axis=0)
        acc = lax.fori_loop(0, n_kb, body, jnp.zeros((SUBLANES, tq), I32))
        return jnp.sum(acc, axis=0, keepdims=True)

    def key_to_float(key):
        bits = key ^ ((key >> 31) & jnp.int32(0x7FFFFFFF))
        return lax.bitcast_convert_type(bits, F32)

    def value_step(i, prefix):
        cand = prefix + (jnp.int32(1) << (31 - i))
        thr = key_to_float(cand)
        return jnp.where(count(lambda s: s >= thr) >= topk, cand, prefix)

    prefix = lax.fori_loop(0, 32, value_step, jnp.full((1, tq), -2 ** 31, I32))
    thr = key_to_float(prefix)
    need = (topk - count(lambda s: s > thr)).astype(F32)

    def tie_ranks(kb):
        tie = jnp.where(sc_ref[kb] == thr, 1.0, 0.0).astype(BF16)
        return _dot(tri_ref[...], tie)

    rank_ref[...] = tie_ranks(0)

    def bias_body(kb, ties_before):
        sc = sc_ref[kb]
        rank = rank_ref[...]
        take = (sc == thr) & (rank + ties_before <= need)
        sel = (sc > thr) | take | (t_pos < topk)
        bias_ref[kb] = jnp.where(sel & (kb * KBLK + row <= t_pos), 0.0, NEG_BIG).astype(BF16)
        rank_ref[...] = tie_ranks(next_block(kb))
        return ties_before + rank[KBLK - 1:KBLK, :]

    lax.fori_loop(0, n_kb, bias_body, jnp.zeros((1, tq), F32))

    eye = (lax.broadcasted_iota(I32, (LANES, LANES), 0)
           == lax.broadcasted_iota(I32, (LANES, LANES), 1)).astype(BF16)
    for g in range(n_grp):
        for p in range(N_PAIRS):
            rhs_ref[g, 2 * p * LANES:(2 * p + 1) * LANES, :LANES], rhs_ref[g, (2 * p + 1) * LANES:(2 * p + 2) * LANES, :LANES] = (
                _split_head_pair(qb_ref[0, group(g), p * LANES:(p + 1) * LANES]))
        for h in range(N_HEADS):
            rhs_ref[g, h * LANES:(h + 1) * LANES, LANES:] = eye
    acc_ref[...] = jnp.zeros_like(acc_ref)
    pt_ref[...] = jnp.zeros_like(pt_ref)

    def store_masked_scores(kb):
        keys = key_rows(kbd_ref, kb)
        for g in range(n_grp):
            lhs = jnp.concatenate([keys, bias_ref[kb, :, group(g)]], axis=1)
            s_ref[:, g * hd_cols:(g + 1) * hd_cols] = _dot_nt(lhs, rhs_ref[g])

    def accumulate(kb, alpha):
        acc_ref[...] = acc_ref[...] * alpha + _dot(ckvt_ref[0, kb], pt_ref[...])

    store_masked_scores(0)

    def attn_body(kb, carry):
        m, l, alpha = carry
        accumulate(jnp.maximum(kb - 1, 0), alpha)
        s = s_ref[...]
        m_new = jnp.maximum(m, jnp.max(s, axis=0, keepdims=True))
        alpha = jnp.exp2(m - m_new)
        pt = jnp.exp2(s - m_new)
        l = alpha * l + jnp.sum(pt, axis=0, keepdims=True)
        pt_ref[...] = pt.astype(BF16)
        store_masked_scores(next_block(kb))
        return m_new, l, alpha

    n_col = n_grp * hd_cols
    init = (jnp.full((1, n_col), NEG_BIG, F32), jnp.zeros((1, n_col), F32), jnp.ones((1, n_col), F32))
    _, l, alpha = lax.fori_loop(0, n_kb, attn_body, init)
    accumulate(n_kb - 1, alpha)
    o_lat = (acc_ref[...] / l).astype(BF16)
    for g in range(n_grp):
        for h in range(N_HEADS):
            c0 = g * hd_cols + h * LANES
            g_ref[h * HEAD_DIM:(h + 1) * HEAD_DIM, group(g)] = _dot(wuvt_ref[h], o_lat[:, c0:c0 + LANES])

    z = zbt_ref[0].astype(F32)
    gated = g_ref[...] * (z / (1.0 + jnp.exp(-z)))
    ms = jnp.mean(gated * gated, axis=0, keepdims=True)
    out_ref[0] = (gated * lax.rsqrt(ms + EPS) * gain_ref[...]).astype(BF16)


def _out_kernel(yat_ref, ybt_ref, x_ref, w_ref, gain_ref, out_ref):
    y = jnp.concatenate([yat_ref[0].astype(F32).T, ybt_ref[0].astype(F32).T], axis=1).astype(BF16)
    h = x_ref[0] + _dot(y, w_ref[...])
    ms = jnp.mean(h * h, axis=-1, keepdims=True)
    out_ref[0] = h * lax.rsqrt(ms + EPS) * gain_ref[...]


def _rope_table(positions):
    half = ROT_DIM // 2
    inv_freq = ROPE_THETA ** (-jnp.arange(0, ROT_DIM, 2, dtype=F32) / ROT_DIM)
    ang = positions.astype(F32)[..., None] * inv_freq
    c = np.arange(LANES) % HEAD_DIM
    hit = (np.arange(half)[:, None] == (c % half)[None, :])
    spread = np.zeros((2 * half, 3 * LANES), np.float32)
    spread[:half, :LANES] = hit & (c < ROT_DIM)
    spread[half:, LANES:2 * LANES] = -1.0 * (hit & (c < half))
    spread[half:, 2 * LANES:] = hit & (c >= half) & (c < ROT_DIM)
    ones = np.concatenate([(c >= ROT_DIM), np.zeros(2 * LANES, bool)]).astype(np.float32)
    trig = jnp.concatenate([jnp.cos(ang), jnp.sin(ang)], axis=-1)
    return jnp.einsum("bsf,fl->bsl", trig, jnp.asarray(spread), precision=lax.Precision.HIGHEST) + ones


def _full(shape):
    return pl.BlockSpec(shape, lambda *_: (0,) * len(shape))


@jax.jit
def kernel(x, positions, norm_in_gain, w_in, kv_norm_gain, w_uk, w_uv, idx_k_norm_gain,
           branch_norm_gain_a, branch_norm_gain_b, w_out, final_norm_gain):
    bsz, seq, d_model = x.shape
    assert w_in.shape[0] == 1
    assert seq % KBLK == 0 and seq <= PATTERNS[-1][0]
    n_blk = seq // KBLK
    topk = min(TOPK_MAX, seq // 4)
    tm = min(TM, seq)
    tqa, tqb = TQ_A, TQ_B
    params = pltpu.CompilerParams(dimension_semantics=("arbitrary", "arbitrary"),
                                  vmem_limit_bytes=VMEM_LIMIT)

    rope = _rope_table(positions)
    w_padded = jnp.pad(w_in[0].astype(BF16), ((0, 0), (0, D_IN_PADDED - w_in.shape[2])))
    wuk_dup = jnp.concatenate([w_uk[0], w_uk[0]], axis=1).astype(BF16)
    gik = jnp.concatenate([idx_k_norm_gain[0], jnp.zeros((LANES - IDX_DIM,), F32)])[None]
    wuv_t = jnp.swapaxes(w_uv[0], 1, 2).astype(BF16)
    gain_a = jnp.broadcast_to(branch_norm_gain_a[0][:, None], (WIDTH, tqa))
    gain_b = jnp.broadcast_to(branch_norm_gain_b[0][:, None], (WIDTH, tqb))

    row_blk = lambda c: pl.BlockSpec((1, tm, c), lambda b, i: (b, i, 0))
    col_blk = lambda c: pl.BlockSpec((1, c, tm), lambda b, i: (b, 0, i))
    key_blk = lambda c: pl.BlockSpec((1, tm // KBLK, c, KBLK), lambda b, i: (b, i, 0, 0))
    qa, ka, vat, zat, qb, zbt, iq, kbd, ckvt, ikd, iwt = pl.pallas_call(
        _proj_kernel,
        grid=(bsz, seq // tm),
        in_specs=[row_blk(d_model), _full((1, d_model)), _full((d_model, D_IN_PADDED)), row_blk(3 * LANES),
                  _full((1, KV_LATENT)), _full((KV_LATENT, LANES)), _full((1, LANES))],
        out_specs=[row_blk(WIDTH), row_blk(WIDTH), key_blk(WIDTH), col_blk(WIDTH),
                   row_blk(WIDTH), col_blk(WIDTH), row_blk(2 * LANES), row_blk(LANES),
                   key_blk(KV_LATENT), row_blk(LANES), col_blk(SUBLANES)],
        out_shape=[jax.ShapeDtypeStruct((bsz, seq, WIDTH), BF16),
                   jax.ShapeDtypeStruct((bsz, seq, WIDTH), BF16),
                   jax.ShapeDtypeStruct((bsz, n_blk, WIDTH, KBLK), BF16),
                   jax.ShapeDtypeStruct((bsz, WIDTH, seq), BF16),
                   jax.ShapeDtypeStruct((bsz, seq, WIDTH), BF16),
                   jax.ShapeDtypeStruct((bsz, WIDTH, seq), BF16),
                   jax.ShapeDtypeStruct((bsz, seq, 2 * LANES), BF16),
                   jax.ShapeDtypeStruct((bsz, seq, LANES), BF16),
                   jax.ShapeDtypeStruct((bsz, n_blk, KV_LATENT, KBLK), BF16),
                   jax.ShapeDtypeStruct((bsz, seq, LANES), BF16),
                   jax.ShapeDtypeStruct((bsz, SUBLANES, seq), F32)],
        compiler_params=params, name="proj",
    )(x, norm_in_gain[0][None], w_padded, rope, kv_norm_gain[0][None], wuk_dup, gik)

    q_row = lambda tq, c: pl.BlockSpec((1, tq, c), lambda b, i: (b, i, 0))
    q_col = lambda tq, c: pl.BlockSpec((1, c, tq), lambda b, i: (b, 0, i))
    per_b3 = lambda s1, s2: pl.BlockSpec((1, s1, s2), lambda b, i: (b, 0, 0))
    per_b4 = lambda s1, s2, s3: pl.BlockSpec((1, s1, s2, s3), lambda b, i: (b, 0, 0, 0))

    bias_np = _mix_a_bias_tables(LANES, KBLK)
    n_grp = tqa // LANES
    yat = pl.pallas_call(
        functools.partial(_mix_a_kernel, bias_np.shape[0]),
        grid=(bsz, seq // tqa),
        in_specs=[q_row(tqa, WIDTH), per_b3(seq, WIDTH), per_b4(n_blk, WIDTH, KBLK), q_col(tqa, WIDTH),
                  _full((WIDTH, tqa)), _full(bias_np.shape)],
        out_specs=q_col(tqa, WIDTH),
        out_shape=jax.ShapeDtypeStruct((bsz, WIDTH, seq), BF16),
        scratch_shapes=[pltpu.VMEM((n_grp, N_HEADS * LANES, LANES), BF16),
                        pltpu.VMEM((n_grp, N_PAIRS, KBLK, 2 * LANES), F32),
                        pltpu.VMEM((n_grp, N_PAIRS, KBLK, 2 * LANES), BF16),
                        pltpu.VMEM((WIDTH, tqa), F32)],
        compiler_params=params, name="mix_a",
    )(qa, ka, vat, zat, gain_a, jnp.asarray(bias_np))

    n_grp = tqb // LANES
    ybt = pl.pallas_call(
        functools.partial(_mix_b_kernel, topk),
        grid=(bsz, seq // tqb),
        in_specs=[q_row(tqb, WIDTH), q_row(tqb, 2 * LANES), q_col(tqb, SUBLANES), q_col(tqb, WIDTH),
                  per_b3(seq, LANES), per_b3(seq, LANES), per_b4(n_blk, KV_LATENT, KBLK),
                  _full((N_HEADS, HEAD_DIM, KV_LATENT)), _full((WIDTH, tqb)), _full((KBLK, KBLK))],
        out_specs=q_col(tqb, WIDTH),
        out_shape=jax.ShapeDtypeStruct((bsz, WIDTH, seq), BF16),
        scratch_shapes=[pltpu.VMEM((n_blk, KBLK, tqb), F32),
                        pltpu.VMEM((n_blk, KBLK, tqb), BF16),
                        pltpu.VMEM((n_grp, IDX_HEADS * LANES, LANES), BF16),
                        pltpu.VMEM((n_grp, N_HEADS * LANES, 2 * LANES), BF16),
                        pltpu.VMEM((KBLK, n_grp * IDX_HEADS * LANES), F32),
                        pltpu.VMEM((KBLK, tqb), F32),
                        pltpu.VMEM((KBLK, n_grp * N_HEADS * LANES), F32),
                        pltpu.VMEM((KBLK, n_grp * N_HEADS * LANES), BF16),
                        pltpu.VMEM((KV_LATENT, n_grp * N_HEADS * LANES), F32),
                        pltpu.VMEM((WIDTH, tqb), F32)],
        compiler_params=params, name="mix_b",
    )(qb, iq, iwt, zbt, ikd, kbd, ckvt, wuv_t, gain_b,
      jnp.asarray(np.tril(np.ones((KBLK, KBLK), np.float32)), BF16))

    return pl.pallas_call(
        _out_kernel,
        grid=(bsz, seq // tm),
        in_specs=[col_blk(WIDTH), col_blk(WIDTH), row_blk(d_model),
                  _full((2 * WIDTH, d_model)), _full((1, d_model))],
        out_specs=row_blk(d_model),
        out_shape=jax.ShapeDtypeStruct((bsz, seq, d_model), F32),
        compiler_params=params, name="out_proj",
    )(yat, ybt, x, w_out[0].astype(BF16), final_norm_gain[None])
```

```python
import functools
import math

import numpy as np
import jax
import jax.numpy as jnp
from jax import lax
from jax.experimental import pallas as pl
from jax.experimental.pallas import tpu as pltpu

F32 = jnp.float32
BF16 = jnp.bfloat16
I32 = jnp.int32

HEAD_DIM = 64
N_HEADS = 8
WIDTH = N_HEADS * HEAD_DIM
N_PAIRS = N_HEADS // 2
ROT_DIM = HEAD_DIM // 4
ROPE_THETA = 500000.0
KV_LATENT = 128
IDX_HEADS = 4
IDX_DIM = 64
TOPK_MAX = 256
EPS = 1e-6
PATTERNS = ((128, 1), (512, 4), (2048, 16))

LANES = 128
SUBLANES = 8
MXU_DEPTH = 256
KBLK = MXU_DEPTH
TQ_A = 2 * LANES
TQ_B = 4 * LANES
TM = 512
NEG_BIG = -1e30
VMEM_LIMIT = 56 * 1024 * 1024
Q_SCALE = HEAD_DIM ** -0.5 * math.log2(math.e)

U_QA, U_KA, U_VA, U_ZA, U_QB, U_CKV, U_ZB, U_IQ, U_LAST, N_UNITS = 0, 4, 8, 12, 16, 20, 21, 25, 27, 28
D_IN_PADDED = N_UNITS * LANES


def _dot(a, b):
    return jnp.dot(a, b, preferred_element_type=F32)


def _dot_nt(a, b):
    return lax.dot_general(a, b, (((1,), (1,)), ((), ())), preferred_element_type=F32)


def _split_head_pair(xp):
    lane = lax.broadcasted_iota(I32, xp.shape, 1)
    zero = jnp.zeros_like(xp)
    return jnp.where(lane < HEAD_DIM, xp, zero), jnp.where(lane >= HEAD_DIM, xp, zero)


def _proj_kernel(x_ref, gin_ref, w_ref, rope_ref, gkv_ref, wuk_ref, gik_ref,
                 qa_ref, ka_ref, vat_ref, zat_ref, qb_ref, zbt_ref, iq_ref, kbd_ref, ckvt_ref,
                 ikd_ref, iwt_ref):
    tm = x_ref.shape[1]
    nblk = tm // KBLK
    x = x_ref[0]
    ms = jnp.mean(x * x, axis=-1, keepdims=True)
    u = (x * lax.rsqrt(ms + EPS) * gin_ref[...]).astype(BF16)
    cosf, sina, sinb = (rope_ref[0, :, i * LANES:(i + 1) * LANES] for i in range(3))

    def rope(t):
        return t * cosf + pltpu.roll(t, LANES - ROT_DIM // 2, 1) * sina + pltpu.roll(t, ROT_DIM // 2, 1) * sinb

    def proj(unit):
        return _dot(u, w_ref[:, unit * LANES:(unit + 2) * LANES])

    def store_t(ref, unit, t):
        tt = t.T.astype(BF16)
        for i in range(nblk):
            ref[0, i, unit * LANES:(unit + 1) * LANES, :] = tt[:, i * KBLK:(i + 1) * KBLK]

    def lanes(unit):
        return slice(unit * LANES, (unit + 1) * LANES)

    def finish(unit, t):
        if unit < U_KA:
            qa_ref[0, :, lanes(unit - U_QA)] = (rope(t) * Q_SCALE).astype(BF16)
        elif unit < U_VA:
            ka_ref[0, :, lanes(unit - U_KA)] = rope(t).astype(BF16)
        elif unit < U_ZA:
            store_t(vat_ref, unit - U_VA, t)
        elif unit < U_QB:
            zat_ref[0, lanes(unit - U_ZA), :] = t.T.astype(BF16)
        elif unit < U_CKV:
            qb_ref[0, :, lanes(unit - U_QB)] = (rope(t) * Q_SCALE).astype(BF16)
        elif unit == U_CKV:
            ckv = t * lax.rsqrt(jnp.mean(t * t, axis=-1, keepdims=True) + EPS) * gkv_ref[...]
            store_t(ckvt_ref, 0, ckv)
            kbd_ref[0] = rope(_dot(ckv.astype(BF16), wuk_ref[...])).astype(BF16)
        elif unit < U_IQ:
            zbt_ref[0, lanes(unit - U_ZB), :] = t.T.astype(BF16)
        elif unit < U_LAST:
            iq_ref[0, :, lanes(unit - U_IQ)] = rope(t).astype(BF16)
        else:
            lane = lax.broadcasted_iota(I32, t.shape, 1)
            ik_ms = jnp.sum(jnp.where(lane < IDX_DIM, t * t, 0.0), axis=-1, keepdims=True) / IDX_DIM
            ik = rope(t * lax.rsqrt(ik_ms + EPS) * gik_ref[...])
            ikd_ref[0] = (ik + pltpu.roll(ik, IDX_DIM, 1)).astype(BF16)
            iwt_ref[0] = t.T[IDX_DIM:IDX_DIM + SUBLANES, :] * (IDX_HEADS ** -0.5 * IDX_DIM ** -0.5)

    for unit in range(0, N_UNITS, 2):
        acc = proj(unit)
        finish(unit, acc[:, :LANES])
        finish(unit + 1, acc[:, LANES:])


def _mix_a_bias_tables(tq, tk):
    assert tq == LANES
    o_last = PATTERNS[1][0] + tk - LANES
    offsets = list(range(0, o_last + 1, LANES)) + [o_last + LANES]
    qi = np.arange(tq)[None, :]
    kj = np.arange(tk)[:, None]
    tables = []
    for o in offsets:
        d = o + qi - kj
        mult = np.zeros(d.shape, np.int64)
        for window, dil in PATTERNS:
            mult += ((d >= 0) & (d % dil == 0) & (d <= window)).astype(np.int64)
        tables.append(np.where(mult > 0, np.log2(np.maximum(mult, 1)), NEG_BIG))
    return np.stack(tables).astype(np.float32)


def _mix_a_kernel(n_tables, qa_ref, ka_ref, vat_ref, zat_ref, gain_ref, bias_ref,
                  out_ref, qm_ref, s_ref, pt_ref, acc_ref):
    tq = qa_ref.shape[1]
    n_grp = tq // LANES
    j = pl.program_id(1)
    n_kb = (j * tq + tq - 1) // KBLK + 1
    chains = [(g, p) for g in range(n_grp) for p in range(N_PAIRS)]

    def group(g):
        return slice(g * LANES, (g + 1) * LANES)

    for g, p in chains:
        qm_ref[g, 2 * p * LANES:(2 * p + 1) * LANES, :], qm_ref[g, (2 * p + 1) * LANES:(2 * p + 2) * LANES, :] = (
            _split_head_pair(qa_ref[0, group(g), p * LANES:(p + 1) * LANES]))
    acc_ref[...] = jnp.zeros_like(acc_ref)
    pt_ref[...] = jnp.zeros_like(pt_ref)

    def scores(kb, g, p):
        k0 = pl.multiple_of(kb * KBLK, KBLK)
        return _dot_nt(ka_ref[0, pl.ds(k0, KBLK), p * LANES:(p + 1) * LANES],
                       qm_ref[g, 2 * p * LANES:(2 * p + 2) * LANES, :])

    def accumulate(kb, g, p, alpha):
        vt = vat_ref[0, kb, p * LANES:(p + 1) * LANES, :]
        r0 = p * LANES
        acc_ref[r0:r0 + HEAD_DIM, group(g)] = (acc_ref[r0:r0 + HEAD_DIM, group(g)] * alpha[:, :LANES]
                                               + _dot(vt[:HEAD_DIM], pt_ref[g, p, :, :LANES]))
        acc_ref[r0 + HEAD_DIM:r0 + LANES, group(g)] = (acc_ref[r0 + HEAD_DIM:r0 + LANES, group(g)] * alpha[:, LANES:]
                                                       + _dot(vt[HEAD_DIM:], pt_ref[g, p, :, LANES:]))

    for g, p in chains:
        s_ref[g, p] = scores(0, g, p)

    def body(kb, carry):
        ms, ls, alphas = carry
        prev = jnp.maximum(kb - 1, 0)
        nxt = jnp.minimum(kb + 1, n_kb - 1)
        new_ms, new_ls, new_alphas = [], [], []
        for c, (g, p) in enumerate(chains):
            bias = bias_ref[jnp.minimum((j * tq + g * LANES - kb * KBLK) // LANES, n_tables - 1)]
            accumulate(prev, g, p, alphas[c])
            s = s_ref[g, p] + jnp.concatenate([bias, bias], axis=1)
            m_new = jnp.maximum(ms[c], jnp.max(s, axis=0, keepdims=True))
            alpha = jnp.exp2(ms[c] - m_new)
            pt = jnp.exp2(s - m_new)
            new_ms.append(m_new)
            new_ls.append(alpha * ls[c] + jnp.sum(pt, axis=0, keepdims=True))
            new_alphas.append(alpha)
            pt_ref[g, p] = pt.astype(BF16)
            s_ref[g, p] = scores(nxt, g, p)
        return tuple(new_ms), tuple(new_ls), tuple(new_alphas)

    init = (tuple(jnp.full((1, 2 * LANES), NEG_BIG, F32) for _ in chains),
            tuple(jnp.zeros((1, 2 * LANES), F32) for _ in chains),
            tuple(jnp.ones((1, 2 * LANES), F32) for _ in chains))
    _, ls, alphas = lax.fori_loop(0, n_kb, body, init)
    for c, (g, p) in enumerate(chains):
        accumulate(n_kb - 1, g, p, alphas[c])
        r0 = p * LANES
        acc_ref[r0:r0 + HEAD_DIM, group(g)] = acc_ref[r0:r0 + HEAD_DIM, group(g)] / ls[c][:, :LANES]
        acc_ref[r0 + HEAD_DIM:r0 + LANES, group(g)] = acc_ref[r0 + HEAD_DIM:r0 + LANES, group(g)] / ls[c][:, LANES:]

    z = zat_ref[0].astype(F32)
    g = acc_ref[...] * (z / (1.0 + jnp.exp(-z)))
    ms = jnp.mean(g * g, axis=0, keepdims=True)
    out_ref[0] = (g * lax.rsqrt(ms + EPS) * gain_ref[...]).astype(BF16)


def _mix_b_kernel(topk, qb_ref, iq_ref, iwt_ref, zbt_ref, ikd_ref, kbd_ref, ckvt_ref, wuvt_ref,
                  gain_ref, tri_ref, out_ref, sc_ref, bias_ref, iqm_ref, rhs_ref, raw_ref, rank_ref,
                  s_ref, pt_ref, acc_ref, g_ref):
    tq = qb_ref.shape[1]
    n_grp = tq // LANES
    iw_cols = IDX_HEADS * LANES
    hd_cols = N_HEADS * LANES
    j = pl.program_id(1)
    n_kb = (j * tq + tq - 1) // KBLK + 1
    row = lax.broadcasted_iota(I32, (KBLK, tq), 0)
    t_pos = j * tq + lax.broadcasted_iota(I32, (KBLK, tq), 1)

    def key_rows(ref, kb):
        return ref[0, pl.ds(pl.multiple_of(kb * KBLK, KBLK), KBLK), :]

    def next_block(kb):
        return jnp.minimum(kb + 1, n_kb - 1)

    def group(g):
        return slice(g * LANES, (g + 1) * LANES)


    for g in range(n_grp):
        for p in range(IDX_HEADS // 2):
            iqm_ref[g, 2 * p * LANES:(2 * p + 1) * LANES, :], iqm_ref[g, (2 * p + 1) * LANES:(2 * p + 2) * LANES, :] = (
                _split_head_pair(iq_ref[0, group(g), p * LANES:(p + 1) * LANES]))

    def store_raw_scores(kb):
        keys = key_rows(ikd_ref, kb)
        for g in range(n_grp):
            raw_ref[:, g * iw_cols:(g + 1) * iw_cols] = _dot_nt(keys, iqm_ref[g])

    store_raw_scores(0)

    def score_body(kb, carry):
        for g in range(n_grp):
            c0 = g * iw_cols
            sc = jnp.maximum(raw_ref[:, c0:c0 + LANES], 0.0) * iwt_ref[0, 0:1, group(g)]
            for h in range(1, IDX_HEADS):
                sc = sc + (jnp.maximum(raw_ref[:, c0 + h * LANES:c0 + (h + 1) * LANES], 0.0)
                           * iwt_ref[0, h:h + 1, group(g)])
            causal = (kb * KBLK + lax.broadcasted_iota(I32, (KBLK, LANES), 0)
                      <= j * tq + g * LANES + lax.broadcasted_iota(I32, (KBLK, LANES), 1))
            sc_ref[kb, :, group(g)] = jnp.where(causal, sc, -jnp.inf)
        store_raw_scores(next_block(kb))
        return carry

    lax.fori_loop(0, n_kb, score_body, 0)

    def count(pred):
        def body(kb, acc):
            hit = pred(sc_ref[kb]).astype(I32)
            return acc + jnp.sum(hit.reshape(KBLK // SUBLANES, SUBLANES, tq), axis=0)
        acc = lax.fori_loop(0, n_kb, body, jnp.zeros((SUBLANES, tq), I32))
        return jnp.sum(acc, axis=0, keepdims=True)

    def key_to_float(key):
        bits = key ^ ((key >> 31) & jnp.int32(0x7FFFFFFF))
        return lax.bitcast_convert_type(bits, F32)

    def value_step(i, prefix):
        cand = prefix + (jnp.int32(1) << (31 - i))
        thr = key_to_float(cand)
        return jnp.where(count(lambda s: s >= thr) >= topk, cand, prefix)

    prefix = lax.fori_loop(0, 32, value_step, jnp.full((1, tq), -2 ** 31, I32))
    thr = key_to_float(prefix)
    need = (topk - count(lambda s: s > thr)).astype(F32)

    def tie_ranks(kb):
        tie = jnp.where(sc_ref[kb] == thr, 1.0, 0.0).astype(BF16)
        return _dot(tri_ref[...], tie)

    rank_ref[...] = tie_ranks(0)

    def bias_body(kb, ties_before):
        sc = sc_ref[kb]
        rank = rank_ref[...]
        take = (sc == thr) & (rank + ties_before <= need)
        sel = (sc > thr) | take | (t_pos < topk)
        bias_ref[kb] = jnp.where(sel & (kb * KBLK + row <= t_pos), 0.0, NEG_BIG).astype(BF16)
        rank_ref[...] = tie_ranks(next_block(kb))
        return ties_before + rank[KBLK - 1:KBLK, :]

    lax.fori_loop(0, n_kb, bias_body, jnp.zeros((1, tq), F32))

    eye = (lax.broadcasted_iota(I32, (LANES, LANES), 0)
           == lax.broadcasted_iota(I32, (LANES, LANES), 1)).astype(BF16)
    for g in range(n_grp):
        for p in range(N_PAIRS):
            rhs_ref[g, 2 * p * LANES:(2 * p + 1) * LANES, :LANES], rhs_ref[g, (2 * p + 1) * LANES:(2 * p + 2) * LANES, :LANES] = (
                _split_head_pair(qb_ref[0, group(g), p * LANES:(p + 1) * LANES]))
        for h in range(N_HEADS):
            rhs_ref[g, h * LANES:(h + 1) * LANES, LANES:] = eye
    acc_ref[...] = jnp.zeros_like(acc_ref)
    pt_ref[...] = jnp.zeros_like(pt_ref)

    def cols(g):
        return slice(g * hd_cols, (g + 1) * hd_cols)

    def store_masked_scores(kb, g):
        lhs = jnp.concatenate([key_rows(kbd_ref, kb), bias_ref[kb, :, group(g)]], axis=1)
        s_ref[:, cols(g)] = _dot_nt(lhs, rhs_ref[g])

    def accumulate(kb, g, alpha):
        acc_ref[:, cols(g)] = acc_ref[:, cols(g)] * alpha + _dot(ckvt_ref[0, kb], pt_ref[:, cols(g)])

    for g in range(n_grp):
        store_masked_scores(0, g)

    def attn_body(kb, carry):
        ms, ls, alphas = carry
        prev = jnp.maximum(kb - 1, 0)
        nxt = next_block(kb)
        new_ms, new_ls, new_alphas = [], [], []
        for g in range(n_grp):
            accumulate(prev, g, alphas[g])
            s = s_ref[:, cols(g)]
            m_new = jnp.maximum(ms[g], jnp.max(s, axis=0, keepdims=True))
            alpha = jnp.exp2(ms[g] - m_new)
            pt = jnp.exp2(s - m_new)
            new_ms.append(m_new)
            new_ls.append(alpha * ls[g] + jnp.sum(pt, axis=0, keepdims=True))
            new_alphas.append(alpha)
            pt_ref[:, cols(g)] = pt.astype(BF16)
            store_masked_scores(nxt, g)
        return tuple(new_ms), tuple(new_ls), tuple(new_alphas)

    init = (tuple(jnp.full((1, hd_cols), NEG_BIG, F32) for _ in range(n_grp)),
            tuple(jnp.zeros((1, hd_cols), F32) for _ in range(n_grp)),
            tuple(jnp.ones((1, hd_cols), F32) for _ in range(n_grp)))
    _, ls, alphas = lax.fori_loop(0, n_kb, attn_body, init)
    for g in range(n_grp):
        accumulate(n_kb - 1, g, alphas[g])
        o_lat = (acc_ref[:, cols(g)] / ls[g]).astype(BF16)
        for h in range(N_HEADS):
            g_ref[h * HEAD_DIM:(h + 1) * HEAD_DIM, group(g)] = _dot(wuvt_ref[h], o_lat[:, h * LANES:(h + 1) * LANES])

    z = zbt_ref[0].astype(F32)
    gated = g_ref[...] * (z / (1.0 + jnp.exp(-z)))
    ms = jnp.mean(gated * gated, axis=0, keepdims=True)
    out_ref[0] = (gated * lax.rsqrt(ms + EPS) * gain_ref[...]).astype(BF16)


def _out_kernel(yat_ref, ybt_ref, x_ref, w_ref, gain_ref, out_ref):
    y = jnp.concatenate([yat_ref[0].astype(F32).T, ybt_ref[0].astype(F32).T], axis=1).astype(BF16)
    h = x_ref[0] + _dot(y, w_ref[...])
    ms = jnp.mean(h * h, axis=-1, keepdims=True)
    out_ref[0] = h * lax.rsqrt(ms + EPS) * gain_ref[...]


def _rope_table(positions):
    half = ROT_DIM // 2
    inv_freq = ROPE_THETA ** (-jnp.arange(0, ROT_DIM, 2, dtype=F32) / ROT_DIM)
    ang = positions.astype(F32)[..., None] * inv_freq
    c = np.arange(LANES) % HEAD_DIM
    hit = (np.arange(half)[:, None] == (c % half)[None, :])
    spread = np.zeros((2 * half, 3 * LANES), np.float32)
    spread[:half, :LANES] = hit & (c < ROT_DIM)
    spread[half:, LANES:2 * LANES] = -1.0 * (hit & (c < half))
    spread[half:, 2 * LANES:] = hit & (c >= half) & (c < ROT_DIM)
    ones = np.concatenate([(c >= ROT_DIM), np.zeros(2 * LANES, bool)]).astype(np.float32)
    trig = jnp.concatenate([jnp.cos(ang), jnp.sin(ang)], axis=-1)
    return jnp.einsum("bsf,fl->bsl", trig, jnp.asarray(spread), precision=lax.Precision.HIGHEST) + ones


def _full(shape):
    return pl.BlockSpec(shape, lambda *_: (0,) * len(shape))


@jax.jit
def kernel(x, positions, norm_in_gain, w_in, kv_norm_gain, w_uk, w_uv, idx_k_norm_gain,
           branch_norm_gain_a, branch_norm_gain_b, w_out, final_norm_gain):
    bsz, seq, d_model = x.shape
    assert w_in.shape[0] == 1
    assert seq % KBLK == 0 and seq <= PATTERNS[-1][0]
    n_blk = seq // KBLK
    topk = min(TOPK_MAX, seq // 4)
    tm = min(TM, seq)
    tqa, tqb = TQ_A, TQ_B
    params = pltpu.CompilerParams(dimension_semantics=("arbitrary", "arbitrary"),
                                  vmem_limit_bytes=VMEM_LIMIT)

    rope = _rope_table(positions)
    w_padded = jnp.pad(w_in[0].astype(BF16), ((0, 0), (0, D_IN_PADDED - w_in.shape[2])))
    wuk_dup = jnp.concatenate([w_uk[0], w_uk[0]], axis=1).astype(BF16)
    gik = jnp.concatenate([idx_k_norm_gain[0], jnp.zeros((LANES - IDX_DIM,), F32)])[None]
    wuv_t = jnp.swapaxes(w_uv[0], 1, 2).astype(BF16)
    gain_a = jnp.broadcast_to(branch_norm_gain_a[0][:, None], (WIDTH, tqa))
    gain_b = jnp.broadcast_to(branch_norm_gain_b[0][:, None], (WIDTH, tqb))

    row_blk = lambda c: pl.BlockSpec((1, tm, c), lambda b, i: (b, i, 0))
    col_blk = lambda c: pl.BlockSpec((1, c, tm), lambda b, i: (b, 0, i))
    key_blk = lambda c: pl.BlockSpec((1, tm // KBLK, c, KBLK), lambda b, i: (b, i, 0, 0))
    qa, ka, vat, zat, qb, zbt, iq, kbd, ckvt, ikd, iwt = pl.pallas_call(
        _proj_kernel,
        grid=(bsz, seq // tm),
        in_specs=[row_blk(d_model), _full((1, d_model)), _full((d_model, D_IN_PADDED)), row_blk(3 * LANES),
                  _full((1, KV_LATENT)), _full((KV_LATENT, LANES)), _full((1, LANES))],
        out_specs=[row_blk(WIDTH), row_blk(WIDTH), key_blk(WIDTH), col_blk(WIDTH),
                   row_blk(WIDTH), col_blk(WIDTH), row_blk(2 * LANES), row_blk(LANES),
                   key_blk(KV_LATENT), row_blk(LANES), col_blk(SUBLANES)],
        out_shape=[jax.ShapeDtypeStruct((bsz, seq, WIDTH), BF16),
                   jax.ShapeDtypeStruct((bsz, seq, WIDTH), BF16),
                   jax.ShapeDtypeStruct((bsz, n_blk, WIDTH, KBLK), BF16),
                   jax.ShapeDtypeStruct((bsz, WIDTH, seq), BF16),
                   jax.ShapeDtypeStruct((bsz, seq, WIDTH), BF16),
                   jax.ShapeDtypeStruct((bsz, WIDTH, seq), BF16),
                   jax.ShapeDtypeStruct((bsz, seq, 2 * LANES), BF16),
                   jax.ShapeDtypeStruct((bsz, seq, LANES), BF16),
                   jax.ShapeDtypeStruct((bsz, n_blk, KV_LATENT, KBLK), BF16),
                   jax.ShapeDtypeStruct((bsz, seq, LANES), BF16),
                   jax.ShapeDtypeStruct((bsz, SUBLANES, seq), F32)],
        compiler_params=params, name="proj",
    )(x, norm_in_gain[0][None], w_padded, rope, kv_norm_gain[0][None], wuk_dup, gik)

    q_row = lambda tq, c: pl.BlockSpec((1, tq, c), lambda b, i: (b, i, 0))
    q_col = lambda tq, c: pl.BlockSpec((1, c, tq), lambda b, i: (b, 0, i))
    per_b3 = lambda s1, s2: pl.BlockSpec((1, s1, s2), lambda b, i: (b, 0, 0))
    per_b4 = lambda s1, s2, s3: pl.BlockSpec((1, s1, s2, s3), lambda b, i: (b, 0, 0, 0))

    bias_np = _mix_a_bias_tables(LANES, KBLK)
    n_grp = tqa // LANES
    yat = pl.pallas_call(
        functools.partial(_mix_a_kernel, bias_np.shape[0]),
        grid=(bsz, seq // tqa),
        in_specs=[q_row(tqa, WIDTH), per_b3(seq, WIDTH), per_b4(n_blk, WIDTH, KBLK), q_col(tqa, WIDTH),
                  _full((WIDTH, tqa)), _full(bias_np.shape)],
        out_specs=q_col(tqa, WIDTH),
        out_shape=jax.ShapeDtypeStruct((bsz, WIDTH, seq), BF16),
        scratch_shapes=[pltpu.VMEM((n_grp, N_HEADS * LANES, LANES), BF16),
                        pltpu.VMEM((n_grp, N_PAIRS, KBLK, 2 * LANES), F32),
                        pltpu.VMEM((n_grp, N_PAIRS, KBLK, 2 * LANES), BF16),
                        pltpu.VMEM((WIDTH, tqa), F32)],
        compiler_params=params, name="mix_a",
    )(qa, ka, vat, zat, gain_a, jnp.asarray(bias_np))

    n_grp = tqb // LANES
    ybt = pl.pallas_call(
        functools.partial(_mix_b_kernel, topk),
        grid=(bsz, seq // tqb),
        in_specs=[q_row(tqb, WIDTH), q_row(tqb, 2 * LANES), q_col(tqb, SUBLANES), q_col(tqb, WIDTH),
                  per_b3(seq, LANES), per_b3(seq, LANES), per_b4(n_blk, KV_LATENT, KBLK),
                  _full((N_HEADS, HEAD_DIM, KV_LATENT)), _full((WIDTH, tqb)), _full((KBLK, KBLK))],
        out_specs=q_col(tqb, WIDTH),
        out_shape=jax.ShapeDtypeStruct((bsz, WIDTH, seq), BF16),
        scratch_shapes=[pltpu.VMEM((n_blk, KBLK, tqb), F32),
                        pltpu.VMEM((n_blk, KBLK, tqb), BF16),
                        pltpu.VMEM((n_grp, IDX_HEADS * LANES, LANES), BF16),
                        pltpu.VMEM((n_grp, N_HEADS * LANES, 2 * LANES), BF16),
                        pltpu.VMEM((KBLK, n_grp * IDX_HEADS * LANES), F32),
                        pltpu.VMEM((KBLK, tqb), F32),
                        pltpu.VMEM((KBLK, n_grp * N_HEADS * LANES), F32),
                        pltpu.VMEM((KBLK, n_grp * N_HEADS * LANES), BF16),
                        pltpu.VMEM((KV_LATENT, n_grp * N_HEADS * LANES), F32),
                        pltpu.VMEM((WIDTH, tqb), F32)],
        compiler_params=params, name="mix_b",
    )(qb, iq, iwt, zbt, ikd, kbd, ckvt, wuv_t, gain_b,
      jnp.asarray(np.tril(np.ones((KBLK, KBLK), np.float32)), BF16))

    return pl.pallas_call(
        _out_kernel,
        grid=(bsz, seq // tm),
        in_specs=[col_blk(WIDTH), col_blk(WIDTH), row_blk(d_model),
                  _full((2 * WIDTH, d_model)), _full((1, d_model))],
        out_specs=row_blk(d_model),
        out_shape=jax.ShapeDtypeStruct((bsz, seq, d_model), F32),
        compiler_params=params, name="out_proj",
    )(yat, ybt, x, w_out[0].astype(BF16), final_norm_gain[None])
```

```python
import functools
import math

import numpy as np
import jax
import jax.numpy as jnp
from jax import lax
from jax.experimental import pallas as pl
from jax.experimental.pallas import tpu as pltpu

F32 = jnp.float32
BF16 = jnp.bfloat16
I32 = jnp.int32

HEAD_DIM = 64
N_HEADS = 8
WIDTH = N_HEADS * HEAD_DIM
N_PAIRS = N_HEADS // 2
ROT_DIM = HEAD_DIM // 4
ROPE_THETA = 500000.0
KV_LATENT = 128
IDX_HEADS = 4
IDX_DIM = 64
TOPK_MAX = 256
EPS = 1e-6
PATTERNS = ((128, 1), (512, 4), (2048, 16))

LANES = 128
SUBLANES = 8
MXU_DEPTH = 256
KBLK = MXU_DEPTH
TQ_A = 2 * LANES
TQ_B = 4 * LANES
TM = 512
NEG_BIG = -1e30
VMEM_LIMIT = 56 * 1024 * 1024
Q_SCALE = HEAD_DIM ** -0.5 * math.log2(math.e)

U_QA, U_KA, U_VA, U_ZA, U_QB, U_CKV, U_ZB, U_IQ, U_LAST, N_UNITS = 0, 4, 8, 12, 16, 20, 21, 25, 27, 28
D_IN_PADDED = N_UNITS * LANES


def _dot(a, b):
    return jnp.dot(a, b, preferred_element_type=F32)


def _dot_nt(a, b):
    return lax.dot_general(a, b, (((1,), (1,)), ((), ())), preferred_element_type=F32)


def _split_head_pair(xp):
    lane = lax.broadcasted_iota(I32, xp.shape, 1)
    zero = jnp.zeros_like(xp)
    return jnp.where(lane < HEAD_DIM, xp, zero), jnp.where(lane >= HEAD_DIM, xp, zero)


def _proj_kernel(x_ref, gin_ref, w_ref, trig_ref, spread_ref, ones_ref, gkv_ref, wuk_ref, gik_ref,
                 qa_ref, ka_ref, vat_ref, zat_ref, qb_ref, zbt_ref, iq_ref, kbd_ref, ckvt_ref,
                 ikd_ref, iwt_ref):
    tm = x_ref.shape[1]
    nblk = tm // KBLK
    x = x_ref[0]
    ms = jnp.mean(x * x, axis=-1, keepdims=True)
    u = (x * lax.rsqrt(ms + EPS) * gin_ref[...]).astype(BF16)
    table = _dot(trig_ref[0], spread_ref[...]) + ones_ref[...]
    cosf, sina, sinb = (table[:, i * LANES:(i + 1) * LANES] for i in range(3))

    def rope(t):
        return t * cosf + pltpu.roll(t, LANES - ROT_DIM // 2, 1) * sina + pltpu.roll(t, ROT_DIM // 2, 1) * sinb

    def proj(unit):
        return _dot(u, w_ref[:, unit * LANES:(unit + 2) * LANES])

    def store_t(ref, unit, t):
        tt = t.T.astype(BF16)
        for i in range(nblk):
            ref[0, i, unit * LANES:(unit + 1) * LANES, :] = tt[:, i * KBLK:(i + 1) * KBLK]

    def lanes(unit):
        return slice(unit * LANES, (unit + 1) * LANES)

    def finish(unit, t):
        if unit < U_KA:
            qa_ref[0, :, lanes(unit - U_QA)] = (rope(t) * Q_SCALE).astype(BF16)
        elif unit < U_VA:
            ka_ref[0, :, lanes(unit - U_KA)] = rope(t).astype(BF16)
        elif unit < U_ZA:
            store_t(vat_ref, unit - U_VA, t)
        elif unit < U_QB:
            zat_ref[0, lanes(unit - U_ZA), :] = t.T.astype(BF16)
        elif unit < U_CKV:
            qb_ref[0, :, lanes(unit - U_QB)] = (rope(t) * Q_SCALE).astype(BF16)
        elif unit == U_CKV:
            ckv = t * lax.rsqrt(jnp.mean(t * t, axis=-1, keepdims=True) + EPS) * gkv_ref[...]
            store_t(ckvt_ref, 0, ckv)
            kbd_ref[0] = rope(_dot(ckv.astype(BF16), wuk_ref[...])).astype(BF16)
        elif unit < U_IQ:
            zbt_ref[0, lanes(unit - U_ZB), :] = t.T.astype(BF16)
        elif unit < U_LAST:
            iq_ref[0, :, lanes(unit - U_IQ)] = rope(t).astype(BF16)
        else:
            lane = lax.broadcasted_iota(I32, t.shape, 1)
            ik_ms = jnp.sum(jnp.where(lane < IDX_DIM, t * t, 0.0), axis=-1, keepdims=True) / IDX_DIM
            ik = rope(t * lax.rsqrt(ik_ms + EPS) * gik_ref[...])
            ikd_ref[0] = (ik + pltpu.roll(ik, IDX_DIM, 1)).astype(BF16)
            iwt_ref[0] = t.T[IDX_DIM:IDX_DIM + SUBLANES, :] * (IDX_HEADS ** -0.5 * IDX_DIM ** -0.5)

    first = [U_LAST - 1, U_CKV]
    for unit in first + [u for u in range(0, N_UNITS, 2) if u not in first]:
        acc = proj(unit)
        finish(unit, acc[:, :LANES])
        finish(unit + 1, acc[:, LANES:])


def _mix_a_bias_tables(tq, tk):
    assert tq == LANES
    o_last = PATTERNS[1][0] + tk - LANES
    offsets = list(range(0, o_last + 1, LANES)) + [o_last + LANES]
    qi = np.arange(tq)[None, :]
    kj = np.arange(tk)[:, None]
    tables = []
    for o in offsets:
        d = o + qi - kj
        mult = np.zeros(d.shape, np.int64)
        for window, dil in PATTERNS:
            mult += ((d >= 0) & (d % dil == 0) & (d <= window)).astype(np.int64)
        tables.append(np.where(mult > 0, np.log2(np.maximum(mult, 1)), NEG_BIG))
    return np.stack(tables).astype(np.float32)


def _mix_a_kernel(n_tables, qa_ref, ka_ref, vat_ref, zat_ref, gain_ref, bias_ref,
                  out_ref, qm_ref, s_ref, pt_ref, acc_ref):
    tq = qa_ref.shape[1]
    n_grp = tq // LANES
    j = pl.program_id(1)
    n_kb = (j * tq + tq - 1) // KBLK + 1
    chains = [(g, p) for g in range(n_grp) for p in range(N_PAIRS)]

    def group(g):
        return slice(g * LANES, (g + 1) * LANES)

    for g, p in chains:
        qm_ref[g, 2 * p * LANES:(2 * p + 1) * LANES, :], qm_ref[g, (2 * p + 1) * LANES:(2 * p + 2) * LANES, :] = (
            _split_head_pair(qa_ref[0, group(g), p * LANES:(p + 1) * LANES]))
    acc_ref[...] = jnp.zeros_like(acc_ref)
    pt_ref[...] = jnp.zeros_like(pt_ref)

    def scores(kb, g, p):
        k0 = pl.multiple_of(kb * KBLK, KBLK)
        return _dot_nt(ka_ref[0, pl.ds(k0, KBLK), p * LANES:(p + 1) * LANES],
                       qm_ref[g, 2 * p * LANES:(2 * p + 2) * LANES, :])

    def accumulate(kb, g, p, alpha):
        vt = vat_ref[0, kb, p * LANES:(p + 1) * LANES, :]
        r0 = p * LANES
        acc_ref[r0:r0 + HEAD_DIM, group(g)] = (acc_ref[r0:r0 + HEAD_DIM, group(g)] * alpha[:, :LANES]
                                               + _dot(vt[:HEAD_DIM], pt_ref[g, p, :, :LANES]))
        acc_ref[r0 + HEAD_DIM:r0 + LANES, group(g)] = (acc_ref[r0 + HEAD_DIM:r0 + LANES, group(g)] * alpha[:, LANES:]
                                                       + _dot(vt[HEAD_DIM:], pt_ref[g, p, :, LANES:]))

    for g, p in chains:
        s_ref[g, p] = scores(0, g, p)

    def body(kb, carry):
        ms, ls, alphas = carry
        prev = jnp.maximum(kb - 1, 0)
        nxt = jnp.minimum(kb + 1, n_kb - 1)
        new_ms, new_ls, new_alphas = [], [], []
        for c, (g, p) in enumerate(chains):
            bias = bias_ref[jnp.minimum((j * tq + g * LANES - kb * KBLK) // LANES, n_tables - 1)]
            accumulate(prev, g, p, alphas[c])
            s = s_ref[g, p] + jnp.concatenate([bias, bias], axis=1)
            m_new = jnp.maximum(ms[c], jnp.max(s, axis=0, keepdims=True))
            alpha = jnp.exp2(ms[c] - m_new)
            pt = jnp.exp2(s - m_new)
            new_ms.append(m_new)
            new_ls.append(alpha * ls[c] + jnp.sum(pt, axis=0, keepdims=True))
            new_alphas.append(alpha)
            pt_ref[g, p] = pt.astype(BF16)
            s_ref[g, p] = scores(nxt, g, p)
        return tuple(new_ms), tuple(new_ls), tuple(new_alphas)

    init = (tuple(jnp.full((1, 2 * LANES), NEG_BIG, F32) for _ in chains),
            tuple(jnp.zeros((1, 2 * LANES), F32) for _ in chains),
            tuple(jnp.ones((1, 2 * LANES), F32) for _ in chains))
    _, ls, alphas = lax.fori_loop(0, n_kb, body, init)
    for c, (g, p) in enumerate(chains):
        accumulate(n_kb - 1, g, p, alphas[c])
        r0 = p * LANES
        acc_ref[r0:r0 + HEAD_DIM, group(g)] = acc_ref[r0:r0 + HEAD_DIM, group(g)] / ls[c][:, :LANES]
        acc_ref[r0 + HEAD_DIM:r0 + LANES, group(g)] = acc_ref[r0 + HEAD_DIM:r0 + LANES, group(g)] / ls[c][:, LANES:]

    z = zat_ref[0].astype(F32)
    g = acc_ref[...] * (z / (1.0 + jnp.exp(-z)))
    ms = jnp.mean(g * g, axis=0, keepdims=True)
    out_ref[0] = (g * lax.rsqrt(ms + EPS) * gain_ref[...]).astype(BF16)


def _mix_b_kernel(topk, qb_ref, iq_ref, iwt_ref, zbt_ref, ikd_ref, kbd_ref, ckvt_ref, wuvt_ref,
                  gain_ref, tri_ref, out_ref, sc_ref, bias_ref, iqm_ref, rhs_ref, raw_ref, rank_ref,
                  s_ref, pt_ref, acc_ref, g_ref):
    tq = qb_ref.shape[1]
    n_grp = tq // LANES
    iw_cols = IDX_HEADS * LANES
    hd_cols = N_HEADS * LANES
    j = pl.program_id(1)
    n_kb = (j * tq + tq - 1) // KBLK + 1
    row = lax.broadcasted_iota(I32, (KBLK, tq), 0)
    t_pos = j * tq + lax.broadcasted_iota(I32, (KBLK, tq), 1)

    def key_rows(ref, kb):
        return ref[0, pl.ds(pl.multiple_of(kb * KBLK, KBLK), KBLK), :]

    def next_block(kb):
        return jnp.minimum(kb + 1, n_kb - 1)

    def group(g):
        return slice(g * LANES, (g + 1) * LANES)


    for g in range(n_grp):
        for p in range(IDX_HEADS // 2):
            iqm_ref[g, 2 * p * LANES:(2 * p + 1) * LANES, :], iqm_ref[g, (2 * p + 1) * LANES:(2 * p + 2) * LANES, :] = (
                _split_head_pair(iq_ref[0, group(g), p * LANES:(p + 1) * LANES]))

    def store_raw_scores(kb):
        keys = key_rows(ikd_ref, kb)
        for g in range(n_grp):
            raw_ref[:, g * iw_cols:(g + 1) * iw_cols] = _dot_nt(keys, iqm_ref[g])

    store_raw_scores(0)

    def score_body(kb, carry):
        for g in range(n_grp):
            c0 = g * iw_cols
            sc = jnp.maximum(raw_ref[:, c0:c0 + LANES], 0.0) * iwt_ref[0, 0:1, group(g)]
            for h in range(1, IDX_HEADS):
                sc = sc + (jnp.maximum(raw_ref[:, c0 + h * LANES:c0 + (h + 1) * LANES], 0.0)
                           * iwt_ref[0, h:h + 1, group(g)])
            causal = (kb * KBLK + lax.broadcasted_iota(I32, (KBLK, LANES), 0)
                      <= j * tq + g * LANES + lax.broadcasted_iota(I32, (KBLK, LANES), 1))
            sc_ref[kb, :, group(g)] = jnp.where(causal, sc, -jnp.inf)
        store_raw_scores(next_block(kb))
        return carry

    lax.fori_loop(0, n_kb, score_body, 0)

    def count(pred):
        def body(kb, acc):
            hit = pred(sc_ref[kb]).astype(I32)
            return acc + jnp.sum(hit.reshape(KBLK // SUBLANES, SUBLANES, tq), axis=0)
        acc = lax.fori_loop(0, n_kb, body, jnp.zeros((SUBLANES, tq), I32))
        return jnp.sum(acc, axis=0, keepdims=True)

    def key_to_float(key):
        bits = key ^ ((key >> 31) & jnp.int32(0x7FFFFFFF))
        return lax.bitcast_convert_type(bits, F32)

    def value_step(i, prefix):
        cand = prefix + (jnp.int32(1) << (31 - i))
        thr = key_to_float(cand)
        return jnp.where(count(lambda s: s >= thr) >= topk, cand, prefix)

    prefix = lax.fori_loop(0, 32, value_step, jnp.full((1, tq), -2 ** 31, I32))
    thr = key_to_float(prefix)
    need = (topk - count(lambda s: s > thr)).astype(F32)

    def tie_ranks(kb):
        tie = jnp.where(sc_ref[kb] == thr, 1.0, 0.0).astype(BF16)
        return _dot(tri_ref[...], tie)

    rank_ref[...] = tie_ranks(0)

    def bias_body(kb, ties_before):
        sc = sc_ref[kb]
        rank = rank_ref[...]
        take = (sc == thr) & (rank + ties_before <= need)
        sel = (sc > thr) | take | (t_pos < topk)
        bias_ref[kb] = jnp.where(sel & (kb * KBLK + row <= t_pos), 0.0, NEG_BIG).astype(BF16)
        rank_ref[...] = tie_ranks(next_block(kb))
        return ties_before + rank[KBLK - 1:KBLK, :]

    lax.fori_loop(0, n_kb, bias_body, jnp.zeros((1, tq), F32))

    eye = (lax.broadcasted_iota(I32, (LANES, LANES), 0)
           == lax.broadcasted_iota(I32, (LANES, LANES), 1)).astype(BF16)
    for g in range(n_grp):
        for p in range(N_PAIRS):
            rhs_ref[g, 2 * p * LANES:(2 * p + 1) * LANES, :LANES], rhs_ref[g, (2 * p + 1) * LANES:(2 * p + 2) * LANES, :LANES] = (
                _split_head_pair(qb_ref[0, group(g), p * LANES:(p + 1) * LANES]))
        for h in range(N_HEADS):
            rhs_ref[g, h * LANES:(h + 1) * LANES, LANES:] = eye
    acc_ref[...] = jnp.zeros_like(acc_ref)
    pt_ref[...] = jnp.zeros_like(pt_ref)

    def cols(g):
        return slice(g * hd_cols, (g + 1) * hd_cols)

    def store_masked_scores(kb, g):
        lhs = jnp.concatenate([key_rows(kbd_ref, kb), bias_ref[kb, :, group(g)]], axis=1)
        s_ref[:, cols(g)] = _dot_nt(lhs, rhs_ref[g])

    def accumulate(kb, g, alpha):
        acc_ref[:, cols(g)] = acc_ref[:, cols(g)] * alpha + _dot(ckvt_ref[0, kb], pt_ref[:, cols(g)])

    for g in range(n_grp):
        store_masked_scores(0, g)

    def attn_body(kb, carry):
        ms, ls, alphas = carry
        prev = jnp.maximum(kb - 1, 0)
        nxt = next_block(kb)
        new_ms, new_ls, new_alphas = [], [], []
        for g in range(n_grp):
            accumulate(prev, g, alphas[g])
            s = s_ref[:, cols(g)]
            m_new = jnp.maximum(ms[g], jnp.max(s, axis=0, keepdims=True))
            alpha = jnp.exp2(ms[g] - m_new)
            pt = jnp.exp2(s - m_new)
            new_ms.append(m_new)
            new_ls.append(alpha * ls[g] + jnp.sum(pt, axis=0, keepdims=True))
            new_alphas.append(alpha)
            pt_ref[:, cols(g)] = pt.astype(BF16)
            store_masked_scores(nxt, g)
        return tuple(new_ms), tuple(new_ls), tuple(new_alphas)

    init = (tuple(jnp.full((1, hd_cols), NEG_BIG, F32) for _ in range(n_grp)),
            tuple(jnp.zeros((1, hd_cols), F32) for _ in range(n_grp)),
            tuple(jnp.ones((1, hd_cols), F32) for _ in range(n_grp)))
    _, ls, alphas = lax.fori_loop(0, n_kb, attn_body, init)
    for g in range(n_grp):
        accumulate(n_kb - 1, g, alphas[g])
        o_lat = (acc_ref[:, cols(g)] / ls[g]).astype(BF16)
        for h in range(N_HEADS):
            g_ref[h * HEAD_DIM:(h + 1) * HEAD_DIM, group(g)] = _dot(wuvt_ref[h], o_lat[:, h * LANES:(h + 1) * LANES])

    z = zbt_ref[0].astype(F32)
    gated = g_ref[...] * (z / (1.0 + jnp.exp(-z)))
    ms = jnp.mean(gated * gated, axis=0, keepdims=True)
    out_ref[0] = (gated * lax.rsqrt(ms + EPS) * gain_ref[...]).astype(BF16)


def _out_kernel(yat_ref, ybt_ref, x_ref, w_ref, gain_ref, out_ref):
    y = jnp.concatenate([yat_ref[0].astype(F32).T, ybt_ref[0].astype(F32).T], axis=1).astype(BF16)
    h = x_ref[0] + _dot(y, w_ref[...])
    ms = jnp.mean(h * h, axis=-1, keepdims=True)
    out_ref[0] = h * lax.rsqrt(ms + EPS) * gain_ref[...]


N_SPLIT = 3


def _rope_trig(positions):
    inv_freq = ROPE_THETA ** (-jnp.arange(0, ROT_DIM, 2, dtype=F32) / ROT_DIM)
    ang = positions.astype(F32)[..., None] * inv_freq
    rest = jnp.concatenate([jnp.cos(ang), jnp.sin(ang)], axis=-1)
    pieces = []
    for _ in range(N_SPLIT):
        pieces.append(rest.astype(BF16))
        rest = rest - pieces[-1].astype(F32)
    return jnp.concatenate(pieces, axis=-1)


def _rope_spread():
    half = ROT_DIM // 2
    c = np.arange(LANES) % HEAD_DIM
    hit = (np.arange(half)[:, None] == (c % half)[None, :])
    spread = np.zeros((2 * half, 3 * LANES), np.float32)
    spread[:half, :LANES] = hit & (c < ROT_DIM)
    spread[half:, LANES:2 * LANES] = -1.0 * (hit & (c < half))
    spread[half:, 2 * LANES:] = hit & (c >= half) & (c < ROT_DIM)
    ones = np.concatenate([(c >= ROT_DIM), np.zeros(2 * LANES, bool)]).astype(np.float32)[None]
    return np.tile(spread, (N_SPLIT, 1)), ones


def _full(shape):
    return pl.BlockSpec(shape, lambda *_: (0,) * len(shape))


@jax.jit
def kernel(x, positions, norm_in_gain, w_in, kv_norm_gain, w_uk, w_uv, idx_k_norm_gain,
           branch_norm_gain_a, branch_norm_gain_b, w_out, final_norm_gain):
    bsz, seq, d_model = x.shape
    assert w_in.shape[0] == 1
    assert seq % KBLK == 0 and seq <= PATTERNS[-1][0]
    n_blk = seq // KBLK
    topk = min(TOPK_MAX, seq // 4)
    tm = min(TM, seq)
    tqa, tqb = TQ_A, TQ_B
    params = pltpu.CompilerParams(dimension_semantics=("arbitrary", "arbitrary"),
                                  vmem_limit_bytes=VMEM_LIMIT)

    trig = _rope_trig(positions)
    spread, rope_ones = _rope_spread()
    w_padded = jnp.pad(w_in[0], ((0, 0), (0, D_IN_PADDED - w_in.shape[2]))).astype(BF16)
    wuk_dup = jnp.concatenate([w_uk[0], w_uk[0]], axis=1).astype(BF16)
    gik = jnp.concatenate([idx_k_norm_gain[0], jnp.zeros((LANES - IDX_DIM,), F32)])[None]
    wuv_t = jnp.swapaxes(w_uv[0], 1, 2).astype(BF16)
    gain_a = jnp.broadcast_to(branch_norm_gain_a[0][:, None], (WIDTH, tqa))
    gain_b = jnp.broadcast_to(branch_norm_gain_b[0][:, None], (WIDTH, tqb))

    row_blk = lambda c: pl.BlockSpec((1, tm, c), lambda b, i: (b, i, 0))
    col_blk = lambda c: pl.BlockSpec((1, c, tm), lambda b, i: (b, 0, i))
    key_blk = lambda c: pl.BlockSpec((1, tm // KBLK, c, KBLK), lambda b, i: (b, i, 0, 0))
    qa, ka, vat, zat, qb, zbt, iq, kbd, ckvt, ikd, iwt = pl.pallas_call(
        _proj_kernel,
        grid=(bsz, seq // tm),
        in_specs=[row_blk(d_model), _full((1, d_model)), _full((d_model, D_IN_PADDED)),
                  row_blk(N_SPLIT * ROT_DIM), _full(spread.shape), _full(rope_ones.shape),
                  _full((1, KV_LATENT)), _full((KV_LATENT, LANES)), _full((1, LANES))],
        out_specs=[row_blk(WIDTH), row_blk(WIDTH), key_blk(WIDTH), col_blk(WIDTH),
                   row_blk(WIDTH), col_blk(WIDTH), row_blk(2 * LANES), row_blk(LANES),
                   key_blk(KV_LATENT), row_blk(LANES), col_blk(SUBLANES)],
        out_shape=[jax.ShapeDtypeStruct((bsz, seq, WIDTH), BF16),
                   jax.ShapeDtypeStruct((bsz, seq, WIDTH), BF16),
                   jax.ShapeDtypeStruct((bsz, n_blk, WIDTH, KBLK), BF16),
                   jax.ShapeDtypeStruct((bsz, WIDTH, seq), BF16),
                   jax.ShapeDtypeStruct((bsz, seq, WIDTH), BF16),
                   jax.ShapeDtypeStruct((bsz, WIDTH, seq), BF16),
                   jax.ShapeDtypeStruct((bsz, seq, 2 * LANES), BF16),
                   jax.ShapeDtypeStruct((bsz, seq, LANES), BF16),
                   jax.ShapeDtypeStruct((bsz, n_blk, KV_LATENT, KBLK), BF16),
                   jax.ShapeDtypeStruct((bsz, seq, LANES), BF16),
                   jax.ShapeDtypeStruct((bsz, SUBLANES, seq), F32)],
        compiler_params=params, name="proj",
    )(x, norm_in_gain[0][None], w_padded, trig, jnp.asarray(spread, BF16), jnp.asarray(rope_ones),
      kv_norm_gain[0][None], wuk_dup, gik)

    q_row = lambda tq, c: pl.BlockSpec((1, tq, c), lambda b, i: (b, i, 0))
    q_col = lambda tq, c: pl.BlockSpec((1, c, tq), lambda b, i: (b, 0, i))
    per_b3 = lambda s1, s2: pl.BlockSpec((1, s1, s2), lambda b, i: (b, 0, 0))
    per_b4 = lambda s1, s2, s3: pl.BlockSpec((1, s1, s2, s3), lambda b, i: (b, 0, 0, 0))

    bias_np = _mix_a_bias_tables(LANES, KBLK)
    n_grp = tqa // LANES
    yat = pl.pallas_call(
        functools.partial(_mix_a_kernel, bias_np.shape[0]),
        grid=(bsz, seq // tqa),
        in_specs=[q_row(tqa, WIDTH), per_b3(seq, WIDTH), per_b4(n_blk, WIDTH, KBLK), q_col(tqa, WIDTH),
                  _full((WIDTH, tqa)), _full(bias_np.shape)],
        out_specs=q_col(tqa, WIDTH),
        out_shape=jax.ShapeDtypeStruct((bsz, WIDTH, seq), BF16),
        scratch_shapes=[pltpu.VMEM((n_grp, N_HEADS * LANES, LANES), BF16),
                        pltpu.VMEM((n_grp, N_PAIRS, KBLK, 2 * LANES), F32),
                        pltpu.VMEM((n_grp, N_PAIRS, KBLK, 2 * LANES), BF16),
                        pltpu.VMEM((WIDTH, tqa), F32)],
        compiler_params=params, name="mix_a",
    )(qa, ka, vat, zat, gain_a, jnp.asarray(bias_np))

    n_grp = tqb // LANES
    ybt = pl.pallas_call(
        functools.partial(_mix_b_kernel, topk),
        grid=(bsz, seq // tqb),
        in_specs=[q_row(tqb, WIDTH), q_row(tqb, 2 * LANES), q_col(tqb, SUBLANES), q_col(tqb, WIDTH),
                  per_b3(seq, LANES), per_b3(seq, LANES), per_b4(n_blk, KV_LATENT, KBLK),
                  _full((N_HEADS, HEAD_DIM, KV_LATENT)), _full((WIDTH, tqb)), _full((KBLK, KBLK))],
        out_specs=q_col(tqb, WIDTH),
        out_shape=jax.ShapeDtypeStruct((bsz, WIDTH, seq), BF16),
        scratch_shapes=[pltpu.VMEM((n_blk, KBLK, tqb), F32),
                        pltpu.VMEM((n_blk, KBLK, tqb), BF16),
                        pltpu.VMEM((n_grp, IDX_HEADS * LANES, LANES), BF16),
                        pltpu.VMEM((n_grp, N_HEADS * LANES, 2 * LANES), BF16),
                        pltpu.VMEM((KBLK, n_grp * IDX_HEADS * LANES), F32),
                        pltpu.VMEM((KBLK, tqb), F32),
                        pltpu.VMEM((KBLK, n_grp * N_HEADS * LANES), F32),
                        pltpu.VMEM((KBLK, n_grp * N_HEADS * LANES), BF16),
                        pltpu.VMEM((KV_LATENT, n_grp * N_HEADS * LANES), F32),
                        pltpu.VMEM((WIDTH, tqb), F32)],
        compiler_params=params, name="mix_b",
    )(qb, iq, iwt, zbt, ikd, kbd, ckvt, wuv_t, gain_b,
      jnp.asarray(np.tril(np.ones((KBLK, KBLK), np.float32)), BF16))

    return pl.pallas_call(
        _out_kernel,
        grid=(bsz, seq // tm),
        in_specs=[col_blk(WIDTH), col_blk(WIDTH), row_blk(d_model),
                  _full((2 * WIDTH, d_model)), _full((1, d_model))],
        out_specs=row_blk(d_model),
        out_shape=jax.ShapeDtypeStruct((bsz, seq, d_model), F32),
        compiler_params=params, name="out_proj",
    )(yat, ybt, x, w_out[0].astype(BF16), final_norm_gain[None])
```

```python
import functools
import math

import numpy as np
import jax
import jax.numpy as jnp
from jax import lax
from jax.experimental import pallas as pl
from jax.experimental.pallas import tpu as pltpu

F32 = jnp.float32
BF16 = jnp.bfloat16
I32 = jnp.int32

HEAD_DIM = 64
N_HEADS = 8
WIDTH = N_HEADS * HEAD_DIM
N_PAIRS = N_HEADS // 2
ROT_DIM = HEAD_DIM // 4
ROPE_THETA = 500000.0
KV_LATENT = 128
IDX_HEADS = 4
IDX_DIM = 64
TOPK_MAX = 256
EPS = 1e-6
PATTERNS = ((128, 1), (512, 4), (2048, 16))

LANES = 128
SUBLANES = 8
MXU_DEPTH = 256
KBLK = MXU_DEPTH
TQ_A = 2 * LANES
TQ_B = 4 * LANES
TM = 512
NEG_BIG = -1e30
VMEM_LIMIT = 56 * 1024 * 1024
Q_SCALE = HEAD_DIM ** -0.5 * math.log2(math.e)

U_QA, U_KA, U_VA, U_ZA, U_QB, U_CKV, U_ZB, U_IQ, U_LAST, N_UNITS = 0, 4, 8, 12, 16, 20, 21, 25, 27, 28
D_IN_PADDED = N_UNITS * LANES


def _dot(a, b):
    return jnp.dot(a, b, preferred_element_type=F32)


def _dot_nt(a, b):
    return lax.dot_general(a, b, (((1,), (1,)), ((), ())), preferred_element_type=F32)


def _split_head_pair(xp):
    lane = lax.broadcasted_iota(I32, xp.shape, 1)
    zero = jnp.zeros_like(xp)
    return jnp.where(lane < HEAD_DIM, xp, zero), jnp.where(lane >= HEAD_DIM, xp, zero)


def _proj_kernel(x_ref, gin_ref, w_ref, trig_ref, spread_ref, ones_ref, gkv_ref, wuk_ref, gik_ref,
                 qa_ref, ka_ref, vat_ref, zat_ref, qb_ref, zbt_ref, iq_ref, kbd_ref, ckvt_ref,
                 ikd_ref, iwt_ref):
    tm = x_ref.shape[1]
    nblk = tm // KBLK
    x = x_ref[0]
    ms = jnp.mean(x * x, axis=-1, keepdims=True)
    u = (x * lax.rsqrt(ms + EPS) * gin_ref[...]).astype(BF16)
    table = _dot(trig_ref[0], spread_ref[...]) + ones_ref[...]
    cosf, sina, sinb = (table[:, i * LANES:(i + 1) * LANES] for i in range(3))

    def rope(t):
        return t * cosf + pltpu.roll(t, LANES - ROT_DIM // 2, 1) * sina + pltpu.roll(t, ROT_DIM // 2, 1) * sinb

    def proj(unit):
        return _dot(u, w_ref[:, unit * LANES:(unit + 2) * LANES])

    def store_t(ref, unit, t):
        tt = t.T.astype(BF16)
        for i in range(nblk):
            ref[0, i, unit * LANES:(unit + 1) * LANES, :] = tt[:, i * KBLK:(i + 1) * KBLK]

    def lanes(unit):
        return slice(unit * LANES, (unit + 1) * LANES)

    def finish(unit, t):
        if unit < U_KA:
            qa_ref[0, :, lanes(unit - U_QA)] = (rope(t) * Q_SCALE).astype(BF16)
        elif unit < U_VA:
            ka_ref[0, :, lanes(unit - U_KA)] = rope(t).astype(BF16)
        elif unit < U_ZA:
            store_t(vat_ref, unit - U_VA, t)
        elif unit < U_QB:
            zat_ref[0, lanes(unit - U_ZA), :] = t.T.astype(BF16)
        elif unit < U_CKV:
            qb_ref[0, :, lanes(unit - U_QB)] = (rope(t) * Q_SCALE).astype(BF16)
        elif unit == U_CKV:
            ckv = t * lax.rsqrt(jnp.mean(t * t, axis=-1, keepdims=True) + EPS) * gkv_ref[...]
            store_t(ckvt_ref, 0, ckv)
            kbd_ref[0] = rope(_dot(ckv.astype(BF16), wuk_ref[...])).astype(BF16)
        elif unit < U_IQ:
            zbt_ref[0, lanes(unit - U_ZB), :] = t.T.astype(BF16)
        elif unit < U_LAST:
            iq_ref[0, :, lanes(unit - U_IQ)] = rope(t).astype(BF16)
        else:
            lane = lax.broadcasted_iota(I32, t.shape, 1)
            ik_ms = jnp.sum(jnp.where(lane < IDX_DIM, t * t, 0.0), axis=-1, keepdims=True) / IDX_DIM
            ik = rope(t * lax.rsqrt(ik_ms + EPS) * gik_ref[...])
            ikd_ref[0] = (ik + pltpu.roll(ik, IDX_DIM, 1)).astype(BF16)
            iwt_ref[0] = t.T[IDX_DIM:IDX_DIM + SUBLANES, :] * (IDX_HEADS ** -0.5 * IDX_DIM ** -0.5)

    first = [U_LAST - 1, U_CKV]
    for unit in first + [u for u in range(0, N_UNITS, 2) if u not in first]:
        acc = proj(unit)
        finish(unit, acc[:, :LANES])
        finish(unit + 1, acc[:, LANES:])


def _mix_a_bias_tables(tq, tk):
    assert tq == LANES
    o_last = PATTERNS[1][0] + tk - LANES
    offsets = list(range(0, o_last + 1, LANES)) + [o_last + LANES]
    qi = np.arange(tq)[None, :]
    kj = np.arange(tk)[:, None]
    tables = []
    for o in offsets:
        d = o + qi - kj
        mult = np.zeros(d.shape, np.int64)
        for window, dil in PATTERNS:
            mult += ((d >= 0) & (d % dil == 0) & (d <= window)).astype(np.int64)
        tables.append(np.where(mult > 0, np.log2(np.maximum(mult, 1)), NEG_BIG))
    return np.stack(tables).astype(np.float32)


def _mix_a_kernel(n_tables, qa_ref, ka_ref, vat_ref, zat_ref, gain_ref, bias_ref,
                  out_ref, qm_ref, s_ref, pt_ref, acc_ref):
    tq = qa_ref.shape[1]
    n_grp = tq // LANES
    j = pl.program_id(1)
    n_kb = (j * tq + tq - 1) // KBLK + 1
    chains = [(g, p) for g in range(n_grp) for p in range(N_PAIRS)]

    def group(g):
        return slice(g * LANES, (g + 1) * LANES)

    for g, p in chains:
        qm_ref[g, 2 * p * LANES:(2 * p + 1) * LANES, :], qm_ref[g, (2 * p + 1) * LANES:(2 * p + 2) * LANES, :] = (
            _split_head_pair(qa_ref[0, group(g), p * LANES:(p + 1) * LANES]))
    acc_ref[...] = jnp.zeros_like(acc_ref)
    pt_ref[...] = jnp.zeros_like(pt_ref)

    def scores(kb, g, p):
        k0 = pl.multiple_of(kb * KBLK, KBLK)
        return _dot_nt(ka_ref[0, pl.ds(k0, KBLK), p * LANES:(p + 1) * LANES],
                       qm_ref[g, 2 * p * LANES:(2 * p + 2) * LANES, :])

    def accumulate(kb, g, p, alpha):
        vt = vat_ref[0, kb, p * LANES:(p + 1) * LANES, :]
        r0 = p * LANES
        acc_ref[r0:r0 + HEAD_DIM, group(g)] = (acc_ref[r0:r0 + HEAD_DIM, group(g)] * alpha[:, :LANES]
                                               + _dot(vt[:HEAD_DIM], pt_ref[g, p, :, :LANES]))
        acc_ref[r0 + HEAD_DIM:r0 + LANES, group(g)] = (acc_ref[r0 + HEAD_DIM:r0 + LANES, group(g)] * alpha[:, LANES:]
                                                       + _dot(vt[HEAD_DIM:], pt_ref[g, p, :, LANES:]))

    for g, p in chains:
        s_ref[g, p] = scores(0, g, p)

    def body(kb, carry):
        ms, ls, alphas = carry
        prev = jnp.maximum(kb - 1, 0)
        nxt = jnp.minimum(kb + 1, n_kb - 1)
        new_ms, new_ls, new_alphas = [], [], []
        for c, (g, p) in enumerate(chains):
            bias = bias_ref[jnp.minimum((j * tq + g * LANES - kb * KBLK) // LANES, n_tables - 1)]
            accumulate(prev, g, p, alphas[c])
            s = s_ref[g, p] + jnp.concatenate([bias, bias], axis=1)
            m_new = jnp.maximum(ms[c], jnp.max(s, axis=0, keepdims=True))
            alpha = jnp.exp2(ms[c] - m_new)
            pt = jnp.exp2(s - m_new)
            new_ms.append(m_new)
            new_ls.append(alpha * ls[c] + jnp.sum(pt, axis=0, keepdims=True))
            new_alphas.append(alpha)
            pt_ref[g, p] = pt.astype(BF16)
            s_ref[g, p] = scores(nxt, g, p)
        return tuple(new_ms), tuple(new_ls), tuple(new_alphas)

    init = (tuple(jnp.full((1, 2 * LANES), NEG_BIG, F32) for _ in chains),
            tuple(jnp.zeros((1, 2 * LANES), F32) for _ in chains),
            tuple(jnp.ones((1, 2 * LANES), F32) for _ in chains))
    _, ls, alphas = lax.fori_loop(0, n_kb, body, init)
    for c, (g, p) in enumerate(chains):
        accumulate(n_kb - 1, g, p, alphas[c])
        r0 = p * LANES
        acc_ref[r0:r0 + HEAD_DIM, group(g)] = acc_ref[r0:r0 + HEAD_DIM, group(g)] / ls[c][:, :LANES]
        acc_ref[r0 + HEAD_DIM:r0 + LANES, group(g)] = acc_ref[r0 + HEAD_DIM:r0 + LANES, group(g)] / ls[c][:, LANES:]

    z = zat_ref[0].astype(F32)
    g = acc_ref[...] * (z / (1.0 + jnp.exp(-z)))
    ms = jnp.mean(g * g, axis=0, keepdims=True)
    out_ref[0] = (g * lax.rsqrt(ms + EPS) * gain_ref[...]).astype(BF16)


def _mix_b_kernel(topk, qb_ref, iq_ref, iwt_ref, zbt_ref, ikd_ref, kbd_ref, ckvt_ref, wuvt_ref,
                  gain_ref, tri_ref, out_ref, sc_ref, bias_ref, iqm_ref, rhs_ref, raw_ref, rank_ref,
                  s_ref, pt_ref, acc_ref, g_ref):
    tq = qb_ref.shape[1]
    n_grp = tq // LANES
    iw_cols = IDX_HEADS * LANES
    hd_cols = N_HEADS * LANES
    j = pl.program_id(1)
    groups = list(range(n_grp))
    long_groups = [g for g in groups if (g * LANES + LANES - 1) // KBLK > 0]
    short_groups = [g for g in groups if g not in long_groups]
    assert tq <= 2 * KBLK and (not long_groups or tq % KBLK == 0)
    long_lanes = slice(len(short_groups) * LANES, tq)
    n_main = (j * tq + LANES - 1) // KBLK + 1
    n_kb = n_main + (1 if long_groups else 0)
    row = lax.broadcasted_iota(I32, (KBLK, tq), 0)
    t_pos = j * tq + lax.broadcasted_iota(I32, (KBLK, tq), 1)

    def key_rows(ref, kb):
        return ref[0, pl.ds(pl.multiple_of(kb * KBLK, KBLK), KBLK), :]

    def next_block(kb):
        return jnp.minimum(kb + 1, n_kb - 1)

    def next_block_of(kb, g):
        return jnp.minimum(kb + 1, (n_kb if g in long_groups else n_main) - 1)

    def group(g):
        return slice(g * LANES, (g + 1) * LANES)


    for g in range(n_grp):
        for p in range(IDX_HEADS // 2):
            iqm_ref[g, 2 * p * LANES:(2 * p + 1) * LANES, :], iqm_ref[g, (2 * p + 1) * LANES:(2 * p + 2) * LANES, :] = (
                _split_head_pair(iq_ref[0, group(g), p * LANES:(p + 1) * LANES]))

    def store_raw_scores(kb, g):
        raw_ref[:, g * iw_cols:(g + 1) * iw_cols] = _dot_nt(key_rows(ikd_ref, kb), iqm_ref[g])

    def finish_scores(kb, g):
        c0 = g * iw_cols
        sc = jnp.maximum(raw_ref[:, c0:c0 + LANES], 0.0) * iwt_ref[0, 0:1, group(g)]
        for h in range(1, IDX_HEADS):
            sc = sc + (jnp.maximum(raw_ref[:, c0 + h * LANES:c0 + (h + 1) * LANES], 0.0)
                       * iwt_ref[0, h:h + 1, group(g)])
        causal = (kb * KBLK + lax.broadcasted_iota(I32, (KBLK, LANES), 0)
                  <= j * tq + g * LANES + lax.broadcasted_iota(I32, (KBLK, LANES), 1))
        sc_ref[kb, :, group(g)] = jnp.where(causal, sc, -jnp.inf)

    for g in groups:
        store_raw_scores(0, g)

    def score_body(kb, carry):
        for g in groups:
            finish_scores(kb, g)
            store_raw_scores(next_block_of(kb, g), g)
        return carry

    lax.fori_loop(0, n_main, score_body, 0)
    if long_groups:
        for g in long_groups:
            finish_scores(n_main, g)
        sc_ref[n_main, :, :long_lanes.start] = jnp.full((KBLK, long_lanes.start), -jnp.inf, F32)

    def count(pred):
        def partial(hit):
            return jnp.sum(hit.astype(I32).reshape(KBLK // SUBLANES, SUBLANES, hit.shape[1]), axis=0)
        acc = lax.fori_loop(0, n_main, lambda kb, acc: acc + partial(pred(sc_ref[kb], slice(None))),
                            jnp.zeros((SUBLANES, tq), I32))
        if long_groups:
            tail = partial(pred(sc_ref[n_main, :, long_lanes], long_lanes))
            acc = jnp.concatenate([acc[:, :long_lanes.start], acc[:, long_lanes] + tail], axis=1)
        return jnp.sum(acc, axis=0, keepdims=True)

    def key_to_float(key):
        bits = key ^ ((key >> 31) & jnp.int32(0x7FFFFFFF))
        return lax.bitcast_convert_type(bits, F32)

    def value_step(i, prefix):
        cand = prefix + (jnp.int32(1) << (31 - i))
        thr = key_to_float(cand)
        return jnp.where(count(lambda s, lanes: s >= thr[:, lanes]) >= topk, cand, prefix)

    prefix = lax.fori_loop(0, 32, value_step, jnp.full((1, tq), -2 ** 31, I32))
    thr = key_to_float(prefix)
    need = (topk - count(lambda s, lanes: s > thr[:, lanes])).astype(F32)

    def tie_ranks(kb):
        tie = jnp.where(sc_ref[kb] == thr, 1.0, 0.0).astype(BF16)
        return _dot(tri_ref[...], tie)

    rank_ref[...] = tie_ranks(0)

    def bias_body(kb, ties_before):
        sc = sc_ref[kb]
        rank = rank_ref[...]
        take = (sc == thr) & (rank + ties_before <= need)
        sel = (sc > thr) | take | (t_pos < topk)
        bias_ref[kb] = jnp.where(sel & (kb * KBLK + row <= t_pos), 0.0, NEG_BIG).astype(BF16)
        rank_ref[...] = tie_ranks(next_block(kb))
        return ties_before + rank[KBLK - 1:KBLK, :]

    lax.fori_loop(0, n_kb, bias_body, jnp.zeros((1, tq), F32))

    eye = (lax.broadcasted_iota(I32, (LANES, LANES), 0)
           == lax.broadcasted_iota(I32, (LANES, LANES), 1)).astype(BF16)
    for g in range(n_grp):
        for p in range(N_PAIRS):
            rhs_ref[g, 2 * p * LANES:(2 * p + 1) * LANES, :LANES], rhs_ref[g, (2 * p + 1) * LANES:(2 * p + 2) * LANES, :LANES] = (
                _split_head_pair(qb_ref[0, group(g), p * LANES:(p + 1) * LANES]))
        for h in range(N_HEADS):
            rhs_ref[g, h * LANES:(h + 1) * LANES, LANES:] = eye
    acc_ref[...] = jnp.zeros_like(acc_ref)
    pt_ref[...] = jnp.zeros_like(pt_ref)

    def cols(g):
        return slice(g * hd_cols, (g + 1) * hd_cols)

    def store_masked_scores(kb, g):
        lhs = jnp.concatenate([key_rows(kbd_ref, kb), bias_ref[kb, :, group(g)]], axis=1)
        s_ref[:, cols(g)] = _dot_nt(lhs, rhs_ref[g])

    def accumulate(kb, g, alpha):
        acc_ref[:, cols(g)] = acc_ref[:, cols(g)] * alpha + _dot(ckvt_ref[0, kb], pt_ref[:, cols(g)])

    for g in range(n_grp):
        store_masked_scores(0, g)

    def softmax_block(g, m, l):
        s = s_ref[:, cols(g)]
        m_new = jnp.maximum(m, jnp.max(s, axis=0, keepdims=True))
        alpha = jnp.exp2(m - m_new)
        pt = jnp.exp2(s - m_new)
        pt_ref[:, cols(g)] = pt.astype(BF16)
        return m_new, alpha * l + jnp.sum(pt, axis=0, keepdims=True), alpha

    def attn_body(kb, carry):
        ms, ls, alphas = carry
        prev = jnp.maximum(kb - 1, 0)
        stats = []
        for g in groups:
            accumulate(prev, g, alphas[g])
            stats.append(softmax_block(g, ms[g], ls[g]))
            store_masked_scores(next_block_of(kb, g), g)
        return tuple(zip(*stats))

    init = (tuple(jnp.full((1, hd_cols), NEG_BIG, F32) for _ in groups),
            tuple(jnp.zeros((1, hd_cols), F32) for _ in groups),
            tuple(jnp.ones((1, hd_cols), F32) for _ in groups))
    ms, ls, alphas = lax.fori_loop(0, n_main, attn_body, init)
    ls, alphas = list(ls), list(alphas)
    for g in groups:
        accumulate(n_main - 1, g, alphas[g])
        if g in long_groups:
            _, ls[g], alphas[g] = softmax_block(g, ms[g], ls[g])
            accumulate(n_main, g, alphas[g])
        o_lat = (acc_ref[:, cols(g)] / ls[g]).astype(BF16)
        for h in range(N_HEADS):
            g_ref[h * HEAD_DIM:(h + 1) * HEAD_DIM, group(g)] = _dot(wuvt_ref[h], o_lat[:, h * LANES:(h + 1) * LANES])

    z = zbt_ref[0].astype(F32)
    gated = g_ref[...] * (z / (1.0 + jnp.exp(-z)))
    ms = jnp.mean(gated * gated, axis=0, keepdims=True)
    out_ref[0] = (gated * lax.rsqrt(ms + EPS) * gain_ref[...]).astype(BF16)


def _out_kernel(yat_ref, ybt_ref, x_ref, w_ref, gain_ref, out_ref):
    y = jnp.concatenate([yat_ref[0].astype(F32).T, ybt_ref[0].astype(F32).T], axis=1).astype(BF16)
    h = x_ref[0] + _dot(y, w_ref[...])
    ms = jnp.mean(h * h, axis=-1, keepdims=True)
    out_ref[0] = h * lax.rsqrt(ms + EPS) * gain_ref[...]


N_SPLIT = 3


def _rope_trig(positions):
    inv_freq = ROPE_THETA ** (-jnp.arange(0, ROT_DIM, 2, dtype=F32) / ROT_DIM)
    ang = positions.astype(F32)[..., None] * inv_freq
    rest = jnp.concatenate([jnp.cos(ang), jnp.sin(ang)], axis=-1)
    pieces = []
    for _ in range(N_SPLIT):
        pieces.append(rest.astype(BF16))
        rest = rest - pieces[-1].astype(F32)
    return jnp.concatenate(pieces, axis=-1)


def _rope_spread():
    half = ROT_DIM // 2
    c = np.arange(LANES) % HEAD_DIM
    hit = (np.arange(half)[:, None] == (c % half)[None, :])
    spread = np.zeros((2 * half, 3 * LANES), np.float32)
    spread[:half, :LANES] = hit & (c < ROT_DIM)
    spread[half:, LANES:2 * LANES] = -1.0 * (hit & (c < half))
    spread[half:, 2 * LANES:] = hit & (c >= half) & (c < ROT_DIM)
    ones = np.concatenate([(c >= ROT_DIM), np.zeros(2 * LANES, bool)]).astype(np.float32)[None]
    return np.tile(spread, (N_SPLIT, 1)), ones


def _full(shape):
    return pl.BlockSpec(shape, lambda *_: (0,) * len(shape))


@jax.jit
def kernel(x, positions, norm_in_gain, w_in, kv_norm_gain, w_uk, w_uv, idx_k_norm_gain,
           branch_norm_gain_a, branch_norm_gain_b, w_out, final_norm_gain):
    bsz, seq, d_model = x.shape
    assert w_in.shape[0] == 1
    assert seq % KBLK == 0 and seq <= PATTERNS[-1][0]
    n_blk = seq // KBLK
    topk = min(TOPK_MAX, seq // 4)
    tm = min(TM, seq)
    tqa, tqb = TQ_A, TQ_B
    params = pltpu.CompilerParams(dimension_semantics=("arbitrary", "arbitrary"),
                                  vmem_limit_bytes=VMEM_LIMIT)

    trig = _rope_trig(positions)
    spread, rope_ones = _rope_spread()
    w_padded = jnp.pad(w_in[0], ((0, 0), (0, D_IN_PADDED - w_in.shape[2]))).astype(BF16)
    wuk_dup = jnp.concatenate([w_uk[0], w_uk[0]], axis=1).astype(BF16)
    gik = jnp.concatenate([idx_k_norm_gain[0], jnp.zeros((LANES - IDX_DIM,), F32)])[None]
    wuv_t = jnp.swapaxes(w_uv[0], 1, 2).astype(BF16)
    gain_a = jnp.broadcast_to(branch_norm_gain_a[0][:, None], (WIDTH, tqa))
    gain_b = jnp.broadcast_to(branch_norm_gain_b[0][:, None], (WIDTH, tqb))

    row_blk = lambda c: pl.BlockSpec((1, tm, c), lambda b, i: (b, i, 0))
    col_blk = lambda c: pl.BlockSpec((1, c, tm), lambda b, i: (b, 0, i))
    key_blk = lambda c: pl.BlockSpec((1, tm // KBLK, c, KBLK), lambda b, i: (b, i, 0, 0))
    qa, ka, vat, zat, qb, zbt, iq, kbd, ckvt, ikd, iwt = pl.pallas_call(
        _proj_kernel,
        grid=(bsz, seq // tm),
        in_specs=[row_blk(d_model), _full((1, d_model)), _full((d_model, D_IN_PADDED)),
                  row_blk(N_SPLIT * ROT_DIM), _full(spread.shape), _full(rope_ones.shape),
                  _full((1, KV_LATENT)), _full((KV_LATENT, LANES)), _full((1, LANES))],
        out_specs=[row_blk(WIDTH), row_blk(WIDTH), key_blk(WIDTH), col_blk(WIDTH),
                   row_blk(WIDTH), col_blk(WIDTH), row_blk(2 * LANES), row_blk(LANES),
                   key_blk(KV_LATENT), row_blk(LANES), col_blk(SUBLANES)],
        out_shape=[jax.ShapeDtypeStruct((bsz, seq, WIDTH), BF16),
                   jax.ShapeDtypeStruct((bsz, seq, WIDTH), BF16),
                   jax.ShapeDtypeStruct((bsz, n_blk, WIDTH, KBLK), BF16),
                   jax.ShapeDtypeStruct((bsz, WIDTH, seq), BF16),
                   jax.ShapeDtypeStruct((bsz, seq, WIDTH), BF16),
                   jax.ShapeDtypeStruct((bsz, WIDTH, seq), BF16),
                   jax.ShapeDtypeStruct((bsz, seq, 2 * LANES), BF16),
                   jax.ShapeDtypeStruct((bsz, seq, LANES), BF16),
                   jax.ShapeDtypeStruct((bsz, n_blk, KV_LATENT, KBLK), BF16),
                   jax.ShapeDtypeStruct((bsz, seq, LANES), BF16),
                   jax.ShapeDtypeStruct((bsz, SUBLANES, seq), F32)],
        compiler_params=params, name="proj",
    )(x, norm_in_gain[0][None], w_padded, trig, jnp.asarray(spread, BF16), jnp.asarray(rope_ones),
      kv_norm_gain[0][None], wuk_dup, gik)

    q_row = lambda tq, c: pl.BlockSpec((1, tq, c), lambda b, i: (b, i, 0))
    q_col = lambda tq, c: pl.BlockSpec((1, c, tq), lambda b, i: (b, 0, i))
    per_b3 = lambda s1, s2: pl.BlockSpec((1, s1, s2), lambda b, i: (b, 0, 0))
    per_b4 = lambda s1, s2, s3: pl.BlockSpec((1, s1, s2, s3), lambda b, i: (b, 0, 0, 0))

    bias_np = _mix_a_bias_tables(LANES, KBLK)
    n_grp = tqa // LANES
    yat = pl.pallas_call(
        functools.partial(_mix_a_kernel, bias_np.shape[0]),
        grid=(bsz, seq // tqa),
        in_specs=[q_row(tqa, WIDTH), per_b3(seq, WIDTH), per_b4(n_blk, WIDTH, KBLK), q_col(tqa, WIDTH),
                  _full((WIDTH, tqa)), _full(bias_np.shape)],
        out_specs=q_col(tqa, WIDTH),
        out_shape=jax.ShapeDtypeStruct((bsz, WIDTH, seq), BF16),
        scratch_shapes=[pltpu.VMEM((n_grp, N_HEADS * LANES, LANES), BF16),
                        pltpu.VMEM((n_grp, N_PAIRS, KBLK, 2 * LANES), F32),
                        pltpu.VMEM((n_grp, N_PAIRS, KBLK, 2 * LANES), BF16),
                        pltpu.VMEM((WIDTH, tqa), F32)],
        compiler_params=params, name="mix_a",
    )(qa, ka, vat, zat, gain_a, jnp.asarray(bias_np))

    n_grp = tqb // LANES
    ybt = pl.pallas_call(
        functools.partial(_mix_b_kernel, topk),
        grid=(bsz, seq // tqb),
        in_specs=[q_row(tqb, WIDTH), q_row(tqb, 2 * LANES), q_col(tqb, SUBLANES), q_col(tqb, WIDTH),
                  per_b3(seq, LANES), per_b3(seq, LANES), per_b4(n_blk, KV_LATENT, KBLK),
                  _full((N_HEADS, HEAD_DIM, KV_LATENT)), _full((WIDTH, tqb)), _full((KBLK, KBLK))],
        out_specs=q_col(tqb, WIDTH),
        out_shape=jax.ShapeDtypeStruct((bsz, WIDTH, seq), BF16),
        scratch_shapes=[pltpu.VMEM((n_blk, KBLK, tqb), F32),
                        pltpu.VMEM((n_blk, KBLK, tqb), BF16),
                        pltpu.VMEM((n_grp, IDX_HEADS * LANES, LANES), BF16),
                        pltpu.VMEM((n_grp, N_HEADS * LANES, 2 * LANES), BF16),
                        pltpu.VMEM((KBLK, n_grp * IDX_HEADS * LANES), F32),
                        pltpu.VMEM((KBLK, tqb), F32),
                        pltpu.VMEM((KBLK, n_grp * N_HEADS * LANES), F32),
                        pltpu.VMEM((KBLK, n_grp * N_HEADS * LANES), BF16),
                        pltpu.VMEM((KV_LATENT, n_grp * N_HEADS * LANES), F32),
                        pltpu.VMEM((WIDTH, tqb), F32)],
        compiler_params=params, name="mix_b",
    )(qb, iq, iwt, zbt, ikd, kbd, ckvt, wuv_t, gain_b,
      jnp.asarray(np.tril(np.ones((KBLK, KBLK), np.float32)), BF16))

    return pl.pallas_call(
        _out_kernel,
        grid=(bsz, seq // tm),
        in_specs=[col_blk(WIDTH), col_blk(WIDTH), row_blk(d_model),
                  _full((2 * WIDTH, d_model)), _full((1, d_model))],
        out_specs=row_blk(d_model),
        out_shape=jax.ShapeDtypeStruct((bsz, seq, d_model), F32),
        compiler_params=params, name="out_proj",
    )(yat, ybt, x, w_out[0].astype(BF16), final_norm_gain[None])
```

```python
import functools
import math

import numpy as np
import jax
import jax.numpy as jnp
from jax import lax
from jax.experimental import pallas as pl
from jax.experimental.pallas import tpu as pltpu

F32 = jnp.float32
BF16 = jnp.bfloat16
I32 = jnp.int32

HEAD_DIM = 64
N_HEADS = 8
WIDTH = N_HEADS * HEAD_DIM
N_PAIRS = N_HEADS // 2
ROT_DIM = HEAD_DIM // 4
ROPE_THETA = 500000.0
KV_LATENT = 128
IDX_HEADS = 4
IDX_DIM = 64
TOPK_MAX = 256
EPS = 1e-6
PATTERNS = ((128, 1), (512, 4), (2048, 16))

LANES = 128
SUBLANES = 8
MXU_DEPTH = 256
KBLK = MXU_DEPTH
TQ_A = 2 * LANES
TQ_B = 4 * LANES
TM = 1024
NEG_BIG = -1e30
VMEM_LIMIT = 56 * 1024 * 1024
Q_SCALE = HEAD_DIM ** -0.5 * math.log2(math.e)

U_QA, U_KA, U_VA, U_ZA, U_QB, U_CKV, U_ZB, U_IQ, U_LAST, N_UNITS = 0, 4, 8, 12, 16, 20, 21, 25, 27, 28
D_IN_PADDED = N_UNITS * LANES


def _dot(a, b):
    return jnp.dot(a, b, preferred_element_type=F32)


def _dot_nt(a, b):
    return lax.dot_general(a, b, (((1,), (1,)), ((), ())), preferred_element_type=F32)


def _split_head_pair(xp):
    lane = lax.broadcasted_iota(I32, xp.shape, 1)
    zero = jnp.zeros_like(xp)
    return jnp.where(lane < HEAD_DIM, xp, zero), jnp.where(lane >= HEAD_DIM, xp, zero)


def _proj_kernel(x_ref, gin_ref, w_ref, trig_ref, spread_ref, ones_ref, gkv_ref, wuk_ref, gik_ref,
                 qa_ref, ka_ref, vat_ref, zat_ref, qb_ref, zbt_ref, iq_ref, kbd_ref, ckvt_ref,
                 ikd_ref, iwt_ref):
    tm = x_ref.shape[1]
    nblk = tm // KBLK
    x = x_ref[0]
    ms = jnp.mean(x * x, axis=-1, keepdims=True)
    u = (x * lax.rsqrt(ms + EPS) * gin_ref[...]).astype(BF16)
    table = _dot(trig_ref[0], spread_ref[...]) + ones_ref[...]
    cosf, sina, sinb = (table[:, i * LANES:(i + 1) * LANES] for i in range(3))

    def rope(t):
        return t * cosf + pltpu.roll(t, LANES - ROT_DIM // 2, 1) * sina + pltpu.roll(t, ROT_DIM // 2, 1) * sinb

    def proj(unit):
        return _dot(u, w_ref[:, unit * LANES:(unit + 2) * LANES])

    def store_t(ref, unit, t):
        tt = t.T.astype(BF16)
        for i in range(nblk):
            ref[0, i, unit * LANES:(unit + 1) * LANES, :] = tt[:, i * KBLK:(i + 1) * KBLK]

    def lanes(unit):
        return slice(unit * LANES, (unit + 1) * LANES)

    def finish(unit, t):
        if unit < U_KA:
            qa_ref[0, :, lanes(unit - U_QA)] = (rope(t) * Q_SCALE).astype(BF16)
        elif unit < U_VA:
            ka_ref[0, :, lanes(unit - U_KA)] = rope(t).astype(BF16)
        elif unit < U_ZA:
            store_t(vat_ref, unit - U_VA, t)
        elif unit < U_QB:
            zat_ref[0, lanes(unit - U_ZA), :] = t.T.astype(BF16)
        elif unit < U_CKV:
            qb_ref[0, :, lanes(unit - U_QB)] = (rope(t) * Q_SCALE).astype(BF16)
        elif unit == U_CKV:
            ckv = t * lax.rsqrt(jnp.mean(t * t, axis=-1, keepdims=True) + EPS) * gkv_ref[...]
            store_t(ckvt_ref, 0, ckv)
            kbd_ref[0] = rope(_dot(ckv.astype(BF16), wuk_ref[...])).astype(BF16)
        elif unit < U_IQ:
            zbt_ref[0, lanes(unit - U_ZB), :] = t.T.astype(BF16)
        elif unit < U_LAST:
            iq_ref[0, :, lanes(unit - U_IQ)] = rope(t).astype(BF16)
        else:
            lane = lax.broadcasted_iota(I32, t.shape, 1)
            ik_ms = jnp.sum(jnp.where(lane < IDX_DIM, t * t, 0.0), axis=-1, keepdims=True) / IDX_DIM
            ik = rope(t * lax.rsqrt(ik_ms + EPS) * gik_ref[...])
            ikd_ref[0] = (ik + pltpu.roll(ik, IDX_DIM, 1)).astype(BF16)
            iwt_ref[0] = t.T[IDX_DIM:IDX_DIM + SUBLANES, :] * (IDX_HEADS ** -0.5 * IDX_DIM ** -0.5)

    first = [U_LAST - 1, U_CKV]
    for unit in first + [u for u in range(0, N_UNITS, 2) if u not in first]:
        acc = proj(unit)
        finish(unit, acc[:, :LANES])
        finish(unit + 1, acc[:, LANES:])


def _mix_a_bias_tables(tq, tk):
    assert tq == LANES
    o_last = PATTERNS[1][0] + tk - LANES
    offsets = list(range(0, o_last + 1, LANES)) + [o_last + LANES]
    qi = np.arange(tq)[None, :]
    kj = np.arange(tk)[:, None]
    tables = []
    for o in offsets:
        d = o + qi - kj
        mult = np.zeros(d.shape, np.int64)
        for window, dil in PATTERNS:
            mult += ((d >= 0) & (d % dil == 0) & (d <= window)).astype(np.int64)
        tables.append(np.where(mult > 0, np.log2(np.maximum(mult, 1)), NEG_BIG))
    return np.stack(tables).astype(np.float32)


def _mix_a_kernel(n_tables, qa_ref, ka_ref, vat_ref, zat_ref, gain_ref, bias_ref,
                  out_ref, qm_ref, s_ref, pt_ref, acc_ref):
    tq = qa_ref.shape[1]
    n_grp = tq // LANES
    j = pl.program_id(1)
    n_kb = (j * tq + tq - 1) // KBLK + 1
    chains = [(g, p) for g in range(n_grp) for p in range(N_PAIRS)]

    def group(g):
        return slice(g * LANES, (g + 1) * LANES)

    for g, p in chains:
        qm_ref[g, 2 * p * LANES:(2 * p + 1) * LANES, :], qm_ref[g, (2 * p + 1) * LANES:(2 * p + 2) * LANES, :] = (
            _split_head_pair(qa_ref[0, group(g), p * LANES:(p + 1) * LANES]))
    acc_ref[...] = jnp.zeros_like(acc_ref)
    pt_ref[...] = jnp.zeros_like(pt_ref)

    def scores(kb, g, p):
        k0 = pl.multiple_of(kb * KBLK, KBLK)
        return _dot_nt(ka_ref[0, pl.ds(k0, KBLK), p * LANES:(p + 1) * LANES],
                       qm_ref[g, 2 * p * LANES:(2 * p + 2) * LANES, :])

    def accumulate(kb, g, p, alpha):
        vt = vat_ref[0, kb, p * LANES:(p + 1) * LANES, :]
        r0 = p * LANES
        acc_ref[r0:r0 + HEAD_DIM, group(g)] = (acc_ref[r0:r0 + HEAD_DIM, group(g)] * alpha[:, :LANES]
                                               + _dot(vt[:HEAD_DIM], pt_ref[g, p, :, :LANES]))
        acc_ref[r0 + HEAD_DIM:r0 + LANES, group(g)] = (acc_ref[r0 + HEAD_DIM:r0 + LANES, group(g)] * alpha[:, LANES:]
                                                       + _dot(vt[HEAD_DIM:], pt_ref[g, p, :, LANES:]))

    for g, p in chains:
        s_ref[g, p] = scores(0, g, p)

    def body(kb, carry):
        ms, ls, alphas = carry
        prev = jnp.maximum(kb - 1, 0)
        nxt = jnp.minimum(kb + 1, n_kb - 1)
        new_ms, new_ls, new_alphas = [], [], []
        for c, (g, p) in enumerate(chains):
            bias = bias_ref[jnp.minimum((j * tq + g * LANES - kb * KBLK) // LANES, n_tables - 1)]
            accumulate(prev, g, p, alphas[c])
            s = s_ref[g, p] + jnp.concatenate([bias, bias], axis=1)
            m_new = jnp.maximum(ms[c], jnp.max(s, axis=0, keepdims=True))
            alpha = jnp.exp2(ms[c] - m_new)
            pt = jnp.exp2(s - m_new)
            new_ms.append(m_new)
            new_ls.append(alpha * ls[c] + jnp.sum(pt, axis=0, keepdims=True))
            new_alphas.append(alpha)
            pt_ref[g, p] = pt.astype(BF16)
            s_ref[g, p] = scores(nxt, g, p)
        return tuple(new_ms), tuple(new_ls), tuple(new_alphas)

    init = (tuple(jnp.full((1, 2 * LANES), NEG_BIG, F32) for _ in chains),
            tuple(jnp.zeros((1, 2 * LANES), F32) for _ in chains),
            tuple(jnp.ones((1, 2 * LANES), F32) for _ in chains))
    _, ls, alphas = lax.fori_loop(0, n_kb, body, init)
    for c, (g, p) in enumerate(chains):
        accumulate(n_kb - 1, g, p, alphas[c])
        r0 = p * LANES
        acc_ref[r0:r0 + HEAD_DIM, group(g)] = acc_ref[r0:r0 + HEAD_DIM, group(g)] / ls[c][:, :LANES]
        acc_ref[r0 + HEAD_DIM:r0 + LANES, group(g)] = acc_ref[r0 + HEAD_DIM:r0 + LANES, group(g)] / ls[c][:, LANES:]

    z = zat_ref[0].astype(F32)
    g = acc_ref[...] * (z / (1.0 + jnp.exp(-z)))
    ms = jnp.mean(g * g, axis=0, keepdims=True)
    out_ref[0] = (g * lax.rsqrt(ms + EPS) * gain_ref[...]).astype(BF16)


def _mix_b_kernel(topk, qb_ref, iq_ref, iwt_ref, zbt_ref, ikd_ref, kbd_ref, ckvt_ref, wuvt_ref,
                  gain_ref, tri_ref, out_ref, sc_ref, bias_ref, iqm_ref, rhs_ref, raw_ref, rank_ref,
                  s_ref, pt_ref, acc_ref, g_ref):
    tq = qb_ref.shape[1]
    n_grp = tq // LANES
    iw_cols = IDX_HEADS * LANES
    hd_cols = N_HEADS * LANES
    j = pl.program_id(1)
    groups = list(range(n_grp))
    long_groups = [g for g in groups if (g * LANES + LANES - 1) // KBLK > 0]
    short_groups = [g for g in groups if g not in long_groups]
    assert tq <= 2 * KBLK and (not long_groups or tq % KBLK == 0)
    long_lanes = slice(len(short_groups) * LANES, tq)
    n_main = (j * tq + LANES - 1) // KBLK + 1
    n_kb = n_main + (1 if long_groups else 0)
    row = lax.broadcasted_iota(I32, (KBLK, tq), 0)
    t_pos = j * tq + lax.broadcasted_iota(I32, (KBLK, tq), 1)

    def key_rows(ref, kb):
        return ref[0, pl.ds(pl.multiple_of(kb * KBLK, KBLK), KBLK), :]

    def next_block(kb):
        return jnp.minimum(kb + 1, n_kb - 1)

    def next_block_of(kb, g):
        return jnp.minimum(kb + 1, (n_kb if g in long_groups else n_main) - 1)

    def group(g):
        return slice(g * LANES, (g + 1) * LANES)


    for g in range(n_grp):
        for p in range(IDX_HEADS // 2):
            iqm_ref[g, 2 * p * LANES:(2 * p + 1) * LANES, :], iqm_ref[g, (2 * p + 1) * LANES:(2 * p + 2) * LANES, :] = (
                _split_head_pair(iq_ref[0, group(g), p * LANES:(p + 1) * LANES]))

    def store_raw_scores(kb, g):
        raw_ref[:, g * iw_cols:(g + 1) * iw_cols] = _dot_nt(key_rows(ikd_ref, kb), iqm_ref[g])

    def finish_scores(kb, g):
        c0 = g * iw_cols
        sc = jnp.maximum(raw_ref[:, c0:c0 + LANES], 0.0) * iwt_ref[0, 0:1, group(g)]
        for h in range(1, IDX_HEADS):
            sc = sc + (jnp.maximum(raw_ref[:, c0 + h * LANES:c0 + (h + 1) * LANES], 0.0)
                       * iwt_ref[0, h:h + 1, group(g)])
        causal = (kb * KBLK + lax.broadcasted_iota(I32, (KBLK, LANES), 0)
                  <= j * tq + g * LANES + lax.broadcasted_iota(I32, (KBLK, LANES), 1))
        sc_ref[kb, :, group(g)] = jnp.where(causal, sc, -jnp.inf)

    for g in groups:
        store_raw_scores(0, g)

    def score_body(kb, carry):
        for g in groups:
            finish_scores(kb, g)
            store_raw_scores(next_block_of(kb, g), g)
        return carry

    lax.fori_loop(0, n_main, score_body, 0)
    if long_groups:
        for g in long_groups:
            finish_scores(n_main, g)
        sc_ref[n_main, :, :long_lanes.start] = jnp.full((KBLK, long_lanes.start), -jnp.inf, F32)

    def count(pred):
        def partial(hit):
            return jnp.sum(hit.astype(I32).reshape(KBLK // SUBLANES, SUBLANES, hit.shape[1]), axis=0)
        acc = lax.fori_loop(0, n_main, lambda kb, acc: acc + partial(pred(sc_ref[kb], slice(None))),
                            jnp.zeros((SUBLANES, tq), I32))
        if long_groups:
            tail = partial(pred(sc_ref[n_main, :, long_lanes], long_lanes))
            acc = jnp.concatenate([acc[:, :long_lanes.start], acc[:, long_lanes] + tail], axis=1)
        return jnp.sum(acc, axis=0, keepdims=True)

    def key_to_float(key):
        bits = key ^ ((key >> 31) & jnp.int32(0x7FFFFFFF))
        return lax.bitcast_convert_type(bits, F32)

    def value_step(i, prefix):
        cand = prefix + (jnp.int32(1) << (31 - i))
        thr = key_to_float(cand)
        return jnp.where(count(lambda s, lanes: s >= thr[:, lanes]) >= topk, cand, prefix)

    prefix = lax.fori_loop(0, 32, value_step, jnp.full((1, tq), -2 ** 31, I32))
    thr = key_to_float(prefix)
    need = (topk - count(lambda s, lanes: s > thr[:, lanes])).astype(F32)

    def tie_ranks(kb):
        tie = jnp.where(sc_ref[kb] == thr, 1.0, 0.0).astype(BF16)
        return _dot(tri_ref[...], tie)

    per_trip = 2 if long_groups else 1
    for u in range(per_trip):
        rank_ref[u] = tie_ranks(u)

    def bias_body(trip, ties_before):
        for u in range(per_trip):
            kb = trip * per_trip + u
            sc = sc_ref[kb]
            rank = rank_ref[u]
            take = (sc == thr) & (rank + ties_before <= need)
            sel = (sc > thr) | take | (t_pos < topk)
            bias_ref[kb] = jnp.where(sel & (kb * KBLK + row <= t_pos), 0.0, NEG_BIG).astype(BF16)
            rank_ref[u] = tie_ranks(jnp.minimum(kb + per_trip, n_kb - per_trip + u))
            ties_before = ties_before + rank[KBLK - 1:KBLK, :]
        return ties_before

    lax.fori_loop(0, n_kb // per_trip, bias_body, jnp.zeros((1, tq), F32))

    eye = (lax.broadcasted_iota(I32, (LANES, LANES), 0)
           == lax.broadcasted_iota(I32, (LANES, LANES), 1)).astype(BF16)
    for g in range(n_grp):
        for p in range(N_PAIRS):
            rhs_ref[g, 2 * p * LANES:(2 * p + 1) * LANES, :LANES], rhs_ref[g, (2 * p + 1) * LANES:(2 * p + 2) * LANES, :LANES] = (
                _split_head_pair(qb_ref[0, group(g), p * LANES:(p + 1) * LANES]))
        for h in range(N_HEADS):
            rhs_ref[g, h * LANES:(h + 1) * LANES, LANES:] = eye
    acc_ref[...] = jnp.zeros_like(acc_ref)
    pt_ref[...] = jnp.zeros_like(pt_ref)

    def cols(g):
        return slice(g * hd_cols, (g + 1) * hd_cols)

    def store_masked_scores(kb, g):
        lhs = jnp.concatenate([key_rows(kbd_ref, kb), bias_ref[kb, :, group(g)]], axis=1)
        s_ref[:, cols(g)] = _dot_nt(lhs, rhs_ref[g])

    def accumulate(kb, g, alpha):
        acc_ref[:, cols(g)] = acc_ref[:, cols(g)] * alpha + _dot(ckvt_ref[0, kb], pt_ref[:, cols(g)])

    for g in range(n_grp):
        store_masked_scores(0, g)

    def softmax_block(g, m, l):
        s = s_ref[:, cols(g)]
        m_new = jnp.maximum(m, jnp.max(s, axis=0, keepdims=True))
        alpha = jnp.exp2(m - m_new)
        pt = jnp.exp2(s - m_new)
        pt_ref[:, cols(g)] = pt.astype(BF16)
        return m_new, alpha * l + jnp.sum(pt, axis=0, keepdims=True), alpha

    def attn_body(kb, carry):
        ms, ls, alphas = carry
        prev = jnp.maximum(kb - 1, 0)
        stats = []
        for g in groups:
            accumulate(prev, g, alphas[g])
            stats.append(softmax_block(g, ms[g], ls[g]))
            store_masked_scores(next_block_of(kb, g), g)
        return tuple(zip(*stats))

    init = (tuple(jnp.full((1, hd_cols), NEG_BIG, F32) for _ in groups),
            tuple(jnp.zeros((1, hd_cols), F32) for _ in groups),
            tuple(jnp.ones((1, hd_cols), F32) for _ in groups))
    ms, ls, alphas = lax.fori_loop(0, n_main, attn_body, init)
    ls, alphas = list(ls), list(alphas)
    for g in groups:
        accumulate(n_main - 1, g, alphas[g])
        if g in long_groups:
            _, ls[g], alphas[g] = softmax_block(g, ms[g], ls[g])
            accumulate(n_main, g, alphas[g])
        o_lat = (acc_ref[:, cols(g)] / ls[g]).astype(BF16)
        for h in range(N_HEADS):
            g_ref[h * HEAD_DIM:(h + 1) * HEAD_DIM, group(g)] = _dot(wuvt_ref[h], o_lat[:, h * LANES:(h + 1) * LANES])

    z = zbt_ref[0].astype(F32)
    gated = g_ref[...] * (z / (1.0 + jnp.exp(-z)))
    ms = jnp.mean(gated * gated, axis=0, keepdims=True)
    out_ref[0] = (gated * lax.rsqrt(ms + EPS) * gain_ref[...]).astype(BF16)


def _out_kernel(yat_ref, ybt_ref, x_ref, w_ref, gain_ref, out_ref):
    y = jnp.concatenate([yat_ref[0].astype(F32).T, ybt_ref[0].astype(F32).T], axis=1).astype(BF16)
    h = x_ref[0] + _dot(y, w_ref[...])
    ms = jnp.mean(h * h, axis=-1, keepdims=True)
    out_ref[0] = h * lax.rsqrt(ms + EPS) * gain_ref[...]


N_SPLIT = 3


def _rope_trig(positions):
    inv_freq = ROPE_THETA ** (-jnp.arange(0, ROT_DIM, 2, dtype=F32) / ROT_DIM)
    ang = positions.astype(F32)[..., None] * inv_freq
    rest = jnp.concatenate([jnp.cos(ang), jnp.sin(ang)], axis=-1)
    pieces = []
    for _ in range(N_SPLIT):
        pieces.append(rest.astype(BF16))
        rest = rest - pieces[-1].astype(F32)
    return jnp.concatenate(pieces, axis=-1)


def _rope_spread():
    half = ROT_DIM // 2
    c = np.arange(LANES) % HEAD_DIM
    hit = (np.arange(half)[:, None] == (c % half)[None, :])
    spread = np.zeros((2 * half, 3 * LANES), np.float32)
    spread[:half, :LANES] = hit & (c < ROT_DIM)
    spread[half:, LANES:2 * LANES] = -1.0 * (hit & (c < half))
    spread[half:, 2 * LANES:] = hit & (c >= half) & (c < ROT_DIM)
    ones = np.concatenate([(c >= ROT_DIM), np.zeros(2 * LANES, bool)]).astype(np.float32)[None]
    return np.tile(spread, (N_SPLIT, 1)), ones


def _full(shape):
    return pl.BlockSpec(shape, lambda *_: (0,) * len(shape))


@jax.jit
def kernel(x, positions, norm_in_gain, w_in, kv_norm_gain, w_uk, w_uv, idx_k_norm_gain,
           branch_norm_gain_a, branch_norm_gain_b, w_out, final_norm_gain):
    bsz, seq, d_model = x.shape
    assert w_in.shape[0] == 1
    assert seq % KBLK == 0 and seq <= PATTERNS[-1][0]
    n_blk = seq // KBLK
    topk = min(TOPK_MAX, seq // 4)
    tm = min(TM, seq)
    tqa, tqb = TQ_A, TQ_B
    params = pltpu.CompilerParams(dimension_semantics=("arbitrary", "arbitrary"),
                                  vmem_limit_bytes=VMEM_LIMIT)

    trig = _rope_trig(positions)
    spread, rope_ones = _rope_spread()
    w_padded = jnp.pad(w_in[0], ((0, 0), (0, D_IN_PADDED - w_in.shape[2]))).astype(BF16)
    wuk_dup = jnp.concatenate([w_uk[0], w_uk[0]], axis=1).astype(BF16)
    gik = jnp.concatenate([idx_k_norm_gain[0], jnp.zeros((LANES - IDX_DIM,), F32)])[None]
    wuv_t = jnp.swapaxes(w_uv[0], 1, 2).astype(BF16)
    gain_a = jnp.broadcast_to(branch_norm_gain_a[0][:, None], (WIDTH, tqa))
    gain_b = jnp.broadcast_to(branch_norm_gain_b[0][:, None], (WIDTH, tqb))

    row_blk = lambda c: pl.BlockSpec((1, tm, c), lambda b, i: (b, i, 0))
    col_blk = lambda c: pl.BlockSpec((1, c, tm), lambda b, i: (b, 0, i))
    key_blk = lambda c: pl.BlockSpec((1, tm // KBLK, c, KBLK), lambda b, i: (b, i, 0, 0))
    qa, ka, vat, zat, qb, zbt, iq, kbd, ckvt, ikd, iwt = pl.pallas_call(
        _proj_kernel,
        grid=(bsz, seq // tm),
        in_specs=[row_blk(d_model), _full((1, d_model)), _full((d_model, D_IN_PADDED)),
                  row_blk(N_SPLIT * ROT_DIM), _full(spread.shape), _full(rope_ones.shape),
                  _full((1, KV_LATENT)), _full((KV_LATENT, LANES)), _full((1, LANES))],
        out_specs=[row_blk(WIDTH), row_blk(WIDTH), key_blk(WIDTH), col_blk(WIDTH),
                   row_blk(WIDTH), col_blk(WIDTH), row_blk(2 * LANES), row_blk(LANES),
                   key_blk(KV_LATENT), row_blk(LANES), col_blk(SUBLANES)],
        out_shape=[jax.ShapeDtypeStruct((bsz, seq, WIDTH), BF16),
                   jax.ShapeDtypeStruct((bsz, seq, WIDTH), BF16),
                   jax.ShapeDtypeStruct((bsz, n_blk, WIDTH, KBLK), BF16),
                   jax.ShapeDtypeStruct((bsz, WIDTH, seq), BF16),
                   jax.ShapeDtypeStruct((bsz, seq, WIDTH), BF16),
                   jax.ShapeDtypeStruct((bsz, WIDTH, seq), BF16),
                   jax.ShapeDtypeStruct((bsz, seq, 2 * LANES), BF16),
                   jax.ShapeDtypeStruct((bsz, seq, LANES), BF16),
                   jax.ShapeDtypeStruct((bsz, n_blk, KV_LATENT, KBLK), BF16),
                   jax.ShapeDtypeStruct((bsz, seq, LANES), BF16),
                   jax.ShapeDtypeStruct((bsz, SUBLANES, seq), F32)],
        compiler_params=params, name="proj",
    )(x, norm_in_gain[0][None], w_padded, trig, jnp.asarray(spread, BF16), jnp.asarray(rope_ones),
      kv_norm_gain[0][None], wuk_dup, gik)

    q_row = lambda tq, c: pl.BlockSpec((1, tq, c), lambda b, i: (b, i, 0))
    q_col = lambda tq, c: pl.BlockSpec((1, c, tq), lambda b, i: (b, 0, i))
    per_b3 = lambda s1, s2: pl.BlockSpec((1, s1, s2), lambda b, i: (b, 0, 0))
    per_b4 = lambda s1, s2, s3: pl.BlockSpec((1, s1, s2, s3), lambda b, i: (b, 0, 0, 0))

    bias_np = _mix_a_bias_tables(LANES, KBLK)
    n_grp = tqa // LANES
    yat = pl.pallas_call(
        functools.partial(_mix_a_kernel, bias_np.shape[0]),
        grid=(bsz, seq // tqa),
        in_specs=[q_row(tqa, WIDTH), per_b3(seq, WIDTH), per_b4(n_blk, WIDTH, KBLK), q_col(tqa, WIDTH),
                  _full((WIDTH, tqa)), _full(bias_np.shape)],
        out_specs=q_col(tqa, WIDTH),
        out_shape=jax.ShapeDtypeStruct((bsz, WIDTH, seq), BF16),
        scratch_shapes=[pltpu.VMEM((n_grp, N_HEADS * LANES, LANES), BF16),
                        pltpu.VMEM((n_grp, N_PAIRS, KBLK, 2 * LANES), F32),
                        pltpu.VMEM((n_grp, N_PAIRS, KBLK, 2 * LANES), BF16),
                        pltpu.VMEM((WIDTH, tqa), F32)],
        compiler_params=params, name="mix_a",
    )(qa, ka, vat, zat, gain_a, jnp.asarray(bias_np))

    n_grp = tqb // LANES
    ybt = pl.pallas_call(
        functools.partial(_mix_b_kernel, topk),
        grid=(bsz, seq // tqb),
        in_specs=[q_row(tqb, WIDTH), q_row(tqb, 2 * LANES), q_col(tqb, SUBLANES), q_col(tqb, WIDTH),
                  per_b3(seq, LANES), per_b3(seq, LANES), per_b4(n_blk, KV_LATENT, KBLK),
                  _full((N_HEADS, HEAD_DIM, KV_LATENT)), _full((WIDTH, tqb)), _full((KBLK, KBLK))],
        out_specs=q_col(tqb, WIDTH),
        out_shape=jax.ShapeDtypeStruct((bsz, WIDTH, seq), BF16),
        scratch_shapes=[pltpu.VMEM((n_blk, KBLK, tqb), F32),
                        pltpu.VMEM((n_blk, KBLK, tqb), BF16),
                        pltpu.VMEM((n_grp, IDX_HEADS * LANES, LANES), BF16),
                        pltpu.VMEM((n_grp, N_HEADS * LANES, 2 * LANES), BF16),
                        pltpu.VMEM((KBLK, n_grp * IDX_HEADS * LANES), F32),
                        pltpu.VMEM((2, KBLK, tqb), F32),
                        pltpu.VMEM((KBLK, n_grp * N_HEADS * LANES), F32),
                        pltpu.VMEM((KBLK, n_grp * N_HEADS * LANES), BF16),
                        pltpu.VMEM((KV_LATENT, n_grp * N_HEADS * LANES), F32),
                        pltpu.VMEM((WIDTH, tqb), F32)],
        compiler_params=params, name="mix_b",
    )(qb, iq, iwt, zbt, ikd, kbd, ckvt, wuv_t, gain_b,
      jnp.asarray(np.tril(np.ones((KBLK, KBLK), np.float32)), BF16))

    return pl.pallas_call(
        _out_kernel,
        grid=(bsz, seq // tm),
        in_specs=[col_blk(WIDTH), col_blk(WIDTH), row_blk(d_model),
                  _full((2 * WIDTH, d_model)), _full((1, d_model))],
        out_specs=row_blk(d_model),
        out_shape=jax.ShapeDtypeStruct((bsz, seq, d_model), F32),
        compiler_params=params, name="out_proj",
    )(yat, ybt, x, w_out[0].astype(BF16), final_norm_gain[None])
```

```python
import functools
import math

import numpy as np
import jax
import jax.numpy as jnp
from jax import lax
from jax.experimental import pallas as pl
from jax.experimental.pallas import tpu as pltpu

F32 = jnp.float32
BF16 = jnp.bfloat16
I32 = jnp.int32

HEAD_DIM = 64
N_HEADS = 8
WIDTH = N_HEADS * HEAD_DIM
N_PAIRS = N_HEADS // 2
ROT_DIM = HEAD_DIM // 4
ROPE_THETA = 500000.0
KV_LATENT = 128
IDX_HEADS = 4
IDX_DIM = 64
TOPK_MAX = 256
EPS = 1e-6
PATTERNS = ((128, 1), (512, 4), (2048, 16))

LANES = 128
SUBLANES = 8
MXU_DEPTH = 256
KBLK = MXU_DEPTH
TQ_A = 2 * LANES
TQ_B = 4 * LANES
TM = 1024
NEG_BIG = -1e30
VMEM_LIMIT = 56 * 1024 * 1024
Q_SCALE = HEAD_DIM ** -0.5 * math.log2(math.e)

U_QA, U_KA, U_VA, U_ZA, U_QB, U_CKV, U_ZB, U_IQ, U_LAST, N_UNITS = 0, 4, 8, 12, 16, 20, 21, 25, 27, 28
D_IN_PADDED = N_UNITS * LANES


def _dot(a, b):
    return jnp.dot(a, b, preferred_element_type=F32)


def _dot_nt(a, b):
    return lax.dot_general(a, b, (((1,), (1,)), ((), ())), preferred_element_type=F32)


def _split_head_pair(xp):
    lane = lax.broadcasted_iota(I32, xp.shape, 1)
    zero = jnp.zeros_like(xp)
    return jnp.where(lane < HEAD_DIM, xp, zero), jnp.where(lane >= HEAD_DIM, xp, zero)


def _proj_kernel(x_ref, gin_ref, w_ref, trig_ref, spread_ref, ones_ref, gkv_ref, wuk_ref, gik_ref,
                 qa_ref, ka_ref, vat_ref, zat_ref, qb_ref, zbt_ref, iq_ref, kbd_ref, ckvt_ref,
                 ikd_ref, iwt_ref):
    tm = x_ref.shape[1]
    nblk = tm // KBLK
    x = x_ref[0]
    ms = jnp.mean(x * x, axis=-1, keepdims=True)
    u = (x * lax.rsqrt(ms + EPS) * gin_ref[...]).astype(BF16)
    table = _dot(trig_ref[0], spread_ref[...]) + ones_ref[...]
    cosf, sina, sinb = (table[:, i * LANES:(i + 1) * LANES] for i in range(3))

    def rope(t):
        return t * cosf + pltpu.roll(t, LANES - ROT_DIM // 2, 1) * sina + pltpu.roll(t, ROT_DIM // 2, 1) * sinb

    def proj(unit):
        return _dot_nt(u, w_ref[unit * LANES:(unit + 2) * LANES, :])

    def store_t(ref, unit, t):
        tt = t.T.astype(BF16)
        for i in range(nblk):
            ref[0, i, unit * LANES:(unit + 1) * LANES, :] = tt[:, i * KBLK:(i + 1) * KBLK]

    def lanes(unit):
        return slice(unit * LANES, (unit + 1) * LANES)

    def finish(unit, t):
        if unit < U_KA:
            qa_ref[0, :, lanes(unit - U_QA)] = (rope(t) * Q_SCALE).astype(BF16)
        elif unit < U_VA:
            ka_ref[0, :, lanes(unit - U_KA)] = rope(t).astype(BF16)
        elif unit < U_ZA:
            store_t(vat_ref, unit - U_VA, t)
        elif unit < U_QB:
            zat_ref[0, lanes(unit - U_ZA), :] = t.T.astype(BF16)
        elif unit < U_CKV:
            qb_ref[0, :, lanes(unit - U_QB)] = (rope(t) * Q_SCALE).astype(BF16)
        elif unit == U_CKV:
            ckv = t * lax.rsqrt(jnp.mean(t * t, axis=-1, keepdims=True) + EPS) * gkv_ref[...]
            store_t(ckvt_ref, 0, ckv)
            kbd_ref[0] = rope(_dot(ckv.astype(BF16), wuk_ref[...])).astype(BF16)
        elif unit < U_IQ:
            zbt_ref[0, lanes(unit - U_ZB), :] = t.T.astype(BF16)
        elif unit < U_LAST:
            iq_ref[0, :, lanes(unit - U_IQ)] = rope(t).astype(BF16)
        else:
            lane = lax.broadcasted_iota(I32, t.shape, 1)
            ik_ms = jnp.sum(jnp.where(lane < IDX_DIM, t * t, 0.0), axis=-1, keepdims=True) / IDX_DIM
            ik = rope(t * lax.rsqrt(ik_ms + EPS) * gik_ref[...])
            ikd_ref[0] = (ik + pltpu.roll(ik, IDX_DIM, 1)).astype(BF16)
            iwt_ref[0] = t.T[IDX_DIM:IDX_DIM + SUBLANES, :] * (IDX_HEADS ** -0.5 * IDX_DIM ** -0.5)

    first = [U_LAST - 1, U_CKV]
    for unit in first + [u for u in range(0, N_UNITS, 2) if u not in first]:
        acc = proj(unit)
        finish(unit, acc[:, :LANES])
        finish(unit + 1, acc[:, LANES:])


def _mix_a_bias_tables(tq, tk):
    assert tq == LANES
    o_last = PATTERNS[1][0] + tk - LANES
    offsets = list(range(0, o_last + 1, LANES)) + [o_last + LANES]
    qi = np.arange(tq)[None, :]
    kj = np.arange(tk)[:, None]
    tables = []
    for o in offsets:
        d = o + qi - kj
        mult = np.zeros(d.shape, np.int64)
        for window, dil in PATTERNS:
            mult += ((d >= 0) & (d % dil == 0) & (d <= window)).astype(np.int64)
        tables.append(np.where(mult > 0, np.log2(np.maximum(mult, 1)), NEG_BIG))
    return np.stack(tables).astype(np.float32)


def _mix_a_kernel(n_tables, qa_ref, ka_ref, vat_ref, zat_ref, gain_ref, bias_ref,
                  out_ref, qm_ref, s_ref, pt_ref, acc_ref):
    tq = qa_ref.shape[1]
    n_grp = tq // LANES
    j = pl.program_id(1)
    n_kb = (j * tq + tq - 1) // KBLK + 1
    chains = [(g, p) for g in range(n_grp) for p in range(N_PAIRS)]

    def group(g):
        return slice(g * LANES, (g + 1) * LANES)

    for g, p in chains:
        qm_ref[g, 2 * p * LANES:(2 * p + 1) * LANES, :], qm_ref[g, (2 * p + 1) * LANES:(2 * p + 2) * LANES, :] = (
            _split_head_pair(qa_ref[0, group(g), p * LANES:(p + 1) * LANES]))
    acc_ref[...] = jnp.zeros_like(acc_ref)
    pt_ref[...] = jnp.zeros_like(pt_ref)

    def scores(kb, g, p):
        k0 = pl.multiple_of(kb * KBLK, KBLK)
        return _dot_nt(ka_ref[0, pl.ds(k0, KBLK), p * LANES:(p + 1) * LANES],
                       qm_ref[g, 2 * p * LANES:(2 * p + 2) * LANES, :])

    def accumulate(kb, g, p, alpha):
        vt = vat_ref[0, kb, p * LANES:(p + 1) * LANES, :]
        r0 = p * LANES
        acc_ref[r0:r0 + HEAD_DIM, group(g)] = (acc_ref[r0:r0 + HEAD_DIM, group(g)] * alpha[:, :LANES]
                                               + _dot(vt[:HEAD_DIM], pt_ref[g, p, :, :LANES]))
        acc_ref[r0 + HEAD_DIM:r0 + LANES, group(g)] = (acc_ref[r0 + HEAD_DIM:r0 + LANES, group(g)] * alpha[:, LANES:]
                                                       + _dot(vt[HEAD_DIM:], pt_ref[g, p, :, LANES:]))

    for g, p in chains:
        s_ref[g, p] = scores(0, g, p)

    def body(kb, carry):
        ms, ls, alphas = carry
        prev = jnp.maximum(kb - 1, 0)
        nxt = jnp.minimum(kb + 1, n_kb - 1)
        new_ms, new_ls, new_alphas = [], [], []
        for c, (g, p) in enumerate(chains):
            bias = bias_ref[jnp.minimum((j * tq + g * LANES - kb * KBLK) // LANES, n_tables - 1)]
            accumulate(prev, g, p, alphas[c])
            s = s_ref[g, p] + jnp.concatenate([bias, bias], axis=1)
            m_new = jnp.maximum(ms[c], jnp.max(s, axis=0, keepdims=True))
            alpha = jnp.exp2(ms[c] - m_new)
            pt = jnp.exp2(s - m_new)
            new_ms.append(m_new)
            new_ls.append(alpha * ls[c] + jnp.sum(pt, axis=0, keepdims=True))
            new_alphas.append(alpha)
            pt_ref[g, p] = pt.astype(BF16)
            s_ref[g, p] = scores(nxt, g, p)
        return tuple(new_ms), tuple(new_ls), tuple(new_alphas)

    init = (tuple(jnp.full((1, 2 * LANES), NEG_BIG, F32) for _ in chains),
            tuple(jnp.zeros((1, 2 * LANES), F32) for _ in chains),
            tuple(jnp.ones((1, 2 * LANES), F32) for _ in chains))
    _, ls, alphas = lax.fori_loop(0, n_kb, body, init)
    for c, (g, p) in enumerate(chains):
        accumulate(n_kb - 1, g, p, alphas[c])
        r0 = p * LANES
        acc_ref[r0:r0 + HEAD_DIM, group(g)] = acc_ref[r0:r0 + HEAD_DIM, group(g)] / ls[c][:, :LANES]
        acc_ref[r0 + HEAD_DIM:r0 + LANES, group(g)] = acc_ref[r0 + HEAD_DIM:r0 + LANES, group(g)] / ls[c][:, LANES:]

    z = zat_ref[0].astype(F32)
    g = acc_ref[...] * (z / (1.0 + jnp.exp(-z)))
    ms = jnp.mean(g * g, axis=0, keepdims=True)
    out_ref[0] = (g * lax.rsqrt(ms + EPS) * gain_ref[...]).astype(BF16)


def _mix_b_kernel(topk, qb_ref, iq_ref, iwt_ref, zbt_ref, ikd_ref, kbd_ref, ckvt_ref, wuvt_ref,
                  gain_ref, tri_ref, out_ref, sc_ref, bias_ref, iqm_ref, rhs_ref, raw_ref, rank_ref,
                  s_ref, pt_ref, acc_ref, g_ref):
    tq = qb_ref.shape[1]
    n_grp = tq // LANES
    iw_cols = IDX_HEADS * LANES
    hd_cols = N_HEADS * LANES
    j = pl.program_id(1)
    groups = list(range(n_grp))
    long_groups = [g for g in groups if (g * LANES + LANES - 1) // KBLK > 0]
    short_groups = [g for g in groups if g not in long_groups]
    assert tq <= 2 * KBLK and (not long_groups or tq % KBLK == 0)
    long_lanes = slice(len(short_groups) * LANES, tq)
    n_main = (j * tq + LANES - 1) // KBLK + 1
    n_kb = n_main + (1 if long_groups else 0)
    row = lax.broadcasted_iota(I32, (KBLK, tq), 0)
    t_pos = j * tq + lax.broadcasted_iota(I32, (KBLK, tq), 1)

    def key_rows(ref, kb):
        return ref[0, pl.ds(pl.multiple_of(kb * KBLK, KBLK), KBLK), :]

    def next_block(kb):
        return jnp.minimum(kb + 1, n_kb - 1)

    def next_block_of(kb, g):
        return jnp.minimum(kb + 1, (n_kb if g in long_groups else n_main) - 1)

    def group(g):
        return slice(g * LANES, (g + 1) * LANES)


    for g in range(n_grp):
        for p in range(IDX_HEADS // 2):
            iqm_ref[g, 2 * p * LANES:(2 * p + 1) * LANES, :], iqm_ref[g, (2 * p + 1) * LANES:(2 * p + 2) * LANES, :] = (
                _split_head_pair(iq_ref[0, group(g), p * LANES:(p + 1) * LANES]))

    def store_raw_scores(kb, g):
        raw_ref[:, g * iw_cols:(g + 1) * iw_cols] = _dot_nt(key_rows(ikd_ref, kb), iqm_ref[g])

    def finish_scores(kb, g):
        c0 = g * iw_cols
        sc = jnp.maximum(raw_ref[:, c0:c0 + LANES], 0.0) * iwt_ref[0, 0:1, group(g)]
        for h in range(1, IDX_HEADS):
            sc = sc + (jnp.maximum(raw_ref[:, c0 + h * LANES:c0 + (h + 1) * LANES], 0.0)
                       * iwt_ref[0, h:h + 1, group(g)])
        causal = (kb * KBLK + lax.broadcasted_iota(I32, (KBLK, LANES), 0)
                  <= j * tq + g * LANES + lax.broadcasted_iota(I32, (KBLK, LANES), 1))
        sc_ref[kb, :, group(g)] = jnp.where(causal, sc, -jnp.inf)

    for g in groups:
        store_raw_scores(0, g)

    def score_body(kb, carry):
        for g in groups:
            finish_scores(kb, g)
            store_raw_scores(next_block_of(kb, g), g)
        return carry

    lax.fori_loop(0, n_main, score_body, 0)
    if long_groups:
        for g in long_groups:
            finish_scores(n_main, g)
        sc_ref[n_main, :, :long_lanes.start] = jnp.full((KBLK, long_lanes.start), -jnp.inf, F32)

    def count(pred):
        def partial(hit):
            return jnp.sum(hit.astype(I32).reshape(KBLK // SUBLANES, SUBLANES, hit.shape[1]), axis=0)
        acc = lax.fori_loop(0, n_main, lambda kb, acc: acc + partial(pred(sc_ref[kb], slice(None))),
                            jnp.zeros((SUBLANES, tq), I32))
        if long_groups:
            tail = partial(pred(sc_ref[n_main, :, long_lanes], long_lanes))
            acc = jnp.concatenate([acc[:, :long_lanes.start], acc[:, long_lanes] + tail], axis=1)
        return jnp.sum(acc, axis=0, keepdims=True)

    def key_to_float(key):
        bits = key ^ ((key >> 31) & jnp.int32(0x7FFFFFFF))
        return lax.bitcast_convert_type(bits, F32)

    def value_step(i, prefix):
        cand = prefix + (jnp.int32(1) << (31 - i))
        thr = key_to_float(cand)
        return jnp.where(count(lambda s, lanes: s >= thr[:, lanes]) >= topk, cand, prefix)

    prefix = lax.fori_loop(0, 32, value_step, jnp.full((1, tq), -2 ** 31, I32))
    thr = key_to_float(prefix)
    need = (topk - count(lambda s, lanes: s > thr[:, lanes])).astype(F32)

    def tie_ranks(kb):
        tie = jnp.where(sc_ref[kb] == thr, 1.0, 0.0).astype(BF16)
        return _dot(tri_ref[...], tie)

    per_trip = 2 if long_groups else 1
    for u in range(per_trip):
        rank_ref[u] = tie_ranks(u)

    def bias_body(trip, ties_before):
        for u in range(per_trip):
            kb = trip * per_trip + u
            sc = sc_ref[kb]
            rank = rank_ref[u]
            take = (sc == thr) & (rank + ties_before <= need)
            sel = (sc > thr) | take | (t_pos < topk)
            bias_ref[kb] = jnp.where(sel & (kb * KBLK + row <= t_pos), 0.0, NEG_BIG).astype(BF16)
            rank_ref[u] = tie_ranks(jnp.minimum(kb + per_trip, n_kb - per_trip + u))
            ties_before = ties_before + rank[KBLK - 1:KBLK, :]
        return ties_before

    lax.fori_loop(0, n_kb // per_trip, bias_body, jnp.zeros((1, tq), F32))

    eye = (lax.broadcasted_iota(I32, (LANES, LANES), 0)
           == lax.broadcasted_iota(I32, (LANES, LANES), 1)).astype(BF16)
    for g in range(n_grp):
        for p in range(N_PAIRS):
            rhs_ref[g, 2 * p * LANES:(2 * p + 1) * LANES, :LANES], rhs_ref[g, (2 * p + 1) * LANES:(2 * p + 2) * LANES, :LANES] = (
                _split_head_pair(qb_ref[0, group(g), p * LANES:(p + 1) * LANES]))
        for h in range(N_HEADS):
            rhs_ref[g, h * LANES:(h + 1) * LANES, LANES:] = eye
    acc_ref[...] = jnp.zeros_like(acc_ref)
    pt_ref[...] = jnp.zeros_like(pt_ref)

    def cols(g):
        return slice(g * hd_cols, (g + 1) * hd_cols)

    def store_masked_scores(kb, g):
        lhs = jnp.concatenate([key_rows(kbd_ref, kb), bias_ref[kb, :, group(g)]], axis=1)
        s_ref[:, cols(g)] = _dot_nt(lhs, rhs_ref[g])

    def accumulate(kb, g, alpha):
        acc_ref[:, cols(g)] = acc_ref[:, cols(g)] * alpha + _dot(ckvt_ref[0, kb], pt_ref[:, cols(g)])

    for g in range(n_grp):
        store_masked_scores(0, g)

    def softmax_block(g, m, l):
        s = s_ref[:, cols(g)]
        m_new = jnp.maximum(m, jnp.max(s, axis=0, keepdims=True))
        alpha = jnp.exp2(m - m_new)
        pt = jnp.exp2(s - m_new)
        pt_ref[:, cols(g)] = pt.astype(BF16)
        return m_new, alpha * l + jnp.sum(pt, axis=0, keepdims=True), alpha

    def attn_body(kb, carry):
        ms, ls, alphas = carry
        prev = jnp.maximum(kb - 1, 0)
        stats = []
        for g in groups:
            accumulate(prev, g, alphas[g])
            stats.append(softmax_block(g, ms[g], ls[g]))
            store_masked_scores(next_block_of(kb, g), g)
        return tuple(zip(*stats))

    init = (tuple(jnp.full((1, hd_cols), NEG_BIG, F32) for _ in groups),
            tuple(jnp.zeros((1, hd_cols), F32) for _ in groups),
            tuple(jnp.ones((1, hd_cols), F32) for _ in groups))
    ms, ls, alphas = lax.fori_loop(0, n_main, attn_body, init)
    ls, alphas = list(ls), list(alphas)
    for g in groups:
        accumulate(n_main - 1, g, alphas[g])
        if g in long_groups:
            _, ls[g], alphas[g] = softmax_block(g, ms[g], ls[g])
            accumulate(n_main, g, alphas[g])
        o_lat = (acc_ref[:, cols(g)] / ls[g]).astype(BF16)
        for h in range(N_HEADS):
            g_ref[h * HEAD_DIM:(h + 1) * HEAD_DIM, group(g)] = _dot(wuvt_ref[h], o_lat[:, h * LANES:(h + 1) * LANES])

    z = zbt_ref[0].astype(F32)
    gated = g_ref[...] * (z / (1.0 + jnp.exp(-z)))
    ms = jnp.mean(gated * gated, axis=0, keepdims=True)
    out_ref[0] = (gated * lax.rsqrt(ms + EPS) * gain_ref[...]).astype(BF16)


def _out_kernel(yat_ref, ybt_ref, x_ref, w_ref, gain_ref, out_ref):
    y = jnp.concatenate([yat_ref[0].astype(F32).T, ybt_ref[0].astype(F32).T], axis=1).astype(BF16)
    h = x_ref[0] + _dot(y, w_ref[...])
    ms = jnp.mean(h * h, axis=-1, keepdims=True)
    out_ref[0] = h * lax.rsqrt(ms + EPS) * gain_ref[...]


N_SPLIT = 3


def _rope_trig(positions):
    inv_freq = ROPE_THETA ** (-jnp.arange(0, ROT_DIM, 2, dtype=F32) / ROT_DIM)
    ang = positions.astype(F32)[..., None] * inv_freq
    rest = jnp.concatenate([jnp.cos(ang), jnp.sin(ang)], axis=-1)
    pieces = []
    for _ in range(N_SPLIT):
        pieces.append(rest.astype(BF16))
        rest = rest - pieces[-1].astype(F32)
    return jnp.concatenate(pieces, axis=-1)


def _rope_spread():
    half = ROT_DIM // 2
    c = np.arange(LANES) % HEAD_DIM
    hit = (np.arange(half)[:, None] == (c % half)[None, :])
    spread = np.zeros((2 * half, 3 * LANES), np.float32)
    spread[:half, :LANES] = hit & (c < ROT_DIM)
    spread[half:, LANES:2 * LANES] = -1.0 * (hit & (c < half))
    spread[half:, 2 * LANES:] = hit & (c >= half) & (c < ROT_DIM)
    ones = np.concatenate([(c >= ROT_DIM), np.zeros(2 * LANES, bool)]).astype(np.float32)[None]
    return np.tile(spread, (N_SPLIT, 1)), ones


def _full(shape):
    return pl.BlockSpec(shape, lambda *_: (0,) * len(shape))


@jax.jit
def kernel(x, positions, norm_in_gain, w_in, kv_norm_gain, w_uk, w_uv, idx_k_norm_gain,
           branch_norm_gain_a, branch_norm_gain_b, w_out, final_norm_gain):
    bsz, seq, d_model = x.shape
    assert w_in.shape[0] == 1
    assert seq % KBLK == 0 and seq <= PATTERNS[-1][0]
    n_blk = seq // KBLK
    topk = min(TOPK_MAX, seq // 4)
    tm = min(TM, seq)
    tqa, tqb = TQ_A, TQ_B
    params = pltpu.CompilerParams(dimension_semantics=("arbitrary", "arbitrary"),
                                  vmem_limit_bytes=VMEM_LIMIT)

    trig = _rope_trig(positions)
    spread, rope_ones = _rope_spread()
    w_padded = jnp.pad(w_in[0].T, ((0, D_IN_PADDED - w_in.shape[2]), (0, 0))).astype(BF16)
    wuk_dup = jnp.concatenate([w_uk[0], w_uk[0]], axis=1).astype(BF16)
    gik = jnp.concatenate([idx_k_norm_gain[0], jnp.zeros((LANES - IDX_DIM,), F32)])[None]
    wuv_t = jnp.swapaxes(w_uv[0], 1, 2).astype(BF16)
    gain_a = jnp.broadcast_to(branch_norm_gain_a[0][:, None], (WIDTH, tqa))
    gain_b = jnp.broadcast_to(branch_norm_gain_b[0][:, None], (WIDTH, tqb))

    row_blk = lambda c: pl.BlockSpec((1, tm, c), lambda b, i: (b, i, 0))
    col_blk = lambda c: pl.BlockSpec((1, c, tm), lambda b, i: (b, 0, i))
    key_blk = lambda c: pl.BlockSpec((1, tm // KBLK, c, KBLK), lambda b, i: (b, i, 0, 0))
    qa, ka, vat, zat, qb, zbt, iq, kbd, ckvt, ikd, iwt = pl.pallas_call(
        _proj_kernel,
        grid=(bsz, seq // tm),
        in_specs=[row_blk(d_model), _full((1, d_model)), _full((D_IN_PADDED, d_model)),
                  row_blk(N_SPLIT * ROT_DIM), _full(spread.shape), _full(rope_ones.shape),
                  _full((1, KV_LATENT)), _full((KV_LATENT, LANES)), _full((1, LANES))],
        out_specs=[row_blk(WIDTH), row_blk(WIDTH), key_blk(WIDTH), col_blk(WIDTH),
                   row_blk(WIDTH), col_blk(WIDTH), row_blk(2 * LANES), row_blk(LANES),
                   key_blk(KV_LATENT), row_blk(LANES), col_blk(SUBLANES)],
        out_shape=[jax.ShapeDtypeStruct((bsz, seq, WIDTH), BF16),
                   jax.ShapeDtypeStruct((bsz, seq, WIDTH), BF16),
                   jax.ShapeDtypeStruct((bsz, n_blk, WIDTH, KBLK), BF16),
                   jax.ShapeDtypeStruct((bsz, WIDTH, seq), BF16),
                   jax.ShapeDtypeStruct((bsz, seq, WIDTH), BF16),
                   jax.ShapeDtypeStruct((bsz, WIDTH, seq), BF16),
                   jax.ShapeDtypeStruct((bsz, seq, 2 * LANES), BF16),
                   jax.ShapeDtypeStruct((bsz, seq, LANES), BF16),
                   jax.ShapeDtypeStruct((bsz, n_blk, KV_LATENT, KBLK), BF16),
                   jax.ShapeDtypeStruct((bsz, seq, LANES), BF16),
                   jax.ShapeDtypeStruct((bsz, SUBLANES, seq), F32)],
        compiler_params=params, name="proj",
    )(x, norm_in_gain[0][None], w_padded, trig, jnp.asarray(spread, BF16), jnp.asarray(rope_ones),
      kv_norm_gain[0][None], wuk_dup, gik)

    q_row = lambda tq, c: pl.BlockSpec((1, tq, c), lambda b, i: (b, i, 0))
    q_col = lambda tq, c: pl.BlockSpec((1, c, tq), lambda b, i: (b, 0, i))
    per_b3 = lambda s1, s2: pl.BlockSpec((1, s1, s2), lambda b, i: (b, 0, 0))
    per_b4 = lambda s1, s2, s3: pl.BlockSpec((1, s1, s2, s3), lambda b, i: (b, 0, 0, 0))

    bias_np = _mix_a_bias_tables(LANES, KBLK)
    n_grp = tqa // LANES
    yat = pl.pallas_call(
        functools.partial(_mix_a_kernel, bias_np.shape[0]),
        grid=(bsz, seq // tqa),
        in_specs=[q_row(tqa, WIDTH), per_b3(seq, WIDTH), per_b4(n_blk, WIDTH, KBLK), q_col(tqa, WIDTH),
                  _full((WIDTH, tqa)), _full(bias_np.shape)],
        out_specs=q_col(tqa, WIDTH),
        out_shape=jax.ShapeDtypeStruct((bsz, WIDTH, seq), BF16),
        scratch_shapes=[pltpu.VMEM((n_grp, N_HEADS * LANES, LANES), BF16),
                        pltpu.VMEM((n_grp, N_PAIRS, KBLK, 2 * LANES), F32),
                        pltpu.VMEM((n_grp, N_PAIRS, KBLK, 2 * LANES), BF16),
                        pltpu.VMEM((WIDTH, tqa), F32)],
        compiler_params=params, name="mix_a",
    )(qa, ka, vat, zat, gain_a, jnp.asarray(bias_np))

    n_grp = tqb // LANES
    ybt = pl.pallas_call(
        functools.partial(_mix_b_kernel, topk),
        grid=(bsz, seq // tqb),
        in_specs=[q_row(tqb, WIDTH), q_row(tqb, 2 * LANES), q_col(tqb, SUBLANES), q_col(tqb, WIDTH),
                  per_b3(seq, LANES), per_b3(seq, LANES), per_b4(n_blk, KV_LATENT, KBLK),
                  _full((N_HEADS, HEAD_DIM, KV_LATENT)), _full((WIDTH, tqb)), _full((KBLK, KBLK))],
        out_specs=q_col(tqb, WIDTH),
        out_shape=jax.ShapeDtypeStruct((bsz, WIDTH, seq), BF16),
        scratch_shapes=[pltpu.VMEM((n_blk, KBLK, tqb), F32),
                        pltpu.VMEM((n_blk, KBLK, tqb), BF16),
                        pltpu.VMEM((n_grp, IDX_HEADS * LANES, LANES), BF16),
                        pltpu.VMEM((n_grp, N_HEADS * LANES, 2 * LANES), BF16),
                        pltpu.VMEM((KBLK, n_grp * IDX_HEADS * LANES), F32),
                        pltpu.VMEM((2, KBLK, tqb), F32),
                        pltpu.VMEM((KBLK, n_grp * N_HEADS * LANES), F32),
                        pltpu.VMEM((KBLK, n_grp * N_HEADS * LANES), BF16),
                        pltpu.VMEM((KV_LATENT, n_grp * N_HEADS * LANES), F32),
                        pltpu.VMEM((WIDTH, tqb), F32)],
        compiler_params=params, name="mix_b",
    )(qb, iq, iwt, zbt, ikd, kbd, ckvt, wuv_t, gain_b,
      jnp.asarray(np.tril(np.ones((KBLK, KBLK), np.float32)), BF16))

    return pl.pallas_call(
        _out_kernel,
        grid=(bsz, seq // tm),
        in_specs=[col_blk(WIDTH), col_blk(WIDTH), row_blk(d_model),
                  _full((2 * WIDTH, d_model)), _full((1, d_model))],
        out_specs=row_blk(d_model),
        out_shape=jax.ShapeDtypeStruct((bsz, seq, d_model), F32),
        compiler_params=params, name="out_proj",
    )(yat, ybt, x, w_out[0].astype(BF16), final_norm_gain[None])
```

```python
import functools
import math

import numpy as np
import jax
import jax.numpy as jnp
from jax import lax
from jax.experimental import pallas as pl
from jax.experimental.pallas import tpu as pltpu

F32 = jnp.float32
BF16 = jnp.bfloat16
I32 = jnp.int32

HEAD_DIM = 64
N_HEADS = 8
WIDTH = N_HEADS * HEAD_DIM
N_PAIRS = N_HEADS // 2
ROT_DIM = HEAD_DIM // 4
ROPE_THETA = 500000.0
KV_LATENT = 128
IDX_HEADS = 4
IDX_DIM = 64
TOPK_MAX = 256
EPS = 1e-6
PATTERNS = ((128, 1), (512, 4), (2048, 16))

LANES = 128
SUBLANES = 8
MXU_DEPTH = 256
KBLK = MXU_DEPTH
TQ_A = 4 * LANES
TQ_B = 4 * LANES
TM = 1024
NEG_BIG = -1e30
VMEM_LIMIT = 56 * 1024 * 1024
Q_SCALE = HEAD_DIM ** -0.5 * math.log2(math.e)

U_QA, U_KA, U_VA, U_ZA, U_QB, U_CKV, U_ZB, U_IQ, U_LAST, N_UNITS = 0, 4, 8, 12, 16, 20, 21, 25, 27, 28
D_IN_PADDED = N_UNITS * LANES


def _dot(a, b):
    return jnp.dot(a, b, preferred_element_type=F32)


def _dot_nt(a, b):
    return lax.dot_general(a, b, (((1,), (1,)), ((), ())), preferred_element_type=F32)


def _split_head_pair(xp):
    lane = lax.broadcasted_iota(I32, xp.shape, 1)
    zero = jnp.zeros_like(xp)
    return jnp.where(lane < HEAD_DIM, xp, zero), jnp.where(lane >= HEAD_DIM, xp, zero)


def _proj_kernel(x_ref, gin_ref, w_ref, trig_ref, spread_ref, ones_ref, gkv_ref, wuk_ref, gik_ref,
                 qa_ref, ka_ref, vat_ref, zat_ref, qb_ref, zbt_ref, iq_ref, kbd_ref, ckvt_ref,
                 ikd_ref, iwt_ref):
    tm = x_ref.shape[1]
    nblk = tm // KBLK
    x = x_ref[0]
    ms = jnp.mean(x * x, axis=-1, keepdims=True)
    u = (x * lax.rsqrt(ms + EPS) * gin_ref[...]).astype(BF16)
    table = _dot(trig_ref[0], spread_ref[...]) + ones_ref[...]
    cosf, sina, sinb = (table[:, i * LANES:(i + 1) * LANES] for i in range(3))

    def rope(t):
        return t * cosf + pltpu.roll(t, LANES - ROT_DIM // 2, 1) * sina + pltpu.roll(t, ROT_DIM // 2, 1) * sinb

    def proj(unit):
        return _dot_nt(u, w_ref[unit * LANES:(unit + 2) * LANES, :])

    def store_t(ref, unit, t):
        tt = t.T.astype(BF16)
        for i in range(nblk):
            ref[0, i, unit * LANES:(unit + 1) * LANES, :] = tt[:, i * KBLK:(i + 1) * KBLK]

    def lanes(unit):
        return slice(unit * LANES, (unit + 1) * LANES)

    def finish(unit, t):
        if unit < U_KA:
            qa_ref[0, :, lanes(unit - U_QA)] = (rope(t) * Q_SCALE).astype(BF16)
        elif unit < U_VA:
            ka_ref[0, :, lanes(unit - U_KA)] = rope(t).astype(BF16)
        elif unit < U_ZA:
            store_t(vat_ref, unit - U_VA, t)
        elif unit < U_QB:
            zat_ref[0, lanes(unit - U_ZA), :] = t.T.astype(BF16)
        elif unit < U_CKV:
            qb_ref[0, :, lanes(unit - U_QB)] = (rope(t) * Q_SCALE).astype(BF16)
        elif unit == U_CKV:
            ckv = t * lax.rsqrt(jnp.mean(t * t, axis=-1, keepdims=True) + EPS) * gkv_ref[...]
            store_t(ckvt_ref, 0, ckv)
            kbd_ref[0] = rope(_dot(ckv.astype(BF16), wuk_ref[...])).astype(BF16)
        elif unit < U_IQ:
            zbt_ref[0, lanes(unit - U_ZB), :] = t.T.astype(BF16)
        elif unit < U_LAST:
            iq_ref[0, :, lanes(unit - U_IQ)] = rope(t).astype(BF16)
        else:
            lane = lax.broadcasted_iota(I32, t.shape, 1)
            ik_ms = jnp.sum(jnp.where(lane < IDX_DIM, t * t, 0.0), axis=-1, keepdims=True) / IDX_DIM
            ik = rope(t * lax.rsqrt(ik_ms + EPS) * gik_ref[...])
            ikd_ref[0] = (ik + pltpu.roll(ik, IDX_DIM, 1)).astype(BF16)
            iwt_ref[0] = t.T[IDX_DIM:IDX_DIM + SUBLANES, :] * (IDX_HEADS ** -0.5 * IDX_DIM ** -0.5)

    first = [U_LAST - 1, U_CKV]
    for unit in first + [u for u in range(0, N_UNITS, 2) if u not in first]:
        acc = proj(unit)
        finish(unit, acc[:, :LANES])
        finish(unit + 1, acc[:, LANES:])


def _mix_a_bias_tables(tq, tk):
    assert tq == LANES
    o_last = PATTERNS[1][0] + tk - LANES
    offsets = list(range(0, o_last + 1, LANES)) + [o_last + LANES]
    qi = np.arange(tq)[None, :]
    kj = np.arange(tk)[:, None]
    tables = []
    for o in offsets:
        d = o + qi - kj
        mult = np.zeros(d.shape, np.int64)
        for window, dil in PATTERNS:
            mult += ((d >= 0) & (d % dil == 0) & (d <= window)).astype(np.int64)
        tables.append(np.where(mult > 0, np.log2(np.maximum(mult, 1)), NEG_BIG))
    return np.stack(tables).astype(np.float32)


def _mix_a_kernel(n_tables, qa_ref, ka_ref, vat_ref, zat_ref, gain_ref, bias_ref,
                  out_ref, qm_ref, s_ref, pt_ref, acc_ref):
    tq = qa_ref.shape[1]
    n_grp = tq // LANES
    j = pl.program_id(1)
    long_groups = [g for g in range(n_grp) if (g * LANES + LANES - 1) // KBLK > 0]
    assert tq <= 2 * KBLK and (not long_groups or tq % KBLK == 0)
    n_main = (j * tq + LANES - 1) // KBLK + 1
    chains = [(g, p) for g in range(n_grp) for p in range(N_PAIRS)]

    def group(g):
        return slice(g * LANES, (g + 1) * LANES)

    def last_block(g):
        return n_main if g in long_groups else n_main - 1

    for g, p in chains:
        qm_ref[g, 2 * p * LANES:(2 * p + 1) * LANES, :], qm_ref[g, (2 * p + 1) * LANES:(2 * p + 2) * LANES, :] = (
            _split_head_pair(qa_ref[0, group(g), p * LANES:(p + 1) * LANES]))
    acc_ref[...] = jnp.zeros_like(acc_ref)
    pt_ref[...] = jnp.zeros_like(pt_ref)

    def scores(kb, g, p):
        k0 = pl.multiple_of(kb * KBLK, KBLK)
        return _dot_nt(ka_ref[0, pl.ds(k0, KBLK), p * LANES:(p + 1) * LANES],
                       qm_ref[g, 2 * p * LANES:(2 * p + 2) * LANES, :])

    def accumulate(kb, g, p, alpha):
        vt = vat_ref[0, kb, p * LANES:(p + 1) * LANES, :]
        r0 = p * LANES
        acc_ref[r0:r0 + HEAD_DIM, group(g)] = (acc_ref[r0:r0 + HEAD_DIM, group(g)] * alpha[:, :LANES]
                                               + _dot(vt[:HEAD_DIM], pt_ref[g, p, :, :LANES]))
        acc_ref[r0 + HEAD_DIM:r0 + LANES, group(g)] = (acc_ref[r0 + HEAD_DIM:r0 + LANES, group(g)] * alpha[:, LANES:]
                                                       + _dot(vt[HEAD_DIM:], pt_ref[g, p, :, LANES:]))

    for g, p in chains:
        s_ref[g, p] = scores(0, g, p)

    def softmax_block(kb, g, p, m, l):
        bias = bias_ref[jnp.minimum((j * tq + g * LANES - kb * KBLK) // LANES, n_tables - 1)]
        s = s_ref[g, p] + jnp.concatenate([bias, bias], axis=1)
        m_new = jnp.maximum(m, jnp.max(s, axis=0, keepdims=True))
        alpha = jnp.exp2(m - m_new)
        pt = jnp.exp2(s - m_new)
        pt_ref[g, p] = pt.astype(BF16)
        return m_new, alpha * l + jnp.sum(pt, axis=0, keepdims=True), alpha

    def body(kb, carry):
        ms, ls, alphas = carry
        prev = jnp.maximum(kb - 1, 0)
        stats = []
        for c, (g, p) in enumerate(chains):
            accumulate(prev, g, p, alphas[c])
            stats.append(softmax_block(kb, g, p, ms[c], ls[c]))
            s_ref[g, p] = scores(jnp.minimum(kb + 1, last_block(g)), g, p)
        return tuple(zip(*stats))

    init = (tuple(jnp.full((1, 2 * LANES), NEG_BIG, F32) for _ in chains),
            tuple(jnp.zeros((1, 2 * LANES), F32) for _ in chains),
            tuple(jnp.ones((1, 2 * LANES), F32) for _ in chains))
    ms, ls, alphas = lax.fori_loop(0, n_main, body, init)
    ls, alphas = list(ls), list(alphas)
    for c, (g, p) in enumerate(chains):
        accumulate(n_main - 1, g, p, alphas[c])
        if g in long_groups:
            _, ls[c], alphas[c] = softmax_block(n_main, g, p, ms[c], ls[c])
            accumulate(n_main, g, p, alphas[c])
        r0 = p * LANES
        acc_ref[r0:r0 + HEAD_DIM, group(g)] = acc_ref[r0:r0 + HEAD_DIM, group(g)] / ls[c][:, :LANES]
        acc_ref[r0 + HEAD_DIM:r0 + LANES, group(g)] = acc_ref[r0 + HEAD_DIM:r0 + LANES, group(g)] / ls[c][:, LANES:]

    z = zat_ref[0].astype(F32)
    g = acc_ref[...] * (z / (1.0 + jnp.exp(-z)))
    ms = jnp.mean(g * g, axis=0, keepdims=True)
    out_ref[0] = (g * lax.rsqrt(ms + EPS) * gain_ref[...]).astype(BF16)


def _mix_b_kernel(topk, qb_ref, iq_ref, iwt_ref, zbt_ref, ikd_ref, kbd_ref, ckvt_ref, wuvt_ref,
                  gain_ref, tri_ref, out_ref, sc_ref, bias_ref, iqm_ref, rhs_ref, raw_ref, rank_ref,
                  s_ref, pt_ref, acc_ref, g_ref):
    tq = qb_ref.shape[1]
    n_grp = tq // LANES
    iw_cols = IDX_HEADS * LANES
    hd_cols = N_HEADS * LANES
    j = pl.program_id(1)
    groups = list(range(n_grp))
    long_groups = [g for g in groups if (g * LANES + LANES - 1) // KBLK > 0]
    short_groups = [g for g in groups if g not in long_groups]
    assert tq <= 2 * KBLK and (not long_groups or tq % KBLK == 0)
    long_lanes = slice(len(short_groups) * LANES, tq)
    n_main = (j * tq + LANES - 1) // KBLK + 1
    n_kb = n_main + (1 if long_groups else 0)
    row = lax.broadcasted_iota(I32, (KBLK, tq), 0)
    t_pos = j * tq + lax.broadcasted_iota(I32, (KBLK, tq), 1)

    def key_rows(ref, kb):
        return ref[0, pl.ds(pl.multiple_of(kb * KBLK, KBLK), KBLK), :]

    def next_block(kb):
        return jnp.minimum(kb + 1, n_kb - 1)

    def next_block_of(kb, g):
        return jnp.minimum(kb + 1, (n_kb if g in long_groups else n_main) - 1)

    def group(g):
        return slice(g * LANES, (g + 1) * LANES)


    for g in range(n_grp):
        for p in range(IDX_HEADS // 2):
            iqm_ref[g, 2 * p * LANES:(2 * p + 1) * LANES, :], iqm_ref[g, (2 * p + 1) * LANES:(2 * p + 2) * LANES, :] = (
                _split_head_pair(iq_ref[0, group(g), p * LANES:(p + 1) * LANES]))

    def store_raw_scores(kb, g):
        raw_ref[:, g * iw_cols:(g + 1) * iw_cols] = _dot_nt(key_rows(ikd_ref, kb), iqm_ref[g])

    def finish_scores(kb, g):
        c0 = g * iw_cols
        sc = jnp.maximum(raw_ref[:, c0:c0 + LANES], 0.0) * iwt_ref[0, 0:1, group(g)]
        for h in range(1, IDX_HEADS):
            sc = sc + (jnp.maximum(raw_ref[:, c0 + h * LANES:c0 + (h + 1) * LANES], 0.0)
                       * iwt_ref[0, h:h + 1, group(g)])
        causal = (kb * KBLK + lax.broadcasted_iota(I32, (KBLK, LANES), 0)
                  <= j * tq + g * LANES + lax.broadcasted_iota(I32, (KBLK, LANES), 1))
        sc_ref[kb, :, group(g)] = jnp.where(causal, sc, -jnp.inf)

    for g in groups:
        store_raw_scores(0, g)

    def score_body(kb, carry):
        for g in groups:
            finish_scores(kb, g)
            store_raw_scores(next_block_of(kb, g), g)
        return carry

    lax.fori_loop(0, n_main, score_body, 0)
    if long_groups:
        for g in long_groups:
            finish_scores(n_main, g)
        sc_ref[n_main, :, :long_lanes.start] = jnp.full((KBLK, long_lanes.start), -jnp.inf, F32)

    def count(pred):
        def partial(hit):
            return jnp.sum(hit.astype(I32).reshape(KBLK // SUBLANES, SUBLANES, hit.shape[1]), axis=0)
        acc = lax.fori_loop(0, n_main, lambda kb, acc: acc + partial(pred(sc_ref[kb], slice(None))),
                            jnp.zeros((SUBLANES, tq), I32))
        if long_groups:
            tail = partial(pred(sc_ref[n_main, :, long_lanes], long_lanes))
            acc = jnp.concatenate([acc[:, :long_lanes.start], acc[:, long_lanes] + tail], axis=1)
        return jnp.sum(acc, axis=0, keepdims=True)

    def key_to_float(key):
        bits = key ^ ((key >> 31) & jnp.int32(0x7FFFFFFF))
        return lax.bitcast_convert_type(bits, F32)

    def value_step(i, prefix):
        cand = prefix + (jnp.int32(1) << (31 - i))
        thr = key_to_float(cand)
        return jnp.where(count(lambda s, lanes: s >= thr[:, lanes]) >= topk, cand, prefix)

    prefix = lax.fori_loop(0, 32, value_step, jnp.full((1, tq), -2 ** 31, I32))
    thr = key_to_float(prefix)
    need = (topk - count(lambda s, lanes: s > thr[:, lanes])).astype(F32)

    def tie_ranks(kb):
        tie = jnp.where(sc_ref[kb] == thr, 1.0, 0.0).astype(BF16)
        return _dot(tri_ref[...], tie)

    per_trip = 2 if long_groups else 1
    for u in range(per_trip):
        rank_ref[u] = tie_ranks(u)

    def mask_loop(has_short_rows):
        def bias_body(trip, ties_before):
            for u in range(per_trip):
                kb = trip * per_trip + u
                sc = sc_ref[kb]
                rank = rank_ref[u]
                sel = (sc > thr) | ((sc == thr) & (rank + ties_before <= need))
                if has_short_rows:
                    sel = (sel | (t_pos < topk)) & (kb * KBLK + row <= t_pos)
                bias_ref[kb] = jnp.where(sel, 0.0, NEG_BIG).astype(BF16)
                rank_ref[u] = tie_ranks(jnp.minimum(kb + per_trip, n_kb - per_trip + u))
                ties_before = ties_before + rank[KBLK - 1:KBLK, :]
            return ties_before

        lax.fori_loop(0, n_kb // per_trip, bias_body, jnp.zeros((1, tq), F32))

    pl.when(j * tq < topk)(lambda: mask_loop(True))
    pl.when(j * tq >= topk)(lambda: mask_loop(False))

    eye = (lax.broadcasted_iota(I32, (LANES, LANES), 0)
           == lax.broadcasted_iota(I32, (LANES, LANES), 1)).astype(BF16)
    for g in range(n_grp):
        for p in range(N_PAIRS):
            rhs_ref[g, 2 * p * LANES:(2 * p + 1) * LANES, :LANES], rhs_ref[g, (2 * p + 1) * LANES:(2 * p + 2) * LANES, :LANES] = (
                _split_head_pair(qb_ref[0, group(g), p * LANES:(p + 1) * LANES]))
        for h in range(N_HEADS):
            rhs_ref[g, h * LANES:(h + 1) * LANES, LANES:] = eye
    acc_ref[...] = jnp.zeros_like(acc_ref)
    pt_ref[...] = jnp.zeros_like(pt_ref)

    def cols(g):
        return slice(g * hd_cols, (g + 1) * hd_cols)

    def store_masked_scores(kb, g):
        lhs = jnp.concatenate([key_rows(kbd_ref, kb), bias_ref[kb, :, group(g)]], axis=1)
        s_ref[:, cols(g)] = _dot_nt(lhs, rhs_ref[g])

    def accumulate(kb, g, alpha):
        acc_ref[:, cols(g)] = acc_ref[:, cols(g)] * alpha + _dot(ckvt_ref[0, kb], pt_ref[:, cols(g)])

    for g in range(n_grp):
        store_masked_scores(0, g)

    def softmax_block(g, m, l):
        s = s_ref[:, cols(g)]
        m_new = jnp.maximum(m, jnp.max(s, axis=0, keepdims=True))
        alpha = jnp.exp2(m - m_new)
        pt = jnp.exp2(s - m_new)
        pt_ref[:, cols(g)] = pt.astype(BF16)
        return m_new, alpha * l + jnp.sum(pt, axis=0, keepdims=True), alpha

    def attn_body(kb, carry):
        ms, ls, alphas = carry
        prev = jnp.maximum(kb - 1, 0)
        stats = []
        for g in groups:
            accumulate(prev, g, alphas[g])
            stats.append(softmax_block(g, ms[g], ls[g]))
            store_masked_scores(next_block_of(kb, g), g)
        return tuple(zip(*stats))

    init = (tuple(jnp.full((1, hd_cols), NEG_BIG, F32) for _ in groups),
            tuple(jnp.zeros((1, hd_cols), F32) for _ in groups),
            tuple(jnp.ones((1, hd_cols), F32) for _ in groups))
    ms, ls, alphas = lax.fori_loop(0, n_main, attn_body, init)
    ls, alphas = list(ls), list(alphas)
    for g in groups:
        accumulate(n_main - 1, g, alphas[g])
        if g in long_groups:
            _, ls[g], alphas[g] = softmax_block(g, ms[g], ls[g])
            accumulate(n_main, g, alphas[g])
        o_lat = (acc_ref[:, cols(g)] / ls[g]).astype(BF16)
        for h in range(N_HEADS):
            g_ref[h * HEAD_DIM:(h + 1) * HEAD_DIM, group(g)] = _dot(wuvt_ref[h], o_lat[:, h * LANES:(h + 1) * LANES])

    z = zbt_ref[0].astype(F32)
    gated = g_ref[...] * (z / (1.0 + jnp.exp(-z)))
    ms = jnp.mean(gated * gated, axis=0, keepdims=True)
    out_ref[0] = (gated * lax.rsqrt(ms + EPS) * gain_ref[...]).astype(BF16)


def _out_kernel(yat_ref, ybt_ref, x_ref, w_ref, gain_ref, out_ref):
    y = jnp.concatenate([yat_ref[0].astype(F32).T, ybt_ref[0].astype(F32).T], axis=1).astype(BF16)
    h = x_ref[0] + _dot(y, w_ref[...])
    ms = jnp.mean(h * h, axis=-1, keepdims=True)
    out_ref[0] = h * lax.rsqrt(ms + EPS) * gain_ref[...]


N_SPLIT = 3


def _rope_trig(positions):
    inv_freq = ROPE_THETA ** (-jnp.arange(0, ROT_DIM, 2, dtype=F32) / ROT_DIM)
    ang = positions.astype(F32)[..., None] * inv_freq
    rest = jnp.concatenate([jnp.cos(ang), jnp.sin(ang)], axis=-1)
    pieces = []
    for _ in range(N_SPLIT):
        pieces.append(rest.astype(BF16))
        rest = rest - pieces[-1].astype(F32)
    return jnp.concatenate(pieces, axis=-1)


def _rope_spread():
    half = ROT_DIM // 2
    c = np.arange(LANES) % HEAD_DIM
    hit = (np.arange(half)[:, None] == (c % half)[None, :])
    spread = np.zeros((2 * half, 3 * LANES), np.float32)
    spread[:half, :LANES] = hit & (c < ROT_DIM)
    spread[half:, LANES:2 * LANES] = -1.0 * (hit & (c < half))
    spread[half:, 2 * LANES:] = hit & (c >= half) & (c < ROT_DIM)
    ones = np.concatenate([(c >= ROT_DIM), np.zeros(2 * LANES, bool)]).astype(np.float32)[None]
    return np.tile(spread, (N_SPLIT, 1)), ones


def _full(shape):
    return pl.BlockSpec(shape, lambda *_: (0,) * len(shape))


@jax.jit
def kernel(x, positions, norm_in_gain, w_in, kv_norm_gain, w_uk, w_uv, idx_k_norm_gain,
           branch_norm_gain_a, branch_norm_gain_b, w_out, final_norm_gain):
    bsz, seq, d_model = x.shape
    assert w_in.shape[0] == 1
    assert seq % KBLK == 0 and seq <= PATTERNS[-1][0]
    n_blk = seq // KBLK
    topk = min(TOPK_MAX, seq // 4)
    tm = min(TM, seq)
    tqa, tqb = TQ_A, TQ_B
    params = pltpu.CompilerParams(dimension_semantics=("arbitrary", "arbitrary"),
                                  vmem_limit_bytes=VMEM_LIMIT)

    trig = _rope_trig(positions)
    spread, rope_ones = _rope_spread()
    w_padded = jnp.pad(w_in[0].T, ((0, D_IN_PADDED - w_in.shape[2]), (0, 0))).astype(BF16)
    wuk_dup = jnp.concatenate([w_uk[0], w_uk[0]], axis=1).astype(BF16)
    gik = jnp.concatenate([idx_k_norm_gain[0], jnp.zeros((LANES - IDX_DIM,), F32)])[None]
    wuv_t = jnp.swapaxes(w_uv[0], 1, 2).astype(BF16)
    gain_a = jnp.broadcast_to(branch_norm_gain_a[0][:, None], (WIDTH, tqa))
    gain_b = jnp.broadcast_to(branch_norm_gain_b[0][:, None], (WIDTH, tqb))

    row_blk = lambda c: pl.BlockSpec((1, tm, c), lambda b, i: (b, i, 0))
    col_blk = lambda c: pl.BlockSpec((1, c, tm), lambda b, i: (b, 0, i))
    key_blk = lambda c: pl.BlockSpec((1, tm // KBLK, c, KBLK), lambda b, i: (b, i, 0, 0))
    qa, ka, vat, zat, qb, zbt, iq, kbd, ckvt, ikd, iwt = pl.pallas_call(
        _proj_kernel,
        grid=(bsz, seq // tm),
        in_specs=[row_blk(d_model), _full((1, d_model)), _full((D_IN_PADDED, d_model)),
                  row_blk(N_SPLIT * ROT_DIM), _full(spread.shape), _full(rope_ones.shape),
                  _full((1, KV_LATENT)), _full((KV_LATENT, LANES)), _full((1, LANES))],
        out_specs=[row_blk(WIDTH), row_blk(WIDTH), key_blk(WIDTH), col_blk(WIDTH),
                   row_blk(WIDTH), col_blk(WIDTH), row_blk(2 * LANES), row_blk(LANES),
                   key_blk(KV_LATENT), row_blk(LANES), col_blk(SUBLANES)],
        out_shape=[jax.ShapeDtypeStruct((bsz, seq, WIDTH), BF16),
                   jax.ShapeDtypeStruct((bsz, seq, WIDTH), BF16),
                   jax.ShapeDtypeStruct((bsz, n_blk, WIDTH, KBLK), BF16),
                   jax.ShapeDtypeStruct((bsz, WIDTH, seq), BF16),
                   jax.ShapeDtypeStruct((bsz, seq, WIDTH), BF16),
                   jax.ShapeDtypeStruct((bsz, WIDTH, seq), BF16),
                   jax.ShapeDtypeStruct((bsz, seq, 2 * LANES), BF16),
                   jax.ShapeDtypeStruct((bsz, seq, LANES), BF16),
                   jax.ShapeDtypeStruct((bsz, n_blk, KV_LATENT, KBLK), BF16),
                   jax.ShapeDtypeStruct((bsz, seq, LANES), BF16),
                   jax.ShapeDtypeStruct((bsz, SUBLANES, seq), F32)],
        compiler_params=params, name="proj",
    )(x, norm_in_gain[0][None], w_padded, trig, jnp.asarray(spread, BF16), jnp.asarray(rope_ones),
      kv_norm_gain[0][None], wuk_dup, gik)

    q_row = lambda tq, c: pl.BlockSpec((1, tq, c), lambda b, i: (b, i, 0))
    q_col = lambda tq, c: pl.BlockSpec((1, c, tq), lambda b, i: (b, 0, i))
    per_b3 = lambda s1, s2: pl.BlockSpec((1, s1, s2), lambda b, i: (b, 0, 0))
    per_b4 = lambda s1, s2, s3: pl.BlockSpec((1, s1, s2, s3), lambda b, i: (b, 0, 0, 0))

    bias_np = _mix_a_bias_tables(LANES, KBLK)
    n_grp = tqa // LANES
    yat = pl.pallas_call(
        functools.partial(_mix_a_kernel, bias_np.shape[0]),
        grid=(bsz, seq // tqa),
        in_specs=[q_row(tqa, WIDTH), per_b3(seq, WIDTH), per_b4(n_blk, WIDTH, KBLK), q_col(tqa, WIDTH),
                  _full((WIDTH, tqa)), _full(bias_np.shape)],
        out_specs=q_col(tqa, WIDTH),
        out_shape=jax.ShapeDtypeStruct((bsz, WIDTH, seq), BF16),
        scratch_shapes=[pltpu.VMEM((n_grp, N_HEADS * LANES, LANES), BF16),
                        pltpu.VMEM((n_grp, N_PAIRS, KBLK, 2 * LANES), F32),
                        pltpu.VMEM((n_grp, N_PAIRS, KBLK, 2 * LANES), BF16),
                        pltpu.VMEM((WIDTH, tqa), F32)],
        compiler_params=params, name="mix_a",
    )(qa, ka, vat, zat, gain_a, jnp.asarray(bias_np))

    n_grp = tqb // LANES
    ybt = pl.pallas_call(
        functools.partial(_mix_b_kernel, topk),
        grid=(bsz, seq // tqb),
        in_specs=[q_row(tqb, WIDTH), q_row(tqb, 2 * LANES), q_col(tqb, SUBLANES), q_col(tqb, WIDTH),
                  per_b3(seq, LANES), per_b3(seq, LANES), per_b4(n_blk, KV_LATENT, KBLK),
                  _full((N_HEADS, HEAD_DIM, KV_LATENT)), _full((WIDTH, tqb)), _full((KBLK, KBLK))],
        out_specs=q_col(tqb, WIDTH),
        out_shape=jax.ShapeDtypeStruct((bsz, WIDTH, seq), BF16),
        scratch_shapes=[pltpu.VMEM((n_blk, KBLK, tqb), F32),
                        pltpu.VMEM((n_blk, KBLK, tqb), BF16),
                        pltpu.VMEM((n_grp, IDX_HEADS * LANES, LANES), BF16),
                        pltpu.VMEM((n_grp, N_HEADS * LANES, 2 * LANES), BF16),
                        pltpu.VMEM((KBLK, n_grp * IDX_HEADS * LANES), F32),
                        pltpu.VMEM((2, KBLK, tqb), F32),
                        pltpu.VMEM((KBLK, n_grp * N_HEADS * LANES), F32),
                        pltpu.VMEM((KBLK, n_grp * N_HEADS * LANES), BF16),
                        pltpu.VMEM((KV_LATENT, n_grp * N_HEADS * LANES), F32),
                        pltpu.VMEM((WIDTH, tqb), F32)],
        compiler_params=params, name="mix_b",
    )(qb, iq, iwt, zbt, ikd, kbd, ckvt, wuv_t, gain_b,
      jnp.asarray(np.tril(np.ones((KBLK, KBLK), np.float32)), BF16))

    return pl.pallas_call(
        _out_kernel,
        grid=(bsz, seq // tm),
        in_specs=[col_blk(WIDTH), col_blk(WIDTH), row_blk(d_model),
                  _full((2 * WIDTH, d_model)), _full((1, d_model))],
        out_specs=row_blk(d_model),
        out_shape=jax.ShapeDtypeStruct((bsz, seq, d_model), F32),
        compiler_params=params, name="out_proj",
    )(yat, ybt, x, w_out[0].astype(BF16), final_norm_gain[None])
```

```python
import functools
import math

import numpy as np
import jax
import jax.numpy as jnp
from jax import lax
from jax.experimental import pallas as pl
from jax.experimental.pallas import tpu as pltpu

F32 = jnp.float32
BF16 = jnp.bfloat16
I32 = jnp.int32

HEAD_DIM = 64
N_HEADS = 8
WIDTH = N_HEADS * HEAD_DIM
N_PAIRS = N_HEADS // 2
ROT_DIM = HEAD_DIM // 4
ROPE_THETA = 500000.0
KV_LATENT = 128
IDX_HEADS = 4
IDX_DIM = 64
TOPK_MAX = 256
EPS = 1e-6
PATTERNS = ((128, 1), (512, 4), (2048, 16))

LANES = 128
SUBLANES = 8
MXU_DEPTH = 256
KBLK = MXU_DEPTH
TQ_A = 4 * LANES
TQ_B = 4 * LANES
TM = 1024
NEG_BIG = -1e30
VMEM_LIMIT = 56 * 1024 * 1024
Q_SCALE = HEAD_DIM ** -0.5 * math.log2(math.e)

U_QA, U_KA, U_VA, U_ZA, U_QB, U_CKV, U_ZB, U_IQ, U_LAST, N_UNITS = 0, 4, 8, 12, 16, 20, 21, 25, 27, 28
D_IN_PADDED = N_UNITS * LANES


def _dot(a, b):
    return jnp.dot(a, b, preferred_element_type=F32)


def _dot_nt(a, b):
    return lax.dot_general(a, b, (((1,), (1,)), ((), ())), preferred_element_type=F32)


def _split_head_pair(xp):
    lane = lax.broadcasted_iota(I32, xp.shape, 1)
    zero = jnp.zeros_like(xp)
    return jnp.where(lane < HEAD_DIM, xp, zero), jnp.where(lane >= HEAD_DIM, xp, zero)


def _proj_kernel(x_ref, gin_ref, w_ref, trig_ref, spread_ref, ones_ref, gkv_ref, wuk_ref, gik_ref,
                 qa_ref, ka_ref, vat_ref, zat_ref, qb_ref, zbt_ref, iq_ref, kbd_ref, ckvt_ref,
                 ikd_ref, iwt_ref):
    tm = x_ref.shape[1]
    nblk = tm // KBLK
    x = x_ref[0]
    ms = jnp.mean(x * x, axis=-1, keepdims=True)
    u = (x * lax.rsqrt(ms + EPS) * gin_ref[...]).astype(BF16)
    table = _dot(trig_ref[0], spread_ref[...]) + ones_ref[...]
    cosf, sina, sinb = (table[:, i * LANES:(i + 1) * LANES] for i in range(3))

    def rope(t):
        return t * cosf + pltpu.roll(t, LANES - ROT_DIM // 2, 1) * sina + pltpu.roll(t, ROT_DIM // 2, 1) * sinb

    def proj(unit):
        return _dot_nt(u, w_ref[unit * LANES:(unit + 2) * LANES, :])

    def store_t(ref, unit, t):
        tt = t.T.astype(BF16)
        for i in range(nblk):
            ref[0, i, unit * LANES:(unit + 1) * LANES, :] = tt[:, i * KBLK:(i + 1) * KBLK]

    def lanes(unit):
        return slice(unit * LANES, (unit + 1) * LANES)

    def finish(unit, t):
        if unit < U_KA:
            qa_ref[0, :, lanes(unit - U_QA)] = (rope(t) * Q_SCALE).astype(BF16)
        elif unit < U_VA:
            ka_ref[0, :, lanes(unit - U_KA)] = rope(t).astype(BF16)
        elif unit < U_ZA:
            store_t(vat_ref, unit - U_VA, t)
        elif unit < U_QB:
            zat_ref[0, lanes(unit - U_ZA), :] = t.T.astype(BF16)
        elif unit < U_CKV:
            qb_ref[0, :, lanes(unit - U_QB)] = (rope(t) * Q_SCALE).astype(BF16)
        elif unit == U_CKV:
            ckv = t * lax.rsqrt(jnp.mean(t * t, axis=-1, keepdims=True) + EPS) * gkv_ref[...]
            store_t(ckvt_ref, 0, ckv)
            kbd_ref[0] = rope(_dot(ckv.astype(BF16), wuk_ref[...])).astype(BF16)
        elif unit < U_IQ:
            zbt_ref[0, lanes(unit - U_ZB), :] = t.T.astype(BF16)
        elif unit < U_LAST:
            iq_ref[0, :, lanes(unit - U_IQ)] = rope(t).astype(BF16)
        else:
            lane = lax.broadcasted_iota(I32, t.shape, 1)
            ik_ms = jnp.sum(jnp.where(lane < IDX_DIM, t * t, 0.0), axis=-1, keepdims=True) / IDX_DIM
            ik = rope(t * lax.rsqrt(ik_ms + EPS) * gik_ref[...])
            ikd_ref[0] = (ik + pltpu.roll(ik, IDX_DIM, 1)).astype(BF16)
            iwt_ref[0] = t.T[IDX_DIM:IDX_DIM + SUBLANES, :] * (IDX_HEADS ** -0.5 * IDX_DIM ** -0.5)

    first = [U_LAST - 1, U_CKV]
    for unit in first + [u for u in range(0, N_UNITS, 2) if u not in first]:
        acc = proj(unit)
        finish(unit, acc[:, :LANES])
        finish(unit + 1, acc[:, LANES:])


def _mix_a_bias_tables(tq, tk):
    assert tq == LANES
    o_last = PATTERNS[1][0] + tk - LANES
    offsets = list(range(0, o_last + 1, LANES)) + [o_last + LANES]
    qi = np.arange(tq)[None, :]
    kj = np.arange(tk)[:, None]
    tables = []
    for o in offsets:
        d = o + qi - kj
        mult = np.zeros(d.shape, np.int64)
        for window, dil in PATTERNS:
            mult += ((d >= 0) & (d % dil == 0) & (d <= window)).astype(np.int64)
        tables.append(np.where(mult > 0, np.log2(np.maximum(mult, 1)), NEG_BIG))
    return np.stack(tables).astype(np.float32)


def _mix_a_kernel(n_tables, qa_ref, ka_ref, vat_ref, zat_ref, gain_ref, bias_ref,
                  out_ref, qm_ref, s_ref, pt_ref, acc_ref):
    tq = qa_ref.shape[1]
    n_grp = tq // LANES
    j = pl.program_id(1)
    long_groups = [g for g in range(n_grp) if (g * LANES + LANES - 1) // KBLK > 0]
    assert tq <= 2 * KBLK and (not long_groups or tq % KBLK == 0)
    n_main = (j * tq + LANES - 1) // KBLK + 1
    chains = [(g, p) for g in range(n_grp) for p in range(N_PAIRS)]

    def group(g):
        return slice(g * LANES, (g + 1) * LANES)

    def last_block(g):
        return n_main if g in long_groups else n_main - 1

    for g, p in chains:
        qm_ref[g, 2 * p * LANES:(2 * p + 1) * LANES, :], qm_ref[g, (2 * p + 1) * LANES:(2 * p + 2) * LANES, :] = (
            _split_head_pair(qa_ref[0, group(g), p * LANES:(p + 1) * LANES]))
    acc_ref[...] = jnp.zeros_like(acc_ref)
    pt_ref[...] = jnp.zeros_like(pt_ref)

    def scores(kb, g, p):
        k0 = pl.multiple_of(kb * KBLK, KBLK)
        return _dot_nt(ka_ref[0, pl.ds(k0, KBLK), p * LANES:(p + 1) * LANES],
                       qm_ref[g, 2 * p * LANES:(2 * p + 2) * LANES, :])

    def accumulate(kb, g, p, alpha):
        vt = vat_ref[0, kb, p * LANES:(p + 1) * LANES, :]
        r0 = p * LANES
        acc_ref[r0:r0 + HEAD_DIM, group(g)] = (acc_ref[r0:r0 + HEAD_DIM, group(g)] * alpha[:, :LANES]
                                               + _dot(vt[:HEAD_DIM], pt_ref[g, p, :, :LANES]))
        acc_ref[r0 + HEAD_DIM:r0 + LANES, group(g)] = (acc_ref[r0 + HEAD_DIM:r0 + LANES, group(g)] * alpha[:, LANES:]
                                                       + _dot(vt[HEAD_DIM:], pt_ref[g, p, :, LANES:]))

    for g, p in chains:
        s_ref[g, p] = scores(0, g, p)

    def softmax_block(kb, g, p, m, l):
        bias = bias_ref[jnp.minimum((j * tq + g * LANES - kb * KBLK) // LANES, n_tables - 1)]
        s = s_ref[g, p] + jnp.concatenate([bias, bias], axis=1)
        m_new = jnp.maximum(m, jnp.max(s, axis=0, keepdims=True))
        alpha = jnp.exp2(m - m_new)
        pt = jnp.exp2(s - m_new)
        pt_ref[g, p] = pt.astype(BF16)
        return m_new, alpha * l + jnp.sum(pt, axis=0, keepdims=True), alpha

    def body(kb, carry):
        ms, ls, alphas = carry
        prev = jnp.maximum(kb - 1, 0)
        stats = []
        for c, (g, p) in enumerate(chains):
            accumulate(prev, g, p, alphas[c])
            stats.append(softmax_block(kb, g, p, ms[c], ls[c]))
            s_ref[g, p] = scores(jnp.minimum(kb + 1, last_block(g)), g, p)
        return tuple(zip(*stats))

    init = (tuple(jnp.full((1, 2 * LANES), NEG_BIG, F32) for _ in chains),
            tuple(jnp.zeros((1, 2 * LANES), F32) for _ in chains),
            tuple(jnp.ones((1, 2 * LANES), F32) for _ in chains))
    ms, ls, alphas = lax.fori_loop(0, n_main, body, init)
    ls, alphas = list(ls), list(alphas)
    for c, (g, p) in enumerate(chains):
        accumulate(n_main - 1, g, p, alphas[c])
        if g in long_groups:
            _, ls[c], alphas[c] = softmax_block(n_main, g, p, ms[c], ls[c])
            accumulate(n_main, g, p, alphas[c])
        r0 = p * LANES
        acc_ref[r0:r0 + HEAD_DIM, group(g)] = acc_ref[r0:r0 + HEAD_DIM, group(g)] / ls[c][:, :LANES]
        acc_ref[r0 + HEAD_DIM:r0 + LANES, group(g)] = acc_ref[r0 + HEAD_DIM:r0 + LANES, group(g)] / ls[c][:, LANES:]

    z = zat_ref[0].astype(F32)
    g = acc_ref[...] * (z / (1.0 + jnp.exp(-z)))
    ms = jnp.mean(g * g, axis=0, keepdims=True)
    out_ref[0] = (g * lax.rsqrt(ms + EPS) * gain_ref[...]).astype(BF16)


def _mix_b_kernel(topk, qb_ref, iq_ref, iwt_ref, zbt_ref, ikd_ref, kbd_ref, ckvt_ref, wuvt_ref,
                  gain_ref, tri_ref, out_ref, sc_ref, bias_ref, iqm_ref, rhs_ref, raw_ref, rank_ref,
                  s_ref, pt_ref, acc_ref, g_ref):
    tq = qb_ref.shape[1]
    n_grp = tq // LANES
    iw_cols = IDX_HEADS * LANES
    hd_cols = N_HEADS * LANES
    j = pl.program_id(1)
    groups = list(range(n_grp))
    long_groups = [g for g in groups if (g * LANES + LANES - 1) // KBLK > 0]
    short_groups = [g for g in groups if g not in long_groups]
    assert tq <= 2 * KBLK and (not long_groups or tq % KBLK == 0)
    long_lanes = slice(len(short_groups) * LANES, tq)
    n_main = (j * tq + LANES - 1) // KBLK + 1
    n_kb = n_main + (1 if long_groups else 0)
    row = lax.broadcasted_iota(I32, (KBLK, tq), 0)
    t_pos = j * tq + lax.broadcasted_iota(I32, (KBLK, tq), 1)

    def key_rows(ref, kb):
        return ref[0, pl.ds(pl.multiple_of(kb * KBLK, KBLK), KBLK), :]

    def next_block(kb):
        return jnp.minimum(kb + 1, n_kb - 1)

    def next_block_of(kb, g):
        return jnp.minimum(kb + 1, (n_kb if g in long_groups else n_main) - 1)

    def group(g):
        return slice(g * LANES, (g + 1) * LANES)


    for g in range(n_grp):
        for p in range(IDX_HEADS // 2):
            iqm_ref[g, 2 * p * LANES:(2 * p + 1) * LANES, :], iqm_ref[g, (2 * p + 1) * LANES:(2 * p + 2) * LANES, :] = (
                _split_head_pair(iq_ref[0, group(g), p * LANES:(p + 1) * LANES]))

    def store_raw_scores(kb, g):
        raw_ref[:, g * iw_cols:(g + 1) * iw_cols] = _dot_nt(key_rows(ikd_ref, kb), iqm_ref[g])

    def finish_scores(kb, g):
        c0 = g * iw_cols
        sc = jnp.maximum(raw_ref[:, c0:c0 + LANES], 0.0) * iwt_ref[0, 0:1, group(g)]
        for h in range(1, IDX_HEADS):
            sc = sc + (jnp.maximum(raw_ref[:, c0 + h * LANES:c0 + (h + 1) * LANES], 0.0)
                       * iwt_ref[0, h:h + 1, group(g)])
        causal = (kb * KBLK + lax.broadcasted_iota(I32, (KBLK, LANES), 0)
                  <= j * tq + g * LANES + lax.broadcasted_iota(I32, (KBLK, LANES), 1))
        sc_ref[kb, :, group(g)] = jnp.where(causal, sc, -jnp.inf)

    for g in groups:
        store_raw_scores(0, g)

    def score_body(kb, carry):
        for g in groups:
            finish_scores(kb, g)
            store_raw_scores(next_block_of(kb, g), g)
        return carry

    lax.fori_loop(0, n_main, score_body, 0)
    if long_groups:
        for g in long_groups:
            finish_scores(n_main, g)
        sc_ref[n_main, :, :long_lanes.start] = jnp.full((KBLK, long_lanes.start), -jnp.inf, F32)

    def count(pred):
        def partial(hit):
            return jnp.sum(hit.astype(I32).reshape(KBLK // SUBLANES, SUBLANES, hit.shape[1]), axis=0)
        acc = lax.fori_loop(0, n_main, lambda kb, acc: acc + partial(pred(sc_ref[kb], slice(None))),
                            jnp.zeros((SUBLANES, tq), I32))
        if long_groups:
            tail = partial(pred(sc_ref[n_main, :, long_lanes], long_lanes))
            acc = jnp.concatenate([acc[:, :long_lanes.start], acc[:, long_lanes] + tail], axis=1)
        return jnp.sum(acc, axis=0, keepdims=True)

    def key_to_float(key):
        bits = key ^ ((key >> 31) & jnp.int32(0x7FFFFFFF))
        return lax.bitcast_convert_type(bits, F32)

    def value_step(i, carry):
        prefix, n_ge = carry
        cand = prefix + (jnp.int32(1) << (31 - i))
        thr = key_to_float(cand)
        cnt = count(lambda s, lanes: s >= thr[:, lanes])
        accept = cnt >= topk
        return jnp.where(accept, cand, prefix), jnp.where(accept, cnt, n_ge)

    prefix, n_ge = lax.fori_loop(0, 32, value_step,
                                 (jnp.full((1, tq), -2 ** 31, I32), jnp.zeros((1, tq), I32)))
    thr = key_to_float(prefix)
    surplus = (n_ge - topk).astype(F32)

    def tie_ranks(kb):
        tie = jnp.where(sc_ref[kb] == thr, 1.0, 0.0).astype(BF16)
        return _dot(tri_ref[...], tie)

    per_trip = 2 if long_groups else 1
    for u in range(per_trip):
        rank_ref[u] = tie_ranks(n_kb - 1 - u)

    def mask_loop(has_short_rows):
        def bias_body(trip, ties_after):
            for u in range(per_trip):
                kb = n_kb - 1 - (trip * per_trip + u)
                sc = sc_ref[kb]
                rank = rank_ref[u]
                sel = (sc > thr) | ((sc == thr) & (rank + ties_after > surplus))
                if has_short_rows:
                    sel = (sel | (t_pos < topk)) & (kb * KBLK + row <= t_pos)
                bias_ref[kb] = jnp.where(sel, 0.0, NEG_BIG).astype(BF16)
                rank_ref[u] = tie_ranks(jnp.maximum(kb - per_trip, (n_kb - 1 - u) % per_trip))
                ties_after = ties_after + rank[0:1, :]
            return ties_after

        lax.fori_loop(0, n_kb // per_trip, bias_body, jnp.zeros((1, tq), F32))

    pl.when(j * tq < topk)(lambda: mask_loop(True))
    pl.when(j * tq >= topk)(lambda: mask_loop(False))

    eye = (lax.broadcasted_iota(I32, (LANES, LANES), 0)
           == lax.broadcasted_iota(I32, (LANES, LANES), 1)).astype(BF16)
    for g in range(n_grp):
        for p in range(N_PAIRS):
            rhs_ref[g, 2 * p * LANES:(2 * p + 1) * LANES, :LANES], rhs_ref[g, (2 * p + 1) * LANES:(2 * p + 2) * LANES, :LANES] = (
                _split_head_pair(qb_ref[0, group(g), p * LANES:(p + 1) * LANES]))
        for h in range(N_HEADS):
            rhs_ref[g, h * LANES:(h + 1) * LANES, LANES:] = eye
    acc_ref[...] = jnp.zeros_like(acc_ref)
    pt_ref[...] = jnp.zeros_like(pt_ref)

    def cols(g):
        return slice(g * hd_cols, (g + 1) * hd_cols)

    def store_masked_scores(kb, g):
        lhs = jnp.concatenate([key_rows(kbd_ref, kb), bias_ref[kb, :, group(g)]], axis=1)
        s_ref[:, cols(g)] = _dot_nt(lhs, rhs_ref[g])

    def accumulate(kb, g, alpha):
        acc_ref[:, cols(g)] = acc_ref[:, cols(g)] * alpha + _dot(ckvt_ref[0, kb], pt_ref[:, cols(g)])

    for g in range(n_grp):
        store_masked_scores(0, g)

    def softmax_block(g, m, l):
        s = s_ref[:, cols(g)]
        m_new = jnp.maximum(m, jnp.max(s, axis=0, keepdims=True))
        alpha = jnp.exp2(m - m_new)
        pt = jnp.exp2(s - m_new)
        pt_ref[:, cols(g)] = pt.astype(BF16)
        return m_new, alpha * l + jnp.sum(pt, axis=0, keepdims=True), alpha

    def attn_body(kb, carry):
        ms, ls, alphas = carry
        prev = jnp.maximum(kb - 1, 0)
        stats = []
        for g in groups:
            accumulate(prev, g, alphas[g])
            stats.append(softmax_block(g, ms[g], ls[g]))
            store_masked_scores(next_block_of(kb, g), g)
        return tuple(zip(*stats))

    init = (tuple(jnp.full((1, hd_cols), NEG_BIG, F32) for _ in groups),
            tuple(jnp.zeros((1, hd_cols), F32) for _ in groups),
            tuple(jnp.ones((1, hd_cols), F32) for _ in groups))
    ms, ls, alphas = lax.fori_loop(0, n_main, attn_body, init)
    ls, alphas = list(ls), list(alphas)
    for g in groups:
        accumulate(n_main - 1, g, alphas[g])
        if g in long_groups:
            _, ls[g], alphas[g] = softmax_block(g, ms[g], ls[g])
            accumulate(n_main, g, alphas[g])
        o_lat = (acc_ref[:, cols(g)] / ls[g]).astype(BF16)
        for h in range(N_HEADS):
            g_ref[h * HEAD_DIM:(h + 1) * HEAD_DIM, group(g)] = _dot(wuvt_ref[h], o_lat[:, h * LANES:(h + 1) * LANES])

    z = zbt_ref[0].astype(F32)
    gated = g_ref[...] * (z / (1.0 + jnp.exp(-z)))
    ms = jnp.mean(gated * gated, axis=0, keepdims=True)
    out_ref[0] = (gated * lax.rsqrt(ms + EPS) * gain_ref[...]).astype(BF16)


def _out_kernel(yat_ref, ybt_ref, x_ref, w_ref, gain_ref, out_ref):
    y = jnp.concatenate([yat_ref[0].astype(F32).T, ybt_ref[0].astype(F32).T], axis=1).astype(BF16)
    h = x_ref[0] + _dot(y, w_ref[...])
    ms = jnp.mean(h * h, axis=-1, keepdims=True)
    out_ref[0] = h * lax.rsqrt(ms + EPS) * gain_ref[...]


N_SPLIT = 3


def _rope_trig(positions):
    inv_freq = ROPE_THETA ** (-jnp.arange(0, ROT_DIM, 2, dtype=F32) / ROT_DIM)
    ang = positions.astype(F32)[..., None] * inv_freq
    rest = jnp.concatenate([jnp.cos(ang), jnp.sin(ang)], axis=-1)
    pieces = []
    for _ in range(N_SPLIT):
        pieces.append(rest.astype(BF16))
        rest = rest - pieces[-1].astype(F32)
    return jnp.concatenate(pieces, axis=-1)


def _rope_spread():
    half = ROT_DIM // 2
    c = np.arange(LANES) % HEAD_DIM
    hit = (np.arange(half)[:, None] == (c % half)[None, :])
    spread = np.zeros((2 * half, 3 * LANES), np.float32)
    spread[:half, :LANES] = hit & (c < ROT_DIM)
    spread[half:, LANES:2 * LANES] = -1.0 * (hit & (c < half))
    spread[half:, 2 * LANES:] = hit & (c >= half) & (c < ROT_DIM)
    ones = np.concatenate([(c >= ROT_DIM), np.zeros(2 * LANES, bool)]).astype(np.float32)[None]
    return np.tile(spread, (N_SPLIT, 1)), ones


def _full(shape):
    return pl.BlockSpec(shape, lambda *_: (0,) * len(shape))


@jax.jit
def kernel(x, positions, norm_in_gain, w_in, kv_norm_gain, w_uk, w_uv, idx_k_norm_gain,
           branch_norm_gain_a, branch_norm_gain_b, w_out, final_norm_gain):
    bsz, seq, d_model = x.shape
    assert w_in.shape[0] == 1
    assert seq % KBLK == 0 and seq <= PATTERNS[-1][0]
    n_blk = seq // KBLK
    topk = min(TOPK_MAX, seq // 4)
    tm = min(TM, seq)
    tqa, tqb = TQ_A, TQ_B
    params = pltpu.CompilerParams(dimension_semantics=("arbitrary", "arbitrary"),
                                  vmem_limit_bytes=VMEM_LIMIT)

    trig = _rope_trig(positions)
    spread, rope_ones = _rope_spread()
    w_padded = jnp.pad(w_in[0].T, ((0, D_IN_PADDED - w_in.shape[2]), (0, 0))).astype(BF16)
    wuk_dup = jnp.concatenate([w_uk[0], w_uk[0]], axis=1).astype(BF16)
    gik = jnp.concatenate([idx_k_norm_gain[0], jnp.zeros((LANES - IDX_DIM,), F32)])[None]
    wuv_t = jnp.swapaxes(w_uv[0], 1, 2).astype(BF16)
    gain_a = jnp.broadcast_to(branch_norm_gain_a[0][:, None], (WIDTH, tqa))
    gain_b = jnp.broadcast_to(branch_norm_gain_b[0][:, None], (WIDTH, tqb))

    row_blk = lambda c: pl.BlockSpec((1, tm, c), lambda b, i: (b, i, 0))
    col_blk = lambda c: pl.BlockSpec((1, c, tm), lambda b, i: (b, 0, i))
    key_blk = lambda c: pl.BlockSpec((1, tm // KBLK, c, KBLK), lambda b, i: (b, i, 0, 0))
    qa, ka, vat, zat, qb, zbt, iq, kbd, ckvt, ikd, iwt = pl.pallas_call(
        _proj_kernel,
        grid=(bsz, seq // tm),
        in_specs=[row_blk(d_model), _full((1, d_model)), _full((D_IN_PADDED, d_model)),
                  row_blk(N_SPLIT * ROT_DIM), _full(spread.shape), _full(rope_ones.shape),
                  _full((1, KV_LATENT)), _full((KV_LATENT, LANES)), _full((1, LANES))],
        out_specs=[row_blk(WIDTH), row_blk(WIDTH), key_blk(WIDTH), col_blk(WIDTH),
                   row_blk(WIDTH), col_blk(WIDTH), row_blk(2 * LANES), row_blk(LANES),
                   key_blk(KV_LATENT), row_blk(LANES), col_blk(SUBLANES)],
        out_shape=[jax.ShapeDtypeStruct((bsz, seq, WIDTH), BF16),
                   jax.ShapeDtypeStruct((bsz, seq, WIDTH), BF16),
                   jax.ShapeDtypeStruct((bsz, n_blk, WIDTH, KBLK), BF16),
                   jax.ShapeDtypeStruct((bsz, WIDTH, seq), BF16),
                   jax.ShapeDtypeStruct((bsz, seq, WIDTH), BF16),
                   jax.ShapeDtypeStruct((bsz, WIDTH, seq), BF16),
                   jax.ShapeDtypeStruct((bsz, seq, 2 * LANES), BF16),
                   jax.ShapeDtypeStruct((bsz, seq, LANES), BF16),
                   jax.ShapeDtypeStruct((bsz, n_blk, KV_LATENT, KBLK), BF16),
                   jax.ShapeDtypeStruct((bsz, seq, LANES), BF16),
                   jax.ShapeDtypeStruct((bsz, SUBLANES, seq), F32)],
        compiler_params=params, name="proj",
    )(x, norm_in_gain[0][None], w_padded, trig, jnp.asarray(spread, BF16), jnp.asarray(rope_ones),
      kv_norm_gain[0][None], wuk_dup, gik)

    q_row = lambda tq, c: pl.BlockSpec((1, tq, c), lambda b, i: (b, i, 0))
    q_col = lambda tq, c: pl.BlockSpec((1, c, tq), lambda b, i: (b, 0, i))
    per_b3 = lambda s1, s2: pl.BlockSpec((1, s1, s2), lambda b, i: (b, 0, 0))
    per_b4 = lambda s1, s2, s3: pl.BlockSpec((1, s1, s2, s3), lambda b, i: (b, 0, 0, 0))

    bias_np = _mix_a_bias_tables(LANES, KBLK)
    n_grp = tqa // LANES
    yat = pl.pallas_call(
        functools.partial(_mix_a_kernel, bias_np.shape[0]),
        grid=(bsz, seq // tqa),
        in_specs=[q_row(tqa, WIDTH), per_b3(seq, WIDTH), per_b4(n_blk, WIDTH, KBLK), q_col(tqa, WIDTH),
                  _full((WIDTH, tqa)), _full(bias_np.shape)],
        out_specs=q_col(tqa, WIDTH),
        out_shape=jax.ShapeDtypeStruct((bsz, WIDTH, seq), BF16),
        scratch_shapes=[pltpu.VMEM((n_grp, N_HEADS * LANES, LANES), BF16),
                        pltpu.VMEM((n_grp, N_PAIRS, KBLK, 2 * LANES), F32),
                        pltpu.VMEM((n_grp, N_PAIRS, KBLK, 2 * LANES), BF16),
                        pltpu.VMEM((WIDTH, tqa), F32)],
        compiler_params=params, name="mix_a",
    )(qa, ka, vat, zat, gain_a, jnp.asarray(bias_np))

    n_grp = tqb // LANES
    ybt = pl.pallas_call(
        functools.partial(_mix_b_kernel, topk),
        grid=(bsz, seq // tqb),
        in_specs=[q_row(tqb, WIDTH), q_row(tqb, 2 * LANES), q_col(tqb, SUBLANES), q_col(tqb, WIDTH),
                  per_b3(seq, LANES), per_b3(seq, LANES), per_b4(n_blk, KV_LATENT, KBLK),
                  _full((N_HEADS, HEAD_DIM, KV_LATENT)), _full((WIDTH, tqb)), _full((KBLK, KBLK))],
        out_specs=q_col(tqb, WIDTH),
        out_shape=jax.ShapeDtypeStruct((bsz, WIDTH, seq), BF16),
        scratch_shapes=[pltpu.VMEM((n_blk, KBLK, tqb), F32),
                        pltpu.VMEM((n_blk, KBLK, tqb), BF16),
                        pltpu.VMEM((n_grp, IDX_HEADS * LANES, LANES), BF16),
                        pltpu.VMEM((n_grp, N_HEADS * LANES, 2 * LANES), BF16),
                        pltpu.VMEM((KBLK, n_grp * IDX_HEADS * LANES), F32),
                        pltpu.VMEM((2, KBLK, tqb), F32),
                        pltpu.VMEM((KBLK, n_grp * N_HEADS * LANES), F32),
                        pltpu.VMEM((KBLK, n_grp * N_HEADS * LANES), BF16),
                        pltpu.VMEM((KV_LATENT, n_grp * N_HEADS * LANES), F32),
                        pltpu.VMEM((WIDTH, tqb), F32)],
        compiler_params=params, name="mix_b",
    )(qb, iq, iwt, zbt, ikd, kbd, ckvt, wuv_t, gain_b,
      jnp.asarray(np.triu(np.ones((KBLK, KBLK), np.float32)), BF16))

    return pl.pallas_call(
        _out_kernel,
        grid=(bsz, seq // tm),
        in_specs=[col_blk(WIDTH), col_blk(WIDTH), row_blk(d_model),
                  _full((2 * WIDTH, d_model)), _full((1, d_model))],
        out_specs=row_blk(d_model),
        out_shape=jax.ShapeDtypeStruct((bsz, seq, d_model), F32),
        compiler_params=params, name="out_proj",
    )(yat, ybt, x, w_out[0].astype(BF16), final_norm_gain[None])
```

```python
import functools
import math

import numpy as np
import jax
import jax.numpy as jnp
from jax import lax
from jax.experimental import pallas as pl
from jax.experimental.pallas import tpu as pltpu

F32 = jnp.float32
BF16 = jnp.bfloat16
I32 = jnp.int32

HEAD_DIM = 64
N_HEADS = 8
WIDTH = N_HEADS * HEAD_DIM
N_PAIRS = N_HEADS // 2
ROT_DIM = HEAD_DIM // 4
ROPE_THETA = 500000.0
KV_LATENT = 128
IDX_HEADS = 4
IDX_DIM = 64
TOPK_MAX = 256
EPS = 1e-6
PATTERNS = ((128, 1), (512, 4), (2048, 16))

LANES = 128
SUBLANES = 8
MXU_DEPTH = 256
KBLK = MXU_DEPTH
TQ_A = 4 * LANES
TQ_B = 4 * LANES
TM = 1024
NEG_BIG = -1e30
F32_BITS = 32
F32_MAGNITUDE_MASK = 0x7FFFFFFF
VMEM_LIMIT = 56 * 1024 * 1024
Q_SCALE = HEAD_DIM ** -0.5 * math.log2(math.e)

U_QA, U_KA, U_VA, U_ZA, U_QB, U_CKV, U_ZB, U_IQ, U_LAST, N_UNITS = 0, 4, 8, 12, 16, 20, 21, 25, 27, 28
D_IN_PADDED = N_UNITS * LANES


def _dot(a, b):
    return jnp.dot(a, b, preferred_element_type=F32)


def _dot_nt(a, b):
    return lax.dot_general(a, b, (((1,), (1,)), ((), ())), preferred_element_type=F32)


def _split_head_pair(xp):
    lane = lax.broadcasted_iota(I32, xp.shape, 1)
    zero = jnp.zeros_like(xp)
    return jnp.where(lane < HEAD_DIM, xp, zero), jnp.where(lane >= HEAD_DIM, xp, zero)


def _proj_kernel(x_ref, gin_ref, w_ref, trig_ref, spread_ref, ones_ref, gkv_ref, wuk_ref, gik_ref,
                 qa_ref, ka_ref, vat_ref, zat_ref, qb_ref, zbt_ref, iq_ref, kbd_ref, ckvt_ref,
                 ikd_ref, iwt_ref):
    tm = x_ref.shape[1]
    nblk = tm // KBLK
    x = x_ref[0]
    ms = jnp.mean(x * x, axis=-1, keepdims=True)
    u = (x * lax.rsqrt(ms + EPS) * gin_ref[...]).astype(BF16)
    table = _dot(trig_ref[0], spread_ref[...]) + ones_ref[...]
    cosf, sina, sinb = (table[:, i * LANES:(i + 1) * LANES] for i in range(3))

    def rope(t):
        return t * cosf + pltpu.roll(t, LANES - ROT_DIM // 2, 1) * sina + pltpu.roll(t, ROT_DIM // 2, 1) * sinb

    def proj(unit):
        return _dot_nt(u, w_ref[unit * LANES:(unit + 2) * LANES, :])

    def store_t(ref, unit, t):
        tt = t.T.astype(BF16)
        for i in range(nblk):
            ref[0, i, unit * LANES:(unit + 1) * LANES, :] = tt[:, i * KBLK:(i + 1) * KBLK]

    def lanes(unit):
        return slice(unit * LANES, (unit + 1) * LANES)

    def finish(unit, t):
        if unit < U_KA:
            qa_ref[0, :, lanes(unit - U_QA)] = (rope(t) * Q_SCALE).astype(BF16)
        elif unit < U_VA:
            ka_ref[0, :, lanes(unit - U_KA)] = rope(t).astype(BF16)
        elif unit < U_ZA:
            store_t(vat_ref, unit - U_VA, t)
        elif unit < U_QB:
            zat_ref[0, lanes(unit - U_ZA), :] = t.T.astype(BF16)
        elif unit < U_CKV:
            qb_ref[0, :, lanes(unit - U_QB)] = (rope(t) * Q_SCALE).astype(BF16)
        elif unit == U_CKV:
            ckv = t * lax.rsqrt(jnp.mean(t * t, axis=-1, keepdims=True) + EPS) * gkv_ref[...]
            store_t(ckvt_ref, 0, ckv)
            kbd_ref[0] = rope(_dot(ckv.astype(BF16), wuk_ref[...])).astype(BF16)
        elif unit < U_IQ:
            zbt_ref[0, lanes(unit - U_ZB), :] = t.T.astype(BF16)
        elif unit < U_LAST:
            iq_ref[0, :, lanes(unit - U_IQ)] = rope(t).astype(BF16)
        else:
            lane = lax.broadcasted_iota(I32, t.shape, 1)
            ik_ms = jnp.sum(jnp.where(lane < IDX_DIM, t * t, 0.0), axis=-1, keepdims=True) / IDX_DIM
            ik = rope(t * lax.rsqrt(ik_ms + EPS) * gik_ref[...])
            ikd_ref[0] = (ik + pltpu.roll(ik, IDX_DIM, 1)).astype(BF16)
            iwt_ref[0] = t.T[IDX_DIM:IDX_DIM + SUBLANES, :] * (IDX_HEADS ** -0.5 * IDX_DIM ** -0.5)

    first = [U_LAST - 1, U_CKV]
    for unit in first + [u for u in range(0, N_UNITS, 2) if u not in first]:
        acc = proj(unit)
        finish(unit, acc[:, :LANES])
        finish(unit + 1, acc[:, LANES:])


def _mix_a_bias_tables(tq, tk):
    assert tq == LANES
    o_last = PATTERNS[1][0] + tk - LANES
    offsets = list(range(0, o_last + 1, LANES)) + [o_last + LANES]
    qi = np.arange(tq)[None, :]
    kj = np.arange(tk)[:, None]
    tables = []
    for o in offsets:
        d = o + qi - kj
        mult = np.zeros(d.shape, np.int64)
        for window, dil in PATTERNS:
            mult += ((d >= 0) & (d % dil == 0) & (d <= window)).astype(np.int64)
        tables.append(np.where(mult > 0, np.log2(np.maximum(mult, 1)), NEG_BIG))
    return np.stack(tables).astype(np.float32)


def _mix_a_kernel(n_tables, qa_ref, ka_ref, vat_ref, zat_ref, gain_ref, bias_ref,
                  out_ref, qm_ref, s_ref, pt_ref, acc_ref):
    tq = qa_ref.shape[1]
    n_grp = tq // LANES
    j = pl.program_id(1)
    long_groups = [g for g in range(n_grp) if (g * LANES + LANES - 1) // KBLK > 0]
    assert tq <= 2 * KBLK and (not long_groups or tq % KBLK == 0)
    n_main = (j * tq + LANES - 1) // KBLK + 1
    chains = [(g, p) for g in range(n_grp) for p in range(N_PAIRS)]

    def group(g):
        return slice(g * LANES, (g + 1) * LANES)

    def last_block(g):
        return n_main if g in long_groups else n_main - 1

    for g, p in chains:
        qm_ref[g, 2 * p * LANES:(2 * p + 1) * LANES, :], qm_ref[g, (2 * p + 1) * LANES:(2 * p + 2) * LANES, :] = (
            _split_head_pair(qa_ref[0, group(g), p * LANES:(p + 1) * LANES]))
    acc_ref[...] = jnp.zeros_like(acc_ref)
    pt_ref[...] = jnp.zeros_like(pt_ref)

    def scores(kb, g, p):
        k0 = pl.multiple_of(kb * KBLK, KBLK)
        return _dot_nt(ka_ref[0, pl.ds(k0, KBLK), p * LANES:(p + 1) * LANES],
                       qm_ref[g, 2 * p * LANES:(2 * p + 2) * LANES, :])

    def accumulate(kb, g, p, alpha):
        vt = vat_ref[0, kb, p * LANES:(p + 1) * LANES, :]
        r0 = p * LANES
        acc_ref[r0:r0 + HEAD_DIM, group(g)] = (acc_ref[r0:r0 + HEAD_DIM, group(g)] * alpha[:, :LANES]
                                               + _dot(vt[:HEAD_DIM], pt_ref[g, p, :, :LANES]))
        acc_ref[r0 + HEAD_DIM:r0 + LANES, group(g)] = (acc_ref[r0 + HEAD_DIM:r0 + LANES, group(g)] * alpha[:, LANES:]
                                                       + _dot(vt[HEAD_DIM:], pt_ref[g, p, :, LANES:]))

    for g, p in chains:
        s_ref[g, p] = scores(0, g, p)

    def softmax_block(kb, g, p, m, l):
        bias = bias_ref[jnp.minimum((j * tq + g * LANES - kb * KBLK) // LANES, n_tables - 1)]
        s = s_ref[g, p] + jnp.concatenate([bias, bias], axis=1)
        m_new = jnp.maximum(m, jnp.max(s, axis=0, keepdims=True))
        alpha = jnp.exp2(m - m_new)
        pt = jnp.exp2(s - m_new)
        pt_ref[g, p] = pt.astype(BF16)
        return m_new, alpha * l + jnp.sum(pt, axis=0, keepdims=True), alpha

    def body(kb, carry):
        ms, ls, alphas = carry
        prev = jnp.maximum(kb - 1, 0)
        stats = []
        for c, (g, p) in enumerate(chains):
            accumulate(prev, g, p, alphas[c])
            stats.append(softmax_block(kb, g, p, ms[c], ls[c]))
            s_ref[g, p] = scores(jnp.minimum(kb + 1, last_block(g)), g, p)
        return tuple(zip(*stats))

    init = (tuple(jnp.full((1, 2 * LANES), NEG_BIG, F32) for _ in chains),
            tuple(jnp.zeros((1, 2 * LANES), F32) for _ in chains),
            tuple(jnp.ones((1, 2 * LANES), F32) for _ in chains))
    ms, ls, alphas = lax.fori_loop(0, n_main, body, init)
    ls, alphas = list(ls), list(alphas)
    for c, (g, p) in enumerate(chains):
        accumulate(n_main - 1, g, p, alphas[c])
        if g in long_groups:
            _, ls[c], alphas[c] = softmax_block(n_main, g, p, ms[c], ls[c])
            accumulate(n_main, g, p, alphas[c])
        r0 = p * LANES
        acc_ref[r0:r0 + HEAD_DIM, group(g)] = acc_ref[r0:r0 + HEAD_DIM, group(g)] / ls[c][:, :LANES]
        acc_ref[r0 + HEAD_DIM:r0 + LANES, group(g)] = acc_ref[r0 + HEAD_DIM:r0 + LANES, group(g)] / ls[c][:, LANES:]

    z = zat_ref[0].astype(F32)
    g = acc_ref[...] * (z / (1.0 + jnp.exp(-z)))
    ms = jnp.mean(g * g, axis=0, keepdims=True)
    out_ref[0] = (g * lax.rsqrt(ms + EPS) * gain_ref[...]).astype(BF16)


def _mix_b_kernel(topk, qb_ref, iq_ref, iwt_ref, zbt_ref, ikd_ref, kbd_ref, ckvt_ref, wuvt_ref,
                  gain_ref, tri_ref, out_ref, sc_ref, bias_ref, iqm_ref, rhs_ref, raw_ref, rank_ref,
                  s_ref, pt_ref, acc_ref, g_ref):
    tq = qb_ref.shape[1]
    n_grp = tq // LANES
    iw_cols = IDX_HEADS * LANES
    hd_cols = N_HEADS * LANES
    j = pl.program_id(1)
    groups = list(range(n_grp))
    long_groups = [g for g in groups if (g * LANES + LANES - 1) // KBLK > 0]
    short_groups = [g for g in groups if g not in long_groups]
    assert tq <= 2 * KBLK and (not long_groups or tq % KBLK == 0)
    long_lanes = slice(len(short_groups) * LANES, tq)
    n_main = (j * tq + LANES - 1) // KBLK + 1
    n_kb = n_main + (1 if long_groups else 0)
    row = lax.broadcasted_iota(I32, (KBLK, tq), 0)
    t_pos = j * tq + lax.broadcasted_iota(I32, (KBLK, tq), 1)

    def key_rows(ref, kb):
        return ref[0, pl.ds(pl.multiple_of(kb * KBLK, KBLK), KBLK), :]

    def next_block_of(kb, g):
        return jnp.minimum(kb + 1, (n_kb if g in long_groups else n_main) - 1)

    def group(g):
        return slice(g * LANES, (g + 1) * LANES)


    for g in range(n_grp):
        for p in range(IDX_HEADS // 2):
            iqm_ref[g, 2 * p * LANES:(2 * p + 1) * LANES, :], iqm_ref[g, (2 * p + 1) * LANES:(2 * p + 2) * LANES, :] = (
                _split_head_pair(iq_ref[0, group(g), p * LANES:(p + 1) * LANES]))

    def store_raw_scores(kb, g):
        raw_ref[:, g * iw_cols:(g + 1) * iw_cols] = _dot_nt(key_rows(ikd_ref, kb), iqm_ref[g])

    def finish_scores(kb, g):
        c0 = g * iw_cols
        sc = jnp.maximum(raw_ref[:, c0:c0 + LANES], 0.0) * iwt_ref[0, 0:1, group(g)]
        for h in range(1, IDX_HEADS):
            sc = sc + (jnp.maximum(raw_ref[:, c0 + h * LANES:c0 + (h + 1) * LANES], 0.0)
                       * iwt_ref[0, h:h + 1, group(g)])
        causal = (kb * KBLK + lax.broadcasted_iota(I32, (KBLK, LANES), 0)
                  <= j * tq + g * LANES + lax.broadcasted_iota(I32, (KBLK, LANES), 1))
        sc_ref[kb, :, group(g)] = jnp.where(causal, sc, -jnp.inf)

    for g in groups:
        store_raw_scores(0, g)

    def score_body(kb, carry):
        for g in groups:
            finish_scores(kb, g)
            store_raw_scores(next_block_of(kb, g), g)
        return carry

    lax.fori_loop(0, n_main, score_body, 0)
    if long_groups:
        for g in long_groups:
            finish_scores(n_main, g)
        sc_ref[n_main, :, :long_lanes.start] = jnp.full((KBLK, long_lanes.start), -jnp.inf, F32)

    def count(pred):
        def partial(hit):
            return jnp.sum(hit.astype(I32).reshape(KBLK // SUBLANES, SUBLANES, hit.shape[1]), axis=0)
        acc = lax.fori_loop(0, n_main, lambda kb, acc: acc + partial(pred(sc_ref[kb], slice(None))),
                            jnp.zeros((SUBLANES, tq), I32))
        if long_groups:
            tail = partial(pred(sc_ref[n_main, :, long_lanes], long_lanes))
            acc = jnp.concatenate([acc[:, :long_lanes.start], acc[:, long_lanes] + tail], axis=1)
        return jnp.sum(acc, axis=0, keepdims=True)

    def key_to_float(key):
        bits = key ^ ((key >> (F32_BITS - 1)) & jnp.int32(F32_MAGNITUDE_MASK))
        return lax.bitcast_convert_type(bits, F32)

    def value_step(i, carry):
        prefix, n_ge = carry
        cand = prefix + (jnp.int32(1) << (F32_BITS - 1 - i))
        thr = key_to_float(cand)
        cnt = count(lambda s, lanes: s >= thr[:, lanes])
        accept = cnt >= topk
        return jnp.where(accept, cand, prefix), jnp.where(accept, cnt, n_ge)

    prefix, n_ge = lax.fori_loop(0, F32_BITS, value_step,
                                 (jnp.full((1, tq), -2 ** (F32_BITS - 1), I32), jnp.zeros((1, tq), I32)))
    thr = key_to_float(prefix)
    surplus = (n_ge - topk).astype(F32)

    def tie_ranks(kb):
        tie = jnp.where(sc_ref[kb] == thr, 1.0, 0.0).astype(BF16)
        return _dot(tri_ref[...], tie)

    per_trip = 2 if long_groups else 1
    for u in range(per_trip):
        rank_ref[u] = tie_ranks(n_kb - 1 - u)

    def mask_loop(has_short_rows):
        def bias_body(trip, ties_after):
            for u in range(per_trip):
                kb = n_kb - 1 - (trip * per_trip + u)
                sc = sc_ref[kb]
                rank = rank_ref[u]
                sel = (sc > thr) | ((sc == thr) & (rank + ties_after > surplus))
                if has_short_rows:
                    sel = (sel | (t_pos < topk)) & (kb * KBLK + row <= t_pos)
                bias_ref[kb] = jnp.where(sel, 0.0, NEG_BIG).astype(BF16)
                rank_ref[u] = tie_ranks(jnp.maximum(kb - per_trip, (n_kb - 1 - u) % per_trip))
                ties_after = ties_after + rank[0:1, :]
            return ties_after

        lax.fori_loop(0, n_kb // per_trip, bias_body, jnp.zeros((1, tq), F32))

    pl.when(j * tq < topk)(lambda: mask_loop(True))
    pl.when(j * tq >= topk)(lambda: mask_loop(False))

    eye = (lax.broadcasted_iota(I32, (LANES, LANES), 0)
           == lax.broadcasted_iota(I32, (LANES, LANES), 1)).astype(BF16)
    for g in range(n_grp):
        for p in range(N_PAIRS):
            rhs_ref[g, 2 * p * LANES:(2 * p + 1) * LANES, :LANES], rhs_ref[g, (2 * p + 1) * LANES:(2 * p + 2) * LANES, :LANES] = (
                _split_head_pair(qb_ref[0, group(g), p * LANES:(p + 1) * LANES]))
        for h in range(N_HEADS):
            rhs_ref[g, h * LANES:(h + 1) * LANES, LANES:] = eye
    acc_ref[...] = jnp.zeros_like(acc_ref)
    pt_ref[...] = jnp.zeros_like(pt_ref)

    def cols(g):
        return slice(g * hd_cols, (g + 1) * hd_cols)

    def store_masked_scores(kb, g):
        lhs = jnp.concatenate([key_rows(kbd_ref, kb), bias_ref[kb, :, group(g)]], axis=1)
        s_ref[:, cols(g)] = _dot_nt(lhs, rhs_ref[g])

    def accumulate(kb, g, alpha):
        acc_ref[:, cols(g)] = acc_ref[:, cols(g)] * alpha + _dot(ckvt_ref[0, kb], pt_ref[:, cols(g)])

    for g in range(n_grp):
        store_masked_scores(0, g)

    def softmax_block(g, m, l):
        s = s_ref[:, cols(g)]
        m_new = jnp.maximum(m, jnp.max(s, axis=0, keepdims=True))
        alpha = jnp.exp2(m - m_new)
        pt = jnp.exp2(s - m_new)
        pt_ref[:, cols(g)] = pt.astype(BF16)
        return m_new, alpha * l + jnp.sum(pt, axis=0, keepdims=True), alpha

    def attn_body(kb, carry):
        ms, ls, alphas = carry
        prev = jnp.maximum(kb - 1, 0)
        stats = []
        for g in groups:
            accumulate(prev, g, alphas[g])
            stats.append(softmax_block(g, ms[g], ls[g]))
            store_masked_scores(next_block_of(kb, g), g)
        return tuple(zip(*stats))

    init = (tuple(jnp.full((1, hd_cols), NEG_BIG, F32) for _ in groups),
            tuple(jnp.zeros((1, hd_cols), F32) for _ in groups),
            tuple(jnp.ones((1, hd_cols), F32) for _ in groups))
    ms, ls, alphas = lax.fori_loop(0, n_main, attn_body, init)
    ls, alphas = list(ls), list(alphas)
    for g in groups:
        accumulate(n_main - 1, g, alphas[g])
        if g in long_groups:
            _, ls[g], alphas[g] = softmax_block(g, ms[g], ls[g])
            accumulate(n_main, g, alphas[g])
        o_lat = (acc_ref[:, cols(g)] / ls[g]).astype(BF16)
        for h in range(N_HEADS):
            g_ref[h * HEAD_DIM:(h + 1) * HEAD_DIM, group(g)] = _dot(wuvt_ref[h], o_lat[:, h * LANES:(h + 1) * LANES])

    z = zbt_ref[0].astype(F32)
    gated = g_ref[...] * (z / (1.0 + jnp.exp(-z)))
    ms = jnp.mean(gated * gated, axis=0, keepdims=True)
    out_ref[0] = (gated * lax.rsqrt(ms + EPS) * gain_ref[...]).astype(BF16)


def _out_kernel(yat_ref, ybt_ref, x_ref, w_ref, gain_ref, out_ref):
    y = jnp.concatenate([yat_ref[0].astype(F32).T, ybt_ref[0].astype(F32).T], axis=1).astype(BF16)
    h = x_ref[0] + _dot(y, w_ref[...])
    ms = jnp.mean(h * h, axis=-1, keepdims=True)
    out_ref[0] = h * lax.rsqrt(ms + EPS) * gain_ref[...]


N_SPLIT = 3


def _rope_trig(positions):
    inv_freq = ROPE_THETA ** (-jnp.arange(0, ROT_DIM, 2, dtype=F32) / ROT_DIM)
    ang = positions.astype(F32)[..., None] * inv_freq
    rest = jnp.concatenate([jnp.cos(ang), jnp.sin(ang)], axis=-1)
    pieces = []
    for _ in range(N_SPLIT):
        pieces.append(rest.astype(BF16))
        rest = rest - pieces[-1].astype(F32)
    return jnp.concatenate(pieces, axis=-1)


def _rope_spread():
    half = ROT_DIM // 2
    c = np.arange(LANES) % HEAD_DIM
    hit = (np.arange(half)[:, None] == (c % half)[None, :])
    spread = np.zeros((2 * half, 3 * LANES), np.float32)
    spread[:half, :LANES] = hit & (c < ROT_DIM)
    spread[half:, LANES:2 * LANES] = -1.0 * (hit & (c < half))
    spread[half:, 2 * LANES:] = hit & (c >= half) & (c < ROT_DIM)
    ones = np.concatenate([(c >= ROT_DIM), np.zeros(2 * LANES, bool)]).astype(np.float32)[None]
    return np.tile(spread, (N_SPLIT, 1)), ones


def _full(shape):
    return pl.BlockSpec(shape, lambda *_: (0,) * len(shape))


@jax.jit
def kernel(x, positions, norm_in_gain, w_in, kv_norm_gain, w_uk, w_uv, idx_k_norm_gain,
           branch_norm_gain_a, branch_norm_gain_b, w_out, final_norm_gain):
    bsz, seq, d_model = x.shape
    assert w_in.shape[0] == 1
    assert seq % KBLK == 0 and seq <= PATTERNS[-1][0]
    n_blk = seq // KBLK
    topk = min(TOPK_MAX, seq // 4)
    tm = min(TM, seq)
    tqa, tqb = TQ_A, TQ_B
    params = pltpu.CompilerParams(dimension_semantics=("arbitrary", "arbitrary"),
                                  vmem_limit_bytes=VMEM_LIMIT)

    trig = _rope_trig(positions)
    spread, rope_ones = _rope_spread()
    w_padded = jnp.pad(w_in[0].T, ((0, D_IN_PADDED - w_in.shape[2]), (0, 0))).astype(BF16)
    wuk_dup = jnp.concatenate([w_uk[0], w_uk[0]], axis=1).astype(BF16)
    gik = jnp.concatenate([idx_k_norm_gain[0], jnp.zeros((LANES - IDX_DIM,), F32)])[None]
    wuv_t = jnp.swapaxes(w_uv[0], 1, 2).astype(BF16)
    gain_a = jnp.broadcast_to(branch_norm_gain_a[0][:, None], (WIDTH, tqa))
    gain_b = jnp.broadcast_to(branch_norm_gain_b[0][:, None], (WIDTH, tqb))

    row_blk = lambda c: pl.BlockSpec((1, tm, c), lambda b, i: (b, i, 0))
    col_blk = lambda c: pl.BlockSpec((1, c, tm), lambda b, i: (b, 0, i))
    key_blk = lambda c: pl.BlockSpec((1, tm // KBLK, c, KBLK), lambda b, i: (b, i, 0, 0))
    qa, ka, vat, zat, qb, zbt, iq, kbd, ckvt, ikd, iwt = pl.pallas_call(
        _proj_kernel,
        grid=(bsz, seq // tm),
        in_specs=[row_blk(d_model), _full((1, d_model)), _full((D_IN_PADDED, d_model)),
                  row_blk(N_SPLIT * ROT_DIM), _full(spread.shape), _full(rope_ones.shape),
                  _full((1, KV_LATENT)), _full((KV_LATENT, LANES)), _full((1, LANES))],
        out_specs=[row_blk(WIDTH), row_blk(WIDTH), key_blk(WIDTH), col_blk(WIDTH),
                   row_blk(WIDTH), col_blk(WIDTH), row_blk(2 * LANES), row_blk(LANES),
                   key_blk(KV_LATENT), row_blk(LANES), col_blk(SUBLANES)],
        out_shape=[jax.ShapeDtypeStruct((bsz, seq, WIDTH), BF16),
                   jax.ShapeDtypeStruct((bsz, seq, WIDTH), BF16),
                   jax.ShapeDtypeStruct((bsz, n_blk, WIDTH, KBLK), BF16),
                   jax.ShapeDtypeStruct((bsz, WIDTH, seq), BF16),
                   jax.ShapeDtypeStruct((bsz, seq, WIDTH), BF16),
                   jax.ShapeDtypeStruct((bsz, WIDTH, seq), BF16),
                   jax.ShapeDtypeStruct((bsz, seq, 2 * LANES), BF16),
                   jax.ShapeDtypeStruct((bsz, seq, LANES), BF16),
                   jax.ShapeDtypeStruct((bsz, n_blk, KV_LATENT, KBLK), BF16),
                   jax.ShapeDtypeStruct((bsz, seq, LANES), BF16),
                   jax.ShapeDtypeStruct((bsz, SUBLANES, seq), F32)],
        compiler_params=params, name="proj",
    )(x, norm_in_gain[0][None], w_padded, trig, jnp.asarray(spread, BF16), jnp.asarray(rope_ones),
      kv_norm_gain[0][None], wuk_dup, gik)

    q_row = lambda tq, c: pl.BlockSpec((1, tq, c), lambda b, i: (b, i, 0))
    q_col = lambda tq, c: pl.BlockSpec((1, c, tq), lambda b, i: (b, 0, i))
    per_b3 = lambda s1, s2: pl.BlockSpec((1, s1, s2), lambda b, i: (b, 0, 0))
    per_b4 = lambda s1, s2, s3: pl.BlockSpec((1, s1, s2, s3), lambda b, i: (b, 0, 0, 0))

    bias_np = _mix_a_bias_tables(LANES, KBLK)
    n_grp = tqa // LANES
    yat = pl.pallas_call(
        functools.partial(_mix_a_kernel, bias_np.shape[0]),
        grid=(bsz, seq // tqa),
        in_specs=[q_row(tqa, WIDTH), per_b3(seq, WIDTH), per_b4(n_blk, WIDTH, KBLK), q_col(tqa, WIDTH),
                  _full((WIDTH, tqa)), _full(bias_np.shape)],
        out_specs=q_col(tqa, WIDTH),
        out_shape=jax.ShapeDtypeStruct((bsz, WIDTH, seq), BF16),
        scratch_shapes=[pltpu.VMEM((n_grp, N_HEADS * LANES, LANES), BF16),
                        pltpu.VMEM((n_grp, N_PAIRS, KBLK, 2 * LANES), F32),
                        pltpu.VMEM((n_grp, N_PAIRS, KBLK, 2 * LANES), BF16),
                        pltpu.VMEM((WIDTH, tqa), F32)],
        compiler_params=params, name="mix_a",
    )(qa, ka, vat, zat, gain_a, jnp.asarray(bias_np))

    n_grp = tqb // LANES
    ybt = pl.pallas_call(
        functools.partial(_mix_b_kernel, topk),
        grid=(bsz, seq // tqb),
        in_specs=[q_row(tqb, WIDTH), q_row(tqb, 2 * LANES), q_col(tqb, SUBLANES), q_col(tqb, WIDTH),
                  per_b3(seq, LANES), per_b3(seq, LANES), per_b4(n_blk, KV_LATENT, KBLK),
                  _full((N_HEADS, HEAD_DIM, KV_LATENT)), _full((WIDTH, tqb)), _full((KBLK, KBLK))],
        out_specs=q_col(tqb, WIDTH),
        out_shape=jax.ShapeDtypeStruct((bsz, WIDTH, seq), BF16),
        scratch_shapes=[pltpu.VMEM((n_blk, KBLK, tqb), F32),
                        pltpu.VMEM((n_blk, KBLK, tqb), BF16),
                        pltpu.VMEM((n_grp, IDX_HEADS * LANES, LANES), BF16),
                        pltpu.VMEM((n_grp, N_HEADS * LANES, 2 * LANES), BF16),
                        pltpu.VMEM((KBLK, n_grp * IDX_HEADS * LANES), F32),
                        pltpu.VMEM((2, KBLK, tqb), F32),
                        pltpu.VMEM((KBLK, n_grp * N_HEADS * LANES), F32),
                        pltpu.VMEM((KBLK, n_grp * N_HEADS * LANES), BF16),
                        pltpu.VMEM((KV_LATENT, n_grp * N_HEADS * LANES), F32),
                        pltpu.VMEM((WIDTH, tqb), F32)],
        compiler_params=params, name="mix_b",
    )(qb, iq, iwt, zbt, ikd, kbd, ckvt, wuv_t, gain_b,
      jnp.asarray(np.triu(np.ones((KBLK, KBLK), np.float32)), BF16))

    return pl.pallas_call(
        _out_kernel,
        grid=(bsz, seq // tm),
        in_specs=[col_blk(WIDTH), col_blk(WIDTH), row_blk(d_model),
                  _full((2 * WIDTH, d_model)), _full((1, d_model))],
        out_specs=row_blk(d_model),
        out_shape=jax.ShapeDtypeStruct((bsz, seq, d_model), F32),
        compiler_params=params, name="out_proj",
    )(yat, ybt, x, w_out[0].astype(BF16), final_norm_gain[None])
```

```python
import functools
import math

import numpy as np
import jax
import jax.numpy as jnp
from jax import lax
from jax.experimental import pallas as pl
from jax.experimental.pallas import tpu as pltpu

F32 = jnp.float32
BF16 = jnp.bfloat16
I32 = jnp.int32

HEAD_DIM = 64
N_HEADS = 8
WIDTH = N_HEADS * HEAD_DIM
N_PAIRS = N_HEADS // 2
ROT_DIM = HEAD_DIM // 4
ROPE_THETA = 500000.0
KV_LATENT = 128
IDX_HEADS = 4
IDX_DIM = 64
TOPK_MAX = 256
EPS = 1e-6
PATTERNS = ((128, 1), (512, 4), (2048, 16))

LANES = 128
SUBLANES = 8
MXU_DEPTH = 256
KBLK = MXU_DEPTH
TQ_A = 4 * LANES
TQ_B = 4 * LANES
TM = 1024
OUT_ROWS = 256
NEG_BIG = -1e30
F32_BITS = 32
F32_MAGNITUDE_MASK = 0x7FFFFFFF
VMEM_LIMIT = 56 * 1024 * 1024
Q_SCALE = HEAD_DIM ** -0.5 * math.log2(math.e)

U_QA, U_KA, U_VA, U_ZA, U_QB, U_CKV, U_ZB, U_IQ, U_LAST, N_UNITS = 0, 4, 8, 12, 16, 20, 21, 25, 27, 28
D_IN_PADDED = N_UNITS * LANES


def _dot(a, b):
    return jnp.dot(a, b, preferred_element_type=F32)


def _dot_nt(a, b):
    return lax.dot_general(a, b, (((1,), (1,)), ((), ())), preferred_element_type=F32)


def _split_head_pair(xp):
    lane = lax.broadcasted_iota(I32, xp.shape, 1)
    zero = jnp.zeros_like(xp)
    return jnp.where(lane < HEAD_DIM, xp, zero), jnp.where(lane >= HEAD_DIM, xp, zero)


def _proj_kernel(x_ref, gin_ref, w_ref, trig_ref, spread_ref, ones_ref, gkv_ref, wuk_ref, gik_ref,
                 qa_ref, ka_ref, vat_ref, zat_ref, qb_ref, zbt_ref, iq_ref, kbd_ref, ckvt_ref,
                 ikd_ref, iwt_ref):
    tm = x_ref.shape[1]
    nblk = tm // KBLK
    x = x_ref[0]
    ms = jnp.mean(x * x, axis=-1, keepdims=True)
    u = (x * lax.rsqrt(ms + EPS) * gin_ref[...]).astype(BF16)
    table = _dot(trig_ref[0], spread_ref[...]) + ones_ref[...]
    cosf, sina, sinb = (table[:, i * LANES:(i + 1) * LANES] for i in range(3))

    def rope(t):
        return t * cosf + pltpu.roll(t, LANES - ROT_DIM // 2, 1) * sina + pltpu.roll(t, ROT_DIM // 2, 1) * sinb

    def proj(unit):
        return _dot_nt(u, w_ref[unit * LANES:(unit + 2) * LANES, :])

    def store_t(ref, unit, t):
        tt = t.T.astype(BF16)
        for i in range(nblk):
            ref[0, i, unit * LANES:(unit + 1) * LANES, :] = tt[:, i * KBLK:(i + 1) * KBLK]

    def lanes(unit):
        return slice(unit * LANES, (unit + 1) * LANES)

    def finish(unit, t):
        if unit < U_KA:
            qa_ref[0, :, lanes(unit - U_QA)] = (rope(t) * Q_SCALE).astype(BF16)
        elif unit < U_VA:
            ka_ref[0, :, lanes(unit - U_KA)] = rope(t).astype(BF16)
        elif unit < U_ZA:
            store_t(vat_ref, unit - U_VA, t)
        elif unit < U_QB:
            zat_ref[0, lanes(unit - U_ZA), :] = t.T.astype(BF16)
        elif unit < U_CKV:
            qb_ref[0, :, lanes(unit - U_QB)] = (rope(t) * Q_SCALE).astype(BF16)
        elif unit == U_CKV:
            ckv = t * lax.rsqrt(jnp.mean(t * t, axis=-1, keepdims=True) + EPS) * gkv_ref[...]
            store_t(ckvt_ref, 0, ckv)
            kbd_ref[0] = rope(_dot(ckv.astype(BF16), wuk_ref[...])).astype(BF16)
        elif unit < U_IQ:
            zbt_ref[0, lanes(unit - U_ZB), :] = t.T.astype(BF16)
        elif unit < U_LAST:
            iq_ref[0, :, lanes(unit - U_IQ)] = rope(t).astype(BF16)
        else:
            lane = lax.broadcasted_iota(I32, t.shape, 1)
            ik_ms = jnp.sum(jnp.where(lane < IDX_DIM, t * t, 0.0), axis=-1, keepdims=True) / IDX_DIM
            ik = rope(t * lax.rsqrt(ik_ms + EPS) * gik_ref[...])
            ikd_ref[0] = (ik + pltpu.roll(ik, IDX_DIM, 1)).astype(BF16)
            iwt_ref[0] = t.T[IDX_DIM:IDX_DIM + SUBLANES, :] * (IDX_HEADS ** -0.5 * IDX_DIM ** -0.5)

    first = [U_LAST - 1, U_CKV, U_QA, U_QA + 2, U_KA, U_KA + 2, U_QB, U_QB + 2, U_ZB + 3]
    for unit in first + [u for u in range(0, N_UNITS, 2) if u not in first]:
        acc = proj(unit)
        finish(unit, acc[:, :LANES])
        finish(unit + 1, acc[:, LANES:])


def _mix_a_bias_tables(tq, tk):
    assert tq == LANES
    o_last = PATTERNS[1][0] + tk - LANES
    offsets = list(range(0, o_last + 1, LANES)) + [o_last + LANES]
    qi = np.arange(tq)[None, :]
    kj = np.arange(tk)[:, None]
    tables = []
    for o in offsets:
        d = o + qi - kj
        mult = np.zeros(d.shape, np.int64)
        for window, dil in PATTERNS:
            mult += ((d >= 0) & (d % dil == 0) & (d <= window)).astype(np.int64)
        tables.append(np.where(mult > 0, np.log2(np.maximum(mult, 1)), NEG_BIG))
    return np.stack(tables).astype(np.float32)


def _mix_a_kernel(n_tables, qa_ref, ka_ref, vat_ref, zat_ref, gain_ref, bias_ref,
                  out_ref, qm_ref, s_ref, pt_ref, acc_ref):
    tq = qa_ref.shape[1]
    n_grp = tq // LANES
    j = pl.program_id(1)
    long_groups = [g for g in range(n_grp) if (g * LANES + LANES - 1) // KBLK > 0]
    assert tq <= 2 * KBLK and (not long_groups or tq % KBLK == 0)
    n_main = (j * tq + LANES - 1) // KBLK + 1
    chains = [(g, p) for g in range(n_grp) for p in range(N_PAIRS)]

    def group(g):
        return slice(g * LANES, (g + 1) * LANES)

    def last_block(g):
        return n_main if g in long_groups else n_main - 1

    for g, p in chains:
        qm_ref[g, 2 * p * LANES:(2 * p + 1) * LANES, :], qm_ref[g, (2 * p + 1) * LANES:(2 * p + 2) * LANES, :] = (
            _split_head_pair(qa_ref[0, group(g), p * LANES:(p + 1) * LANES]))
    acc_ref[...] = jnp.zeros_like(acc_ref)
    pt_ref[...] = jnp.zeros_like(pt_ref)

    def scores(kb, g, p):
        k0 = pl.multiple_of(kb * KBLK, KBLK)
        return _dot_nt(ka_ref[0, pl.ds(k0, KBLK), p * LANES:(p + 1) * LANES],
                       qm_ref[g, 2 * p * LANES:(2 * p + 2) * LANES, :])

    def accumulate(kb, g, p, alpha):
        vt = vat_ref[0, kb, p * LANES:(p + 1) * LANES, :]
        r0 = p * LANES
        acc_ref[r0:r0 + HEAD_DIM, group(g)] = (acc_ref[r0:r0 + HEAD_DIM, group(g)] * alpha[:, :LANES]
                                               + _dot(vt[:HEAD_DIM], pt_ref[g, p, :, :LANES]))
        acc_ref[r0 + HEAD_DIM:r0 + LANES, group(g)] = (acc_ref[r0 + HEAD_DIM:r0 + LANES, group(g)] * alpha[:, LANES:]
                                                       + _dot(vt[HEAD_DIM:], pt_ref[g, p, :, LANES:]))

    for g, p in chains:
        s_ref[g, p] = scores(0, g, p)

    def softmax_block(kb, g, p, m, l):
        bias = bias_ref[jnp.minimum((j * tq + g * LANES - kb * KBLK) // LANES, n_tables - 1)]
        s = s_ref[g, p] + jnp.concatenate([bias, bias], axis=1)
        m_new = jnp.maximum(m, jnp.max(s, axis=0, keepdims=True))
        alpha = jnp.exp2(m - m_new)
        pt = jnp.exp2(s - m_new)
        pt_ref[g, p] = pt.astype(BF16)
        return m_new, alpha * l + jnp.sum(pt, axis=0, keepdims=True), alpha

    def body(kb, carry):
        ms, ls, alphas = carry
        prev = jnp.maximum(kb - 1, 0)
        stats = []
        for c, (g, p) in enumerate(chains):
            accumulate(prev, g, p, alphas[c])
            stats.append(softmax_block(kb, g, p, ms[c], ls[c]))
            s_ref[g, p] = scores(jnp.minimum(kb + 1, last_block(g)), g, p)
        return tuple(zip(*stats))

    init = (tuple(jnp.full((1, 2 * LANES), NEG_BIG, F32) for _ in chains),
            tuple(jnp.zeros((1, 2 * LANES), F32) for _ in chains),
            tuple(jnp.ones((1, 2 * LANES), F32) for _ in chains))
    ms, ls, alphas = lax.fori_loop(0, n_main, body, init)
    ls, alphas = list(ls), list(alphas)
    for c, (g, p) in enumerate(chains):
        accumulate(n_main - 1, g, p, alphas[c])
        if g in long_groups:
            _, ls[c], alphas[c] = softmax_block(n_main, g, p, ms[c], ls[c])
            accumulate(n_main, g, p, alphas[c])
        r0 = p * LANES
        acc_ref[r0:r0 + HEAD_DIM, group(g)] = acc_ref[r0:r0 + HEAD_DIM, group(g)] / ls[c][:, :LANES]
        acc_ref[r0 + HEAD_DIM:r0 + LANES, group(g)] = acc_ref[r0 + HEAD_DIM:r0 + LANES, group(g)] / ls[c][:, LANES:]

    z = zat_ref[0].astype(F32)
    g = acc_ref[...] * (z / (1.0 + jnp.exp(-z)))
    ms = jnp.mean(g * g, axis=0, keepdims=True)
    out_ref[0] = (g * lax.rsqrt(ms + EPS) * gain_ref[...]).astype(BF16)


def _mix_b_kernel(topk, qb_ref, iq_ref, iwt_ref, zbt_ref, ikd_ref, kbd_ref, ckvt_ref, wuvt_ref,
                  gain_ref, tri_ref, out_ref, sc_ref, bias_ref, iqm_ref, rhs_ref, raw_ref, rank_ref,
                  s_ref, pt_ref, acc_ref, g_ref):
    tq = qb_ref.shape[1]
    n_grp = tq // LANES
    iw_cols = IDX_HEADS * LANES
    hd_cols = N_HEADS * LANES
    j = pl.program_id(1)
    groups = list(range(n_grp))
    long_groups = [g for g in groups if (g * LANES + LANES - 1) // KBLK > 0]
    short_groups = [g for g in groups if g not in long_groups]
    assert tq <= 2 * KBLK and (not long_groups or tq % KBLK == 0)
    long_lanes = slice(len(short_groups) * LANES, tq)
    n_main = (j * tq + LANES - 1) // KBLK + 1
    n_kb = n_main + (1 if long_groups else 0)
    row = lax.broadcasted_iota(I32, (KBLK, tq), 0)
    t_pos = j * tq + lax.broadcasted_iota(I32, (KBLK, tq), 1)

    def key_rows(ref, kb):
        return ref[0, pl.ds(pl.multiple_of(kb * KBLK, KBLK), KBLK), :]

    def next_block_of(kb, g):
        return jnp.minimum(kb + 1, (n_kb if g in long_groups else n_main) - 1)

    def group(g):
        return slice(g * LANES, (g + 1) * LANES)


    for g in range(n_grp):
        for p in range(IDX_HEADS // 2):
            iqm_ref[g, 2 * p * LANES:(2 * p + 1) * LANES, :], iqm_ref[g, (2 * p + 1) * LANES:(2 * p + 2) * LANES, :] = (
                _split_head_pair(iq_ref[0, group(g), p * LANES:(p + 1) * LANES]))

    def store_raw_scores(kb, g):
        raw_ref[:, g * iw_cols:(g + 1) * iw_cols] = _dot_nt(key_rows(ikd_ref, kb), iqm_ref[g])

    def finish_scores(kb, g):
        c0 = g * iw_cols
        sc = jnp.maximum(raw_ref[:, c0:c0 + LANES], 0.0) * iwt_ref[0, 0:1, group(g)]
        for h in range(1, IDX_HEADS):
            sc = sc + (jnp.maximum(raw_ref[:, c0 + h * LANES:c0 + (h + 1) * LANES], 0.0)
                       * iwt_ref[0, h:h + 1, group(g)])
        causal = (kb * KBLK + lax.broadcasted_iota(I32, (KBLK, LANES), 0)
                  <= j * tq + g * LANES + lax.broadcasted_iota(I32, (KBLK, LANES), 1))
        sc_ref[kb, :, group(g)] = jnp.where(causal, sc, -jnp.inf)

    for g in groups:
        store_raw_scores(0, g)

    def score_body(kb, carry):
        for g in groups:
            finish_scores(kb, g)
            store_raw_scores(next_block_of(kb, g), g)
        return carry

    lax.fori_loop(0, n_main, score_body, 0)
    if long_groups:
        for g in long_groups:
            finish_scores(n_main, g)
        sc_ref[n_main, :, :long_lanes.start] = jnp.full((KBLK, long_lanes.start), -jnp.inf, F32)

    def count(pred):
        def partial(hit):
            return jnp.sum(hit.astype(I32).reshape(KBLK // SUBLANES, SUBLANES, hit.shape[1]), axis=0)
        acc = lax.fori_loop(0, n_main, lambda kb, acc: acc + partial(pred(sc_ref[kb], slice(None))),
                            jnp.zeros((SUBLANES, tq), I32))
        if long_groups:
            tail = partial(pred(sc_ref[n_main, :, long_lanes], long_lanes))
            acc = jnp.concatenate([acc[:, :long_lanes.start], acc[:, long_lanes] + tail], axis=1)
        return jnp.sum(acc, axis=0, keepdims=True)

    def key_to_float(key):
        bits = key ^ ((key >> (F32_BITS - 1)) & jnp.int32(F32_MAGNITUDE_MASK))
        return lax.bitcast_convert_type(bits, F32)

    def value_step(i, carry):
        prefix, n_ge = carry
        cand = prefix + (jnp.int32(1) << (F32_BITS - 1 - i))
        thr = key_to_float(cand)
        cnt = count(lambda s, lanes: s >= thr[:, lanes])
        accept = cnt >= topk
        return jnp.where(accept, cand, prefix), jnp.where(accept, cnt, n_ge)

    prefix, n_ge = lax.fori_loop(0, F32_BITS, value_step,
                                 (jnp.full((1, tq), -2 ** (F32_BITS - 1), I32), jnp.zeros((1, tq), I32)))
    thr = key_to_float(prefix)
    surplus = (n_ge - topk).astype(F32)

    def tie_ranks(kb):
        tie = jnp.where(sc_ref[kb] == thr, 1.0, 0.0).astype(BF16)
        return _dot(tri_ref[...], tie)

    per_trip = 2 if long_groups else 1
    for u in range(per_trip):
        rank_ref[u] = tie_ranks(n_kb - 1 - u)

    def mask_loop(has_short_rows):
        def bias_body(trip, ties_after):
            for u in range(per_trip):
                kb = n_kb - 1 - (trip * per_trip + u)
                sc = sc_ref[kb]
                rank = rank_ref[u]
                sel = (sc > thr) | ((sc == thr) & (rank + ties_after > surplus))
                if has_short_rows:
                    sel = (sel | (t_pos < topk)) & (kb * KBLK + row <= t_pos)
                bias_ref[kb] = jnp.where(sel, 0.0, NEG_BIG).astype(BF16)
                rank_ref[u] = tie_ranks(jnp.maximum(kb - per_trip, (n_kb - 1 - u) % per_trip))
                ties_after = ties_after + rank[0:1, :]
            return ties_after

        lax.fori_loop(0, n_kb // per_trip, bias_body, jnp.zeros((1, tq), F32))

    pl.when(j * tq < topk)(lambda: mask_loop(True))
    pl.when(j * tq >= topk)(lambda: mask_loop(False))

    eye = (lax.broadcasted_iota(I32, (LANES, LANES), 0)
           == lax.broadcasted_iota(I32, (LANES, LANES), 1)).astype(BF16)
    for g in range(n_grp):
        for p in range(N_PAIRS):
            rhs_ref[g, 2 * p * LANES:(2 * p + 1) * LANES, :LANES], rhs_ref[g, (2 * p + 1) * LANES:(2 * p + 2) * LANES, :LANES] = (
                _split_head_pair(qb_ref[0, group(g), p * LANES:(p + 1) * LANES]))
        for h in range(N_HEADS):
            rhs_ref[g, h * LANES:(h + 1) * LANES, LANES:] = eye
    acc_ref[...] = jnp.zeros_like(acc_ref)
    pt_ref[...] = jnp.zeros_like(pt_ref)

    def cols(g):
        return slice(g * hd_cols, (g + 1) * hd_cols)

    def store_masked_scores(kb, g):
        lhs = jnp.concatenate([key_rows(kbd_ref, kb), bias_ref[kb, :, group(g)]], axis=1)
        s_ref[:, cols(g)] = _dot_nt(lhs, rhs_ref[g])

    def accumulate(kb, g, alpha):
        acc_ref[:, cols(g)] = acc_ref[:, cols(g)] * alpha + _dot(ckvt_ref[0, kb], pt_ref[:, cols(g)])

    for g in range(n_grp):
        store_masked_scores(0, g)

    def softmax_block(g, m, l):
        s = s_ref[:, cols(g)]
        m_new = jnp.maximum(m, jnp.max(s, axis=0, keepdims=True))
        alpha = jnp.exp2(m - m_new)
        pt = jnp.exp2(s - m_new)
        pt_ref[:, cols(g)] = pt.astype(BF16)
        return m_new, alpha * l + jnp.sum(pt, axis=0, keepdims=True), alpha

    def attn_body(kb, carry):
        ms, ls, alphas = carry
        prev = jnp.maximum(kb - 1, 0)
        stats = []
        for g in groups:
            accumulate(prev, g, alphas[g])
            stats.append(softmax_block(g, ms[g], ls[g]))
            store_masked_scores(next_block_of(kb, g), g)
        return tuple(zip(*stats))

    init = (tuple(jnp.full((1, hd_cols), NEG_BIG, F32) for _ in groups),
            tuple(jnp.zeros((1, hd_cols), F32) for _ in groups),
            tuple(jnp.ones((1, hd_cols), F32) for _ in groups))
    ms, ls, alphas = lax.fori_loop(0, n_main, attn_body, init)
    ls, alphas = list(ls), list(alphas)
    for g in groups:
        accumulate(n_main - 1, g, alphas[g])
        if g in long_groups:
            _, ls[g], alphas[g] = softmax_block(g, ms[g], ls[g])
            accumulate(n_main, g, alphas[g])
        o_lat = (acc_ref[:, cols(g)] / ls[g]).astype(BF16)
        for h in range(N_HEADS):
            g_ref[h * HEAD_DIM:(h + 1) * HEAD_DIM, group(g)] = _dot(wuvt_ref[h], o_lat[:, h * LANES:(h + 1) * LANES])

    z = zbt_ref[0].astype(F32)
    gated = g_ref[...] * (z / (1.0 + jnp.exp(-z)))
    ms = jnp.mean(gated * gated, axis=0, keepdims=True)
    out_ref[0] = (gated * lax.rsqrt(ms + EPS) * gain_ref[...]).astype(BF16)


def _out_kernel(yat_ref, ybt_ref, x_ref, w_ref, gain_ref, out_ref):
    for r0 in range(0, x_ref.shape[1], OUT_ROWS):
        rows = slice(r0, r0 + OUT_ROWS)
        y = jnp.concatenate([yat_ref[0, :, rows].astype(F32).T, ybt_ref[0, :, rows].astype(F32).T],
                            axis=1).astype(BF16)
        h = x_ref[0, rows, :] + _dot(y, w_ref[...])
        ms = jnp.mean(h * h, axis=-1, keepdims=True)
        out_ref[0, rows, :] = h * lax.rsqrt(ms + EPS) * gain_ref[...]


N_SPLIT = 3


def _rope_trig(positions):
    inv_freq = ROPE_THETA ** (-jnp.arange(0, ROT_DIM, 2, dtype=F32) / ROT_DIM)
    ang = positions.astype(F32)[..., None] * inv_freq
    rest = jnp.concatenate([jnp.cos(ang), jnp.sin(ang)], axis=-1)
    pieces = []
    for _ in range(N_SPLIT):
        pieces.append(rest.astype(BF16))
        rest = rest - pieces[-1].astype(F32)
    return jnp.concatenate(pieces, axis=-1)


def _rope_spread():
    half = ROT_DIM // 2
    c = np.arange(LANES) % HEAD_DIM
    hit = (np.arange(half)[:, None] == (c % half)[None, :])
    spread = np.zeros((2 * half, 3 * LANES), np.float32)
    spread[:half, :LANES] = hit & (c < ROT_DIM)
    spread[half:, LANES:2 * LANES] = -1.0 * (hit & (c < half))
    spread[half:, 2 * LANES:] = hit & (c >= half) & (c < ROT_DIM)
    ones = np.concatenate([(c >= ROT_DIM), np.zeros(2 * LANES, bool)]).astype(np.float32)[None]
    return np.tile(spread, (N_SPLIT, 1)), ones


def _full(shape):
    return pl.BlockSpec(shape, lambda *_: (0,) * len(shape))


@jax.jit
def kernel(x, positions, norm_in_gain, w_in, kv_norm_gain, w_uk, w_uv, idx_k_norm_gain,
           branch_norm_gain_a, branch_norm_gain_b, w_out, final_norm_gain):
    bsz, seq, d_model = x.shape
    assert w_in.shape[0] == 1
    assert seq % KBLK == 0 and seq <= PATTERNS[-1][0]
    n_blk = seq // KBLK
    topk = min(TOPK_MAX, seq // 4)
    tm = min(TM, seq)
    tqa, tqb = TQ_A, TQ_B
    params = pltpu.CompilerParams(dimension_semantics=("arbitrary", "arbitrary"),
                                  vmem_limit_bytes=VMEM_LIMIT)

    trig = _rope_trig(positions)
    spread, rope_ones = _rope_spread()
    w_padded = jnp.pad(w_in[0].T, ((0, D_IN_PADDED - w_in.shape[2]), (0, 0))).astype(BF16)
    wuk_dup = jnp.concatenate([w_uk[0], w_uk[0]], axis=1).astype(BF16)
    gik = jnp.concatenate([idx_k_norm_gain[0], jnp.zeros((LANES - IDX_DIM,), F32)])[None]
    wuv_t = jnp.swapaxes(w_uv[0], 1, 2).astype(BF16)
    gain_a = jnp.broadcast_to(branch_norm_gain_a[0][:, None], (WIDTH, tqa))
    gain_b = jnp.broadcast_to(branch_norm_gain_b[0][:, None], (WIDTH, tqb))

    row_blk = lambda c: pl.BlockSpec((1, tm, c), lambda b, i: (b, i, 0))
    col_blk = lambda c: pl.BlockSpec((1, c, tm), lambda b, i: (b, 0, i))
    key_blk = lambda c: pl.BlockSpec((1, tm // KBLK, c, KBLK), lambda b, i: (b, i, 0, 0))
    qa, ka, vat, zat, qb, zbt, iq, kbd, ckvt, ikd, iwt = pl.pallas_call(
        _proj_kernel,
        grid=(bsz, seq // tm),
        in_specs=[row_blk(d_model), _full((1, d_model)), _full((D_IN_PADDED, d_model)),
                  row_blk(N_SPLIT * ROT_DIM), _full(spread.shape), _full(rope_ones.shape),
                  _full((1, KV_LATENT)), _full((KV_LATENT, LANES)), _full((1, LANES))],
        out_specs=[row_blk(WIDTH), row_blk(WIDTH), key_blk(WIDTH), col_blk(WIDTH),
                   row_blk(WIDTH), col_blk(WIDTH), row_blk(2 * LANES), row_blk(LANES),
                   key_blk(KV_LATENT), row_blk(LANES), col_blk(SUBLANES)],
        out_shape=[jax.ShapeDtypeStruct((bsz, seq, WIDTH), BF16),
                   jax.ShapeDtypeStruct((bsz, seq, WIDTH), BF16),
                   jax.ShapeDtypeStruct((bsz, n_blk, WIDTH, KBLK), BF16),
                   jax.ShapeDtypeStruct((bsz, WIDTH, seq), BF16),
                   jax.ShapeDtypeStruct((bsz, seq, WIDTH), BF16),
                   jax.ShapeDtypeStruct((bsz, WIDTH, seq), BF16),
                   jax.ShapeDtypeStruct((bsz, seq, 2 * LANES), BF16),
                   jax.ShapeDtypeStruct((bsz, seq, LANES), BF16),
                   jax.ShapeDtypeStruct((bsz, n_blk, KV_LATENT, KBLK), BF16),
                   jax.ShapeDtypeStruct((bsz, seq, LANES), BF16),
                   jax.ShapeDtypeStruct((bsz, SUBLANES, seq), F32)],
        compiler_params=params, name="proj",
    )(x, norm_in_gain[0][None], w_padded, trig, jnp.asarray(spread, BF16), jnp.asarray(rope_ones),
      kv_norm_gain[0][None], wuk_dup, gik)

    q_row = lambda tq, c: pl.BlockSpec((1, tq, c), lambda b, i: (b, i, 0))
    q_col = lambda tq, c: pl.BlockSpec((1, c, tq), lambda b, i: (b, 0, i))
    per_b3 = lambda s1, s2: pl.BlockSpec((1, s1, s2), lambda b, i: (b, 0, 0))
    per_b4 = lambda s1, s2, s3: pl.BlockSpec((1, s1, s2, s3), lambda b, i: (b, 0, 0, 0))

    bias_np = _mix_a_bias_tables(LANES, KBLK)
    n_grp = tqa // LANES
    yat = pl.pallas_call(
        functools.partial(_mix_a_kernel, bias_np.shape[0]),
        grid=(bsz, seq // tqa),
        in_specs=[q_row(tqa, WIDTH), per_b3(seq, WIDTH), per_b4(n_blk, WIDTH, KBLK), q_col(tqa, WIDTH),
                  _full((WIDTH, tqa)), _full(bias_np.shape)],
        out_specs=q_col(tqa, WIDTH),
        out_shape=jax.ShapeDtypeStruct((bsz, WIDTH, seq), BF16),
        scratch_shapes=[pltpu.VMEM((n_grp, N_HEADS * LANES, LANES), BF16),
                        pltpu.VMEM((n_grp, N_PAIRS, KBLK, 2 * LANES), F32),
                        pltpu.VMEM((n_grp, N_PAIRS, KBLK, 2 * LANES), BF16),
                        pltpu.VMEM((WIDTH, tqa), F32)],
        compiler_params=params, name="mix_a",
    )(qa, ka, vat, zat, gain_a, jnp.asarray(bias_np))

    n_grp = tqb // LANES
    ybt = pl.pallas_call(
        functools.partial(_mix_b_kernel, topk),
        grid=(bsz, seq // tqb),
        in_specs=[q_row(tqb, WIDTH), q_row(tqb, 2 * LANES), q_col(tqb, SUBLANES), q_col(tqb, WIDTH),
                  per_b3(seq, LANES), per_b3(seq, LANES), per_b4(n_blk, KV_LATENT, KBLK),
                  _full((N_HEADS, HEAD_DIM, KV_LATENT)), _full((WIDTH, tqb)), _full((KBLK, KBLK))],
        out_specs=q_col(tqb, WIDTH),
        out_shape=jax.ShapeDtypeStruct((bsz, WIDTH, seq), BF16),
        scratch_shapes=[pltpu.VMEM((n_blk, KBLK, tqb), F32),
                        pltpu.VMEM((n_blk, KBLK, tqb), BF16),
                        pltpu.VMEM((n_grp, IDX_HEADS * LANES, LANES), BF16),
                        pltpu.VMEM((n_grp, N_HEADS * LANES, 2 * LANES), BF16),
                        pltpu.VMEM((KBLK, n_grp * IDX_HEADS * LANES), F32),
                        pltpu.VMEM((2, KBLK, tqb), F32),
                        pltpu.VMEM((KBLK, n_grp * N_HEADS * LANES), F32),
                        pltpu.VMEM((KBLK, n_grp * N_HEADS * LANES), BF16),
                        pltpu.VMEM((KV_LATENT, n_grp * N_HEADS * LANES), F32),
                        pltpu.VMEM((WIDTH, tqb), F32)],
        compiler_params=params, name="mix_b",
    )(qb, iq, iwt, zbt, ikd, kbd, ckvt, wuv_t, gain_b,
      jnp.asarray(np.triu(np.ones((KBLK, KBLK), np.float32)), BF16))

    return pl.pallas_call(
        _out_kernel,
        grid=(bsz, seq // tm),
        in_specs=[col_blk(WIDTH), col_blk(WIDTH), row_blk(d_model),
                  _full((2 * WIDTH, d_model)), _full((1, d_model))],
        out_specs=row_blk(d_model),
        out_shape=jax.ShapeDtypeStruct((bsz, seq, d_model), F32),
        compiler_params=params, name="out_proj",
    )(yat, ybt, x, w_out[0].astype(BF16), final_norm_gain[None])
```
